```python
import math
import jax, jax.numpy as jnp
from jax import lax
import numpy as np

D_MODEL = 1024
BATCH = 8
SEQ = 2048
DEPTH = 4
DEC_BATCH = 128
DEC_SEQ = 1
PAST_LEN = 16384
PAGE_SIZE = 128

N_MIXERS = 3
N_SCONV_LAYERS = (DEPTH + 2) // 3
N_RWKV_LAYERS = (DEPTH + 1) // 3
N_GLA_LAYERS = DEPTH // 3
CONV_W = 3
RMS_EPS = 1e-6
N_MOD = 6
RWKV_HEAD = 64
RWKV_HEADS = D_MODEL // RWKV_HEAD
RWKV_DECAY_LORA = 64
RWKV_A_LORA = 64
RWKV_GATE_LORA = 128
RWKV_GN_EPS = 64e-5
GLA_HEADS = 4
GLA_DK_TOTAL = D_MODEL // 2
GLA_DV_TOTAL = D_MODEL
GLA_DK = GLA_DK_TOTAL // GLA_HEADS
GLA_DV = GLA_DV_TOTAL // GLA_HEADS
GLA_GATE_RANK = 16
GLA_GATE_NORMALIZER = 16.0
GLA_CHUNK = 64
D_FF = ((8 * D_MODEL // 3 + 127) // 128) * 128

kernel_name = 'hybrid_sconv_rwkv7_gla_convffn_adaln_step'


def rmsnorm(x, g, eps=RMS_EPS):
    xf = x.astype(jnp.float32)
    y = xf * lax.rsqrt(jnp.mean(xf * xf, axis=-1, keepdims=True) + eps)
    return (y * g.astype(jnp.float32)).astype(x.dtype)


def causal_dwconv(u, buf, w):
    T = u.shape[1]
    full = jnp.concatenate([buf.astype(u.dtype), u], axis=1)
    y = full[:, 0:T] * w[0]
    for j in range(1, CONV_W):
        y = y + full[:, j:j + T] * w[j]
    return y, full[:, T:]


def short_conv_mixer(x, buf, w_in, conv_w, w_out):
    b_gate, c_gate, xi = jnp.split(x @ w_in, 3, axis=-1)
    y, new_buf = causal_dwconv(c_gate * xi, buf, conv_w)
    return (b_gate * y) @ w_out, new_buf


def rwkv7_mixer(x, shift_buf, S0, mu, w_rkv, w0, w1, w2, a0, a1, a2, g1, g2, k_k, k_a, r_k,
                lnx_w, lnx_b, w_o):
    B, T, D = x.shape
    H, N = RWKV_HEADS, RWKV_HEAD
    prev = jnp.concatenate([shift_buf[:, None, :].astype(x.dtype), x[:, :-1]], axis=1)
    xx = prev - x
    xr, xw, xk, xv, xa, xg = [x + xx * mu[i] for i in range(6)]
    r = xr @ w_rkv[0]
    k = xk @ w_rkv[1]
    v = xv @ w_rkv[2]
    logw = -jax.nn.softplus(-(w0 + jnp.tanh(xw @ w1) @ w2)) - 0.5
    a = jax.nn.sigmoid(a0 + (xa @ a1) @ a2)
    g = jax.nn.sigmoid(xg @ g1) @ g2
    hd = lambda t: t.reshape(B, T, H, N).astype(jnp.float32)
    kk = hd(k * k_k)
    kk = kk / jnp.maximum(jnp.sqrt(jnp.sum(kk * kk, axis=-1, keepdims=True)), 1e-12)
    k = k * (1 + (a - 1) * k_a)
    rh, kh, vh, ah = hd(r), hd(k), hd(v), hd(a)
    decay = jnp.exp(-jnp.exp(hd(logw)))

    def step(S, inp):
        r_t, w_t, k_t, v_t, kk_t, a_t = inp
        sa = jnp.einsum('bhvk,bhk->bhv', S, -kk_t)
        S = (S * w_t[:, :, None, :] + sa[..., None] * (kk_t * a_t)[:, :, None, :]
             + v_t[..., None] * k_t[:, :, None, :])
        return S, jnp.einsum('bhvk,bhk->bhv', S, r_t)

    tm = lambda t: jnp.moveaxis(t, 1, 0)
    S, y = lax.scan(step, S0.astype(jnp.float32), (tm(rh), tm(decay), tm(kh), tm(vh), tm(kk), tm(ah)))
    y = jnp.moveaxis(y, 0, 1)
    mean = jnp.mean(y, axis=-1, keepdims=True)
    var = jnp.mean(jnp.square(y - mean), axis=-1, keepdims=True)
    y = ((y - mean) * lax.rsqrt(var + RWKV_GN_EPS)).reshape(B, T, D)
    y = y * lnx_w.astype(jnp.float32) + lnx_b.astype(jnp.float32)
    bonus = jnp.sum(rh * kh * r_k.astype(jnp.float32), axis=-1, keepdims=True) * vh
    y = y + bonus.reshape(B, T, D)
    out = (y.astype(x.dtype) * g) @ w_o
    return out, x[:, -1], S.astype(S0.dtype)


def gla_chunked(q, k, v, lg, S0):
    B, T, H, dk = q.shape
    dv = v.shape[-1]
    L = math.gcd(T, GLA_CHUNK)
    n = T // L

    def blk(t):
        return t.reshape(B, n, L, H, t.shape[-1]).transpose(1, 0, 3, 2, 4).astype(jnp.float32)

    qc, kc, vc = blk(q), blk(k), blk(v)
    G = jnp.cumsum(blk(lg), axis=-2)
    causal = jnp.tril(jnp.ones((L, L), dtype=bool))[:, :, None]

    def step(S, inp):
        qb, kb, vb, Gb = inp
        diff = jnp.where(causal, Gb[:, :, :, None, :] - Gb[:, :, None, :, :], -jnp.inf)
        A = jnp.sum(qb[:, :, :, None, :] * kb[:, :, None, :, :] * jnp.exp(diff), axis=-1)
        o = (jnp.einsum('bhij,bhjv->bhiv', A, vb)
             + jnp.einsum('bhik,bhkv->bhiv', qb * jnp.exp(Gb), S))
        GL = Gb[:, :, -1:, :]
        S = (jnp.exp(GL[:, :, 0, :])[..., None] * S
             + jnp.einsum('bhjk,bhjv->bhkv', kb * jnp.exp(GL - Gb), vb))
        return S, o

    S, o = lax.scan(step, S0.astype(jnp.float32), (qc, kc, vc, G))
    o = o.transpose(1, 0, 3, 2, 4).reshape(B, T, H, dv)
    return o, S


def gla_mixer(x, S0, w_in, wa1, wa2, ba, onorm_g, w_o):
    B, T, _ = x.shape
    H = GLA_HEADS
    q, k, v, og = jnp.split(x @ w_in, [GLA_DK_TOTAL, 2 * GLA_DK_TOTAL, 2 * GLA_DK_TOTAL + GLA_DV_TOTAL], axis=-1)
    lg = jax.nn.log_sigmoid(((x @ wa1) @ wa2 + ba).astype(jnp.float32)) / GLA_GATE_NORMALIZER
    hk = lambda t: t.reshape(B, T, H, GLA_DK)
    o, S = gla_chunked(hk(q) * GLA_DK ** -0.5, hk(k), v.reshape(B, T, H, GLA_DV), hk(lg), S0)
    o = rmsnorm(o, onorm_g).astype(x.dtype) * jax.nn.silu(og).reshape(B, T, H, GLA_DV)
    return o.reshape(B, T, GLA_DV_TOTAL) @ w_o, S.astype(S0.dtype)


def conv_ffn(x, buf, w_up, conv_w, conv_b, w_down):
    h_gate, h_val = jnp.split(x @ w_up, 2, axis=-1)
    hc, new_buf = causal_dwconv(h_gate, buf, conv_w)
    return (jax.nn.silu(hc + conv_b) * h_val) @ w_down, new_buf


def trunk(x, c, st_conv_a, st_shift_b, st_wkv_b, st_gla_c, st_conv_f, p):
    mod = jnp.einsum('bd,lde->lbe', jax.nn.silu(c), p['ada_w']) + p['ada_b'][:, None, :]
    new_a, new_sb, new_wb, new_gc, new_f = [], [], [], [], []
    h = x
    for l in range(DEPTH):
        sh_m, sc_m, gt_m, sh_f, sc_f, gt_f = jnp.split(mod[l][:, None, :], N_MOD, axis=-1)
        u = rmsnorm(h, p['norm_g'][l, 0]) * (1 + sc_m) + sh_m
        kind, j = l % N_MIXERS, l // N_MIXERS
        if kind == 0:
            out, nb = short_conv_mixer(u, st_conv_a[j], p['sconv_w_in'][j], p['sconv_conv_w'][j],
                                       p['sconv_w_out'][j])
            new_a.append(nb)
        elif kind == 1:
            out, nsh, nS = rwkv7_mixer(u, st_shift_b[j], st_wkv_b[j], p['rwkv_mu'][j], p['rwkv_w_rkv'][j],
                                       p['rwkv_w0'][j], p['rwkv_w1'][j], p['rwkv_w2'][j], p['rwkv_a0'][j],
                                       p['rwkv_a1'][j], p['rwkv_a2'][j], p['rwkv_g1'][j], p['rwkv_g2'][j],
                                       p['rwkv_k_k'][j], p['rwkv_k_a'][j], p['rwkv_r_k'][j],
                                       p['rwkv_lnx_w'][j], p['rwkv_lnx_b'][j], p['rwkv_w_o'][j])
            new_sb.append(nsh)
            new_wb.append(nS)
        else:
            out, nS = gla_mixer(u, st_gla_c[j], p['gla_w_in'][j], p['gla_wa1'][j], p['gla_wa2'][j],
                                p['gla_ba'][j], p['gla_onorm_g'][j], p['gla_w_o'][j])
            new_gc.append(nS)
        h = h + gt_m * rmsnorm(out, p['norm_g'][l, 1])
        u = rmsnorm(h, p['norm_g'][l, 2]) * (1 + sc_f) + sh_f
        out, nb = conv_ffn(u, st_conv_f[l], p['ffn_w_up'][l], p['ffn_conv_w'][l], p['ffn_conv_b'][l],
                           p['ffn_w_down'][l])
        new_f.append(nb)
        h = h + gt_f * rmsnorm(out, p['norm_g'][l, 3])
    return (h, jnp.stack(new_a), jnp.stack(new_sb), jnp.stack(new_wb), jnp.stack(new_gc), jnp.stack(new_f))


def setup_inputs(seed: int = 0) -> dict:
    key = jax.random.key(seed)
    ks = iter(jax.random.split(key, 64))

    def nrm(shape, scale=1.0):
        return jax.random.normal(next(ks), shape, jnp.float32) * scale

    def unif(shape, lo, hi):
        return jax.random.uniform(next(ks), shape, jnp.float32, lo, hi)

    D = D_MODEL
    NA, NB, NC = N_SCONV_LAYERS, N_RWKV_LAYERS, N_GLA_LAYERS
    H, N = RWKV_HEADS, RWKV_HEAD
    return {
        'x_prompt': nrm((BATCH, SEQ, D)),
        'x_sample': nrm((DEC_BATCH, DEC_SEQ, D)),
        'state_conv_a': nrm((NA, DEC_BATCH, CONV_W - 1, D)),
        'state_shift_b': nrm((NB, DEC_BATCH, D)),
        'state_wkv_b': nrm((NB, DEC_BATCH, H, N, N)),
        'state_gla_c': nrm((NC, DEC_BATCH, GLA_HEADS, GLA_DK, GLA_DV)),
        'state_conv_ffn': nrm((DEPTH, DEC_BATCH, CONV_W - 1, D_FF)),
        'c_prompt': nrm((BATCH, D)),
        'c_sample': nrm((DEC_BATCH, D)),
        'ada_w': nrm((DEPTH, D, N_MOD * D), D ** -0.5),
        'ada_b': nrm((DEPTH, N_MOD * D), 0.02),
        'norm_g': 1.0 + nrm((DEPTH, 4, D), 0.05),
        'sconv_w_in': nrm((NA, D, 3 * D), D ** -0.5),
        'sconv_conv_w': nrm((NA, CONV_W, D), CONV_W ** -0.5),
        'sconv_w_out': nrm((NA, D, D), D ** -0.5),
        'rwkv_mu': unif((NB, 6, D), 0.0, 1.0),
        'rwkv_w_rkv': nrm((NB, 3, D, D), D ** -0.5),
        'rwkv_w0': unif((NB, D), -4.0, 1.0),
        'rwkv_w1': nrm((NB, D, RWKV_DECAY_LORA), D ** -0.5),
        'rwkv_w2': nrm((NB, RWKV_DECAY_LORA, D), 0.5 * RWKV_DECAY_LORA ** -0.5),
        'rwkv_a0': nrm((NB, D), 0.1),
        'rwkv_a1': nrm((NB, D, RWKV_A_LORA), D ** -0.5),
        'rwkv_a2': nrm((NB, RWKV_A_LORA, D), RWKV_A_LORA ** -0.5),
        'rwkv_g1': nrm((NB, D, RWKV_GATE_LORA), D ** -0.5),
        'rwkv_g2': nrm((NB, RWKV_GATE_LORA, D), RWKV_GATE_LORA ** -0.5),
        'rwkv_k_k': 0.85 + nrm((NB, D), 0.05),
        'rwkv_k_a': 1.0 + nrm((NB, D), 0.05),
        'rwkv_r_k': nrm((NB, H, N), 0.1),
        'rwkv_lnx_w': 1.0 + nrm((NB, D), 0.05),
        'rwkv_lnx_b': nrm((NB, D), 0.02),
        'rwkv_w_o': nrm((NB, D, D), D ** -0.5),
        'gla_w_in': nrm((NC, D, 2 * GLA_DK_TOTAL + 2 * GLA_DV_TOTAL), D ** -0.5),
        'gla_wa1': nrm((NC, D, GLA_GATE_RANK), D ** -0.5),
        'gla_wa2': nrm((NC, GLA_GATE_RANK, GLA_DK_TOTAL), GLA_GATE_RANK ** -0.5),
        'gla_ba': nrm((NC, GLA_DK_TOTAL), 0.1),
        'gla_onorm_g': 1.0 + nrm((NC, GLA_DV), 0.05),
        'gla_w_o': nrm((NC, GLA_DV_TOTAL, D), GLA_DV_TOTAL ** -0.5),
        'ffn_w_up': nrm((DEPTH, D, 2 * D_FF), D ** -0.5),
        'ffn_conv_w': nrm((DEPTH, CONV_W, D_FF), CONV_W ** -0.5),
        'ffn_conv_b': nrm((DEPTH, D_FF), 0.02),
        'ffn_w_down': nrm((DEPTH, D_FF, D), D_FF ** -0.5),
    }


def reference(x_prompt, x_sample, state_conv_a, state_shift_b, state_wkv_b, state_gla_c, state_conv_ffn,
              c_prompt, c_sample, ada_w, ada_b, norm_g, sconv_w_in, sconv_conv_w, sconv_w_out,
              rwkv_mu, rwkv_w_rkv, rwkv_w0, rwkv_w1, rwkv_w2, rwkv_a0, rwkv_a1, rwkv_a2, rwkv_g1, rwkv_g2,
              rwkv_k_k, rwkv_k_a, rwkv_r_k, rwkv_lnx_w, rwkv_lnx_b, rwkv_w_o,
              gla_w_in, gla_wa1, gla_wa2, gla_ba, gla_onorm_g, gla_w_o,
              ffn_w_up, ffn_conv_w, ffn_conv_b, ffn_w_down):
    p = dict(ada_w=ada_w, ada_b=ada_b, norm_g=norm_g,
             sconv_w_in=sconv_w_in, sconv_conv_w=sconv_conv_w, sconv_w_out=sconv_w_out,
             rwkv_mu=rwkv_mu, rwkv_w_rkv=rwkv_w_rkv, rwkv_w0=rwkv_w0, rwkv_w1=rwkv_w1, rwkv_w2=rwkv_w2,
             rwkv_a0=rwkv_a0, rwkv_a1=rwkv_a1, rwkv_a2=rwkv_a2, rwkv_g1=rwkv_g1, rwkv_g2=rwkv_g2,
             rwkv_k_k=rwkv_k_k, rwkv_k_a=rwkv_k_a, rwkv_r_k=rwkv_r_k, rwkv_lnx_w=rwkv_lnx_w,
             rwkv_lnx_b=rwkv_lnx_b, rwkv_w_o=rwkv_w_o,
             gla_w_in=gla_w_in, gla_wa1=gla_wa1, gla_wa2=gla_wa2, gla_ba=gla_ba,
             gla_onorm_g=gla_onorm_g, gla_w_o=gla_w_o,
             ffn_w_up=ffn_w_up, ffn_conv_w=ffn_conv_w, ffn_conv_b=ffn_conv_b, ffn_w_down=ffn_w_down)
    bp = x_prompt.shape[0]
    zeros = lambda s: jnp.zeros((s.shape[0], bp) + s.shape[2:], x_prompt.dtype)
    y_p, ca_p, sb_p, wb_p, gc_p, cf_p = trunk(x_prompt, c_prompt, zeros(state_conv_a), zeros(state_shift_b),
                                              zeros(state_wkv_b), zeros(state_gla_c), zeros(state_conv_ffn), p)
    y_s, ca_s, sb_s, wb_s, gc_s, cf_s = trunk(x_sample, c_sample, state_conv_a, state_shift_b, state_wkv_b,
                                              state_gla_c, state_conv_ffn, p)
    return (y_p, y_s, ca_p, ca_s, sb_p, sb_s, wb_p, wb_s, gc_p, gc_s, cf_p, cf_s)
```

```python
import functools

import jax
import jax.numpy as jnp
import numpy as np
from jax import lax
from jax.experimental import pallas as pl
from jax.experimental.pallas import tpu as pltpu

F32, BF16 = jnp.float32, jnp.bfloat16

D = 1024
DEPTH = 4
N_MOD = 6
CONV_W = 3
RMS_EPS = 1e-6
RWKV_HEADS, RWKV_HEAD = 16, 64
RWKV_GN_EPS = 64e-5
GLA_HEADS, GLA_DK, GLA_DV = 4, 128, 256
GLA_DK_TOTAL, GLA_DV_TOTAL = 512, 1024
GLA_GATE_NORMALIZER = 16.0
D_FF = 2816

LANES = 128
CARRY_ROWS = 8
CHUNK = 64
TM = 256
VMEM_LIMIT = 56 * 1024 * 1024


def _rms(x, g):
    return x * lax.rsqrt(jnp.mean(x * x, axis=-1, keepdims=True) + RMS_EPS) * g


def _silu(x):
    return x * jax.nn.sigmoid(x)


def _softplus(x):
    return jnp.maximum(x, 0.0) + jnp.log1p(jnp.exp(-jnp.abs(x)))


def _bdot(a, w):
    return jnp.dot(a.astype(BF16), w, preferred_element_type=F32)


def _dot_nt(a, b):
    return lax.dot_general(a, b, (((1,), (1,)), ((), ())), preferred_element_type=F32)


def _dot_tn(a, b):
    return lax.dot_general(a, b, (((0,), (0,)), ((), ())), preferred_element_type=F32)


_NN = (((1,), (0,)), ((), ()))
_NT = (((1,), (1,)), ((), ()))


def _dot3(a, b, dims=_NN):
    ah = a.astype(BF16)
    al = (a - ah.astype(F32)).astype(BF16)
    bh = b.astype(BF16)
    bl = (b - bh.astype(F32)).astype(BF16)
    dg = functools.partial(lax.dot_general, dimension_numbers=dims, preferred_element_type=F32)
    return dg(ah, bh) + (dg(ah, bl) + dg(al, bh))


def _split3(x):
    hi = x.astype(BF16)
    r1 = x - hi.astype(F32)
    mid = r1.astype(BF16)
    lo = (r1 - mid.astype(F32)).astype(BF16)
    return hi, mid, lo


def _dot01(m01, x):
    hi, mid, lo = _split3(x)
    return (jnp.dot(m01, hi, preferred_element_type=F32)
            + jnp.dot(m01, mid, preferred_element_type=F32)
            + jnp.dot(m01, lo, preferred_element_type=F32))


def _x01(x, m01):
    hi, mid, lo = _split3(x)
    return (jnp.dot(hi, m01, preferred_element_type=F32)
            + jnp.dot(mid, m01, preferred_element_type=F32)
            + jnp.dot(lo, m01, preferred_element_type=F32))


def _segsum(x, seg1, seg2):
    return _x01(_x01(x, seg1), seg2)


def _mod(mod_ref, i, sample):
    return mod_ref[:, i * D:(i + 1) * D] if sample else mod_ref[i:i + 1, :]


def _shift_rows(x, k, carry):
    row = lax.broadcasted_iota(jnp.int32, (x.shape[0], 1), 0)
    y = pltpu.roll(x, k, axis=0)
    for j in range(k):
        src = CARRY_ROWS - k + j
        y = jnp.where(row == j, carry[src:src + 1, :], y)
    return y


def _whole(shape):
    nd = len(shape)
    return pl.BlockSpec(tuple(shape), lambda b, t: (0,) * nd, pipeline_mode=pl.Buffered(1))


def _rows(tm, width):
    return pl.BlockSpec((None, tm, width), lambda b, t: (b, t, 0))


def _per_seq(r, width):
    return pl.BlockSpec((None, r, width), lambda b, t: (b, 0, 0))


def _params():
    return pltpu.CompilerParams(dimension_semantics=("arbitrary", "arbitrary"),
                                vmem_limit_bytes=VMEM_LIMIT)


def _mod_kernel(c_ref, w_ref, b_ref, o_ref):
    c = c_ref[...]
    o_ref[...] = _bdot(_silu(c), w_ref[...].astype(BF16)) + b_ref[...]


def _modulation(c_all, ada_w, ada_b):
    n = c_all.shape[0]
    tn = 1536
    return pl.pallas_call(
        _mod_kernel,
        grid=(DEPTH, N_MOD * D // tn),
        in_specs=[pl.BlockSpec((n, D), lambda l, j: (0, 0)),
                  pl.BlockSpec((None, D, tn), lambda l, j: (l, 0, j)),
                  pl.BlockSpec((None, 1, tn), lambda l, j: (l, 0, j))],
        out_specs=pl.BlockSpec((None, n, tn), lambda l, j: (l, 0, j)),
        out_shape=jax.ShapeDtypeStruct((DEPTH, n, N_MOD * D), F32),
        compiler_params=_params(),
        name="adaln_mod",
    )(c_all, ada_w, ada_b.reshape(DEPTH, 1, N_MOD * D))


def _sconv_kernel(*refs, sample, tm, nt):
    if sample:
        h_ref, mod_ref, ng_ref, win_ref, cw_ref, wout_ref, st_ref, o_ref, nb_ref = refs
    else:
        h_ref, mod_ref, ng_ref, win_ref, cw_ref, wout_ref, o_ref, nb_ref, carry_ref = refs
    mod = lambda i: _mod(mod_ref, i, sample)
    h = h_ref[...]
    u = _rms(h, ng_ref[0:1, :]) * (1.0 + mod(1)) + mod(0)
    p = _bdot(u, win_ref[...])
    bg = p[:, :D]
    z = p[:, D:2 * D] * p[:, 2 * D:]
    cw = cw_ref[...]
    if sample:
        y = st_ref[:, :D] * cw[0:1] + st_ref[:, D:] * cw[1:2] + z * cw[2:3]
        nb_ref[:, :D] = st_ref[:, D:]
        nb_ref[:, D:] = z
    else:
        t = pl.program_id(1)

        @pl.when(t == 0)
        def _():
            carry_ref[...] = jnp.zeros_like(carry_ref)

        c = carry_ref[...]
        y = _shift_rows(z, 2, c) * cw[0:1] + _shift_rows(z, 1, c) * cw[1:2] + z * cw[2:3]
        carry_ref[...] = z[tm - CARRY_ROWS:, :]

        @pl.when(t == nt - 1)
        def _():
            nb_ref[...] = carry_ref[CARRY_ROWS - 2:, :]

    out = _bdot(bg * y, wout_ref[...])
    o_ref[...] = h + mod(2) * _rms(out, ng_ref[1:2, :])


def _sconv_layer(h, mod, ng, w_in, conv_w, w_out, state, *, sample):
    nb, rows, _ = h.shape
    tm = rows if sample else TM
    nt = rows // tm
    in_specs = [_rows(tm, D), _per_seq(*mod.shape[1:]), _whole(ng.shape), _whole(w_in.shape),
                _whole(conv_w.shape), _whole(w_out.shape)]
    args = [h, mod, ng, w_in, conv_w, w_out]
    if sample:
        in_specs.append(_per_seq(rows, 2 * D))
        args.append(state)
        nb_shape, nb_spec, scratch = (nb, rows, 2 * D), _per_seq(rows, 2 * D), []
    else:
        nb_shape, nb_spec = (nb, 2, D), _per_seq(2, D)
        scratch = [pltpu.VMEM((CARRY_ROWS, D), F32)]
    return pl.pallas_call(
        functools.partial(_sconv_kernel, sample=sample, tm=tm, nt=nt),
        grid=(nb, nt), in_specs=in_specs,
        out_specs=[_rows(tm, D), nb_spec],
        out_shape=[jax.ShapeDtypeStruct(h.shape, F32), jax.ShapeDtypeStruct(nb_shape, F32)],
        scratch_shapes=scratch, compiler_params=_params(),
        name="sconv_sample" if sample else "sconv_prompt",
    )(*args)


def _ffn_kernel(*refs, sample, tm, nt):
    if sample:
        h_ref, mod_ref, ng_ref, wup_ref, cw_ref, cb_ref, wdn_ref, st_ref, o_ref, nb_ref = refs
    else:
        h_ref, mod_ref, ng_ref, wup_ref, cw_ref, cb_ref, wdn_ref, o_ref, nb_ref, carry_ref = refs
    mod = lambda i: _mod(mod_ref, i, sample)
    h = h_ref[...]
    u = (_rms(h, ng_ref[2:3, :]) * (1.0 + mod(4)) + mod(3)).astype(BF16)
    g = jnp.dot(u, wup_ref[:, :D_FF], preferred_element_type=F32)
    cw = cw_ref[...]
    if sample:
        hc = st_ref[:, :D_FF] * cw[0:1] + st_ref[:, D_FF:] * cw[1:2] + g * cw[2:3]
        nb_ref[:, :D_FF] = st_ref[:, D_FF:]
        nb_ref[:, D_FF:] = g
    else:
        t = pl.program_id(1)

        @pl.when(t == 0)
        def _():
            carry_ref[...] = jnp.zeros_like(carry_ref)

        c = carry_ref[...]
        hc = _shift_rows(g, 2, c) * cw[0:1] + _shift_rows(g, 1, c) * cw[1:2] + g * cw[2:3]
        carry_ref[...] = g[tm - CARRY_ROWS:, :]

        @pl.when(t == nt - 1)
        def _():
            nb_ref[...] = carry_ref[CARRY_ROWS - 2:, :]

    val = jnp.dot(u, wup_ref[:, D_FF:], preferred_element_type=F32)
    out = _bdot(_silu(hc + cb_ref[...]) * val, wdn_ref[...])
    o_ref[...] = h + mod(5) * _rms(out, ng_ref[3:4, :])


def _ffn_layer(h, mod, ng, w_up, conv_w, conv_b, w_down, state, *, sample):
    nb, rows, _ = h.shape
    tm = rows if sample else TM
    nt = rows // tm
    in_specs = [_rows(tm, D), _per_seq(*mod.shape[1:]), _whole(ng.shape), _whole(w_up.shape),
                _whole(conv_w.shape), _whole(conv_b.shape), _whole(w_down.shape)]
    args = [h, mod, ng, w_up, conv_w, conv_b, w_down]
    if sample:
        in_specs.append(_per_seq(rows, 2 * D_FF))
        args.append(state)
        nb_shape, nb_spec, scratch = (nb, rows, 2 * D_FF), _per_seq(rows, 2 * D_FF), []
    else:
        nb_shape, nb_spec = (nb, 2, D_FF), _per_seq(2, D_FF)
        scratch = [pltpu.VMEM((CARRY_ROWS, D_FF), F32)]
    return pl.pallas_call(
        functools.partial(_ffn_kernel, sample=sample, tm=tm, nt=nt),
        grid=(nb, nt), in_specs=in_specs,
        out_specs=[_rows(tm, D), nb_spec],
        out_shape=[jax.ShapeDtypeStruct(h.shape, F32), jax.ShapeDtypeStruct(nb_shape, F32)],
        scratch_shapes=scratch, compiler_params=_params(),
        name="ffn_sample" if sample else "ffn_prompt",
    )(*args)


def _rwkv_pre_kernel(*refs, sample, tm, nt):
    (h_ref, mod_ref, ng_ref, mu_ref, wrkv_ref, w0_ref, w1_ref, w2_ref, a0_ref, a1_ref, a2_ref,
     g1_ref, g2_ref, kk_ref, ka_ref, rk_ref, seg1_ref, seg2_ref) = refs[:18]
    if sample:
        shift_ref = refs[18]
        outs = refs[19:]
    else:
        tri_ref, sel_ref = refs[18:20]
        outs = refs[20:-1]
        carry_ref = refs[-1]
    at_ref, bt_ref, kt_ref, rt_ref, v_ref, g_ref, bonus_ref, wc_ref, sh_ref = outs
    mod = lambda i: _mod(mod_ref, i, sample)
    seg1, seg2 = seg1_ref[...], seg2_ref[...]

    u = _rms(h_ref[...], ng_ref[0:1, :]) * (1.0 + mod(1)) + mod(0)
    if sample:
        prev = shift_ref[...]
        sh_ref[...] = u
    else:
        t = pl.program_id(1)

        @pl.when(t == 0)
        def _():
            carry_ref[...] = jnp.zeros_like(carry_ref)

        prev = _shift_rows(u, 1, carry_ref[...])
        carry_ref[...] = u[tm - CARRY_ROWS:, :]

        @pl.when(t == nt - 1)
        def _():
            sh_ref[...] = carry_ref[CARRY_ROWS - 1:, :]

    xx = prev - u
    mix = lambda i: u + xx * mu_ref[i:i + 1, :]
    r = _bdot(mix(0), wrkv_ref[0])
    k = _bdot(mix(2), wrkv_ref[1])
    v = _bdot(mix(3), wrkv_ref[2])
    z = w0_ref[...] + _bdot(jnp.tanh(_bdot(mix(1), w1_ref[...])), w2_ref[...])
    lw = -jnp.exp(-_softplus(-z) - 0.5)
    a = jax.nn.sigmoid(a0_ref[...] + _bdot(_bdot(mix(4), a1_ref[...]), a2_ref[...]))
    g_ref[...] = _bdot(jax.nn.sigmoid(_bdot(mix(5), g1_ref[...])), g2_ref[...])

    kk = k * kk_ref[...]
    kk = kk / jnp.maximum(jnp.sqrt(_segsum(kk * kk, seg1, seg2)), 1e-12)
    k2 = k * (1.0 + (a - 1.0) * ka_ref[...])
    bonus_ref[...] = _segsum(r * k2 * rk_ref[...], seg1, seg2) * v

    if sample:
        lc = lw
        wc_ref[...] = jnp.exp(lw)
    else:
        lc = _dot01(tri_ref[...], lw)
        wc_ref[...] = jnp.exp(_dot01(sel_ref[...], lc))
    e_neg = jnp.exp(-lc)
    at_ref[...] = -kk * jnp.exp(lc - lw)
    bt_ref[...] = kk * a * e_neg
    kt_ref[...] = k2 * e_neg
    rt_ref[...] = (r * jnp.exp(lc)).astype(rt_ref.dtype)
    v_ref[...] = v


def _rwkv_core_kernel(at_ref, bt_ref, kt_ref, rt_ref, v_ref, wc_ref, y_ref, sout_ref, s_ref, *, nt):
    t = pl.program_id(1)

    @pl.when(t == 0)
    def _():
        s_ref[...] = jnp.zeros_like(s_ref)

    rows2 = 2 * CHUNK
    lane_lo = lax.broadcasted_iota(jnp.int32, (1, LANES), 1) < RWKV_HEAD
    ri = lax.broadcasted_iota(jnp.int32, (rows2, rows2), 0)
    ci = lax.broadcasted_iota(jnp.int32, (rows2, rows2), 1)
    shift = CHUNK.bit_length() - 1
    same = (ri >> shift) == (ci >> shift)
    m_strict = same & (ci < ri)
    m_incl = same & (ci <= ri)
    eye = (ri == ci).astype(F32)

    def stack(x):
        zero = jnp.zeros_like(x)
        return jnp.concatenate([jnp.where(lane_lo, x, zero), jnp.where(lane_lo, zero, x)], axis=0)

    for p in range(RWKV_HEADS // 2):
        sl = slice(p * LANES, (p + 1) * LANES)
        a_s, b_s, k_s, r_s, v_s = (stack(ref[:, sl]) for ref in (at_ref, bt_ref, kt_ref, rt_ref, v_ref))
        bk = jnp.concatenate([b_s, k_s], axis=0)
        bkb = bk.astype(BF16)
        vb = v_s.astype(BF16)
        sc_a = _dot3(a_s, bk, _NT)
        sc_r = _dot_nt(r_s, bkb)
        a_ab = jnp.where(m_strict, sc_a[:, :rows2], 0.0)
        a_ak = jnp.where(m_strict, sc_a[:, rows2:], 0.0)
        a_rb = jnp.where(m_incl, sc_r[:, :rows2], 0.0)
        a_rk = jnp.where(m_incl, sc_r[:, rows2:], 0.0)
        pw = a_ab
        tinv = eye + pw
        for _ in range(5):
            pw = _dot3(pw, pw)
            tinv = tinv + _dot3(tinv, pw)
        s = s_ref[p]
        x = _dot3(a_s, s, _NT) + _dot3(a_ak, v_s)
        ub = _dot3(tinv, x).astype(BF16)
        y = (_dot_nt(r_s, s.astype(BF16)) + jnp.dot(a_rb.astype(BF16), ub, preferred_element_type=F32)
             + jnp.dot(a_rk.astype(BF16), vb, preferred_element_type=F32))
        y_ref[:, sl] = y[:CHUNK, :] + y[CHUNK:, :]
        ds = _dot_tn(jnp.concatenate([ub, vb], axis=0), bkb)
        s_ref[p] = (s + ds) * wc_ref[:, sl]

    @pl.when(t == nt - 1)
    def _():
        sout_ref[...] = s_ref[...]


def _rwkv_core_sample_kernel(at_ref, bt_ref, kt_ref, rt_ref, v_ref, wc_ref, s_ref, y_ref, sout_ref):
    n = RWKV_HEAD
    eye = (lax.broadcasted_iota(jnp.int32, (n, n), 0) == lax.broadcasted_iota(jnp.int32, (n, n), 1)).astype(F32)
    s = s_ref[...]
    u = jnp.sum(s * at_ref[...], axis=-1, keepdims=True)
    v_col = jnp.sum(eye * v_ref[...], axis=-1, keepdims=True)
    s1 = s + u * bt_ref[...] + v_col * kt_ref[...]
    y_col = jnp.sum(s1 * rt_ref[...], axis=-1, keepdims=True)
    y_ref[...] = jnp.sum(eye * y_col, axis=-2, keepdims=True)
    sout_ref[...] = s1 * wc_ref[...]


def _rwkv_post_kernel(y_ref, g_ref, bonus_ref, h_ref, mod_ref, ng_ref, lnw_ref, lnb_ref, wo_ref,
                      seg1_ref, seg2_ref, o_ref, *, sample):
    mod = lambda i: _mod(mod_ref, i, sample)
    seg1, seg2 = seg1_ref[...], seg2_ref[...]
    y = y_ref[...]
    yc = y - _segsum(y, seg1, seg2) * (1.0 / RWKV_HEAD)
    var = _segsum(yc * yc, seg1, seg2) * (1.0 / RWKV_HEAD)
    yn = yc * lax.rsqrt(var + RWKV_GN_EPS) * lnw_ref[...] + lnb_ref[...] + bonus_ref[...]
    out = _bdot(yn * g_ref[...], wo_ref[...])
    o_ref[...] = h_ref[...] + mod(2) * _rms(out, ng_ref[1:2, :])


def _np_bf16(a):
    return jnp.asarray(np.asarray(a, np.float32), BF16)


def _rwkv_layer(h, mod, ng, w, state_shift, state_wkv, *, sample):
    nb, rows, _ = h.shape
    tm = rows if sample else TM
    nt = rows // tm
    lane_head = np.arange(D) // RWKV_HEAD
    seg1 = _np_bf16(lane_head[:, None] == np.arange(LANES)[None, :])
    seg2 = _np_bf16(np.arange(LANES)[:, None] == lane_head[None, :])
    pad_c = lambda x: jnp.pad(x, ((0, 0), (0, LANES - x.shape[1]))).astype(BF16)
    pad_r = lambda x: jnp.pad(x, ((0, LANES - x.shape[0]), (0, 0))).astype(BF16)
    row1 = lambda x: x.reshape(1, D)
    consts = [ng, w["mu"], w["w_rkv"].astype(BF16), row1(w["w0"]), pad_c(w["w1"]), pad_r(w["w2"]),
              row1(w["a0"]), pad_c(w["a1"]), pad_r(w["a2"]), w["g1"].astype(BF16), w["g2"].astype(BF16),
              row1(w["k_k"]), row1(w["k_a"]), row1(w["r_k"]), seg1, seg2]
    in_specs = [_rows(tm, D), _per_seq(*mod.shape[1:])] + [_whole(c.shape) for c in consts]
    args = [h, mod] + consts
    act_dt = F32 if sample else BF16
    nchunk = 1 if sample else tm // CHUNK
    if sample:
        in_specs.append(_per_seq(rows, D))
        args.append(state_shift)
        wc_shape, wc_spec = (nb, rows, D), _rows(tm, D)
        sh_shape, sh_spec = (nb, rows, D), _rows(tm, D)
        scratch = []
    else:
        ti = np.arange(tm)
        tri = _np_bf16((ti[:, None] // CHUNK == ti[None, :] // CHUNK) & (ti[None, :] <= ti[:, None]))
        sel = _np_bf16(ti[None, :] == (np.arange(CARRY_ROWS)[:, None] * CHUNK + CHUNK - 1))
        in_specs += [_whole(tri.shape), _whole(sel.shape)]
        args += [tri, sel]
        wc_shape = (nb, nt, CARRY_ROWS, D)
        wc_spec = pl.BlockSpec((None, None, CARRY_ROWS, D), lambda b, t: (b, t, 0, 0))
        sh_shape, sh_spec = (nb, 1, D), _per_seq(1, D)
        scratch = [pltpu.VMEM((CARRY_ROWS, D), F32)]
    act = jax.ShapeDtypeStruct(h.shape, act_dt)
    f32 = jax.ShapeDtypeStruct(h.shape, F32)
    at, bt, kt, rt, vv, g, bonus, wc, new_shift = pl.pallas_call(
        functools.partial(_rwkv_pre_kernel, sample=sample, tm=tm, nt=nt),
        grid=(nb, nt), in_specs=in_specs,
        out_specs=[_rows(tm, D)] * 7 + [wc_spec, sh_spec],
        out_shape=[f32, f32, f32, act, f32, f32, f32, jax.ShapeDtypeStruct(wc_shape, F32),
                                jax.ShapeDtypeStruct(sh_shape, F32)],
        scratch_shapes=scratch, compiler_params=_params(),
        name="rwkv_pre_sample" if sample else "rwkv_pre_prompt",
    )(*args)

    hh, n = RWKV_HEADS, RWKV_HEAD
    if sample:
        sb = 8
        nseq = rows
        hv = lambda x: x.reshape(nseq, hh, 1, n)
        vec_spec = pl.BlockSpec((sb, hh, 1, n), lambda i: (i, 0, 0, 0))
        st_spec = pl.BlockSpec((sb, hh, n, n), lambda i: (i, 0, 0, 0))
        y, new_s = pl.pallas_call(
            _rwkv_core_sample_kernel,
            grid=(nseq // sb,),
            in_specs=[vec_spec] * 6 + [st_spec],
            out_specs=[vec_spec, st_spec],
            out_shape=[jax.ShapeDtypeStruct((nseq, hh, 1, n), F32),
                       jax.ShapeDtypeStruct((nseq, hh, n, n), F32)],
            compiler_params=pltpu.CompilerParams(dimension_semantics=("arbitrary",),
                                                 vmem_limit_bytes=VMEM_LIMIT),
            name="rwkv_core_sample",
        )(hv(at), hv(bt), hv(kt), hv(rt), hv(vv), hv(wc), state_wkv)
        y = y.reshape(nb, rows, D)
    else:
        ncht = rows // CHUNK
        wc_rows = wc[:, :, :nchunk, :].reshape(nb, ncht, 1, D)
        npair = hh // 2
        y, s_bd = pl.pallas_call(
            functools.partial(_rwkv_core_kernel, nt=ncht),
            grid=(nb, ncht),
            in_specs=[_rows(CHUNK, D)] * 5 + [pl.BlockSpec((None, None, 1, D), lambda b, t: (b, t, 0, 0))],
            out_specs=[_rows(CHUNK, D),
                       pl.BlockSpec((None, npair, LANES, LANES), lambda b, t: (b, 0, 0, 0))],
            out_shape=[f32, jax.ShapeDtypeStruct((nb, npair, LANES, LANES), F32)],
            scratch_shapes=[pltpu.VMEM((npair, LANES, LANES), F32)],
            compiler_params=_params(),
            name="rwkv_core_prompt",
        )(at, bt, kt, rt, vv, wc_rows)
        s5 = s_bd.reshape(nb, npair, 2, n, 2, n)
        new_s = jnp.stack([s5[:, :, 0, :, 0, :], s5[:, :, 1, :, 1, :]], axis=2).reshape(nb, hh, n, n)

    consts = [ng, row1(w["lnx_w"]), row1(w["lnx_b"]), w["w_o"].astype(BF16), seg1, seg2]
    h_new = pl.pallas_call(
        functools.partial(_rwkv_post_kernel, sample=sample),
        grid=(nb, nt),
        in_specs=[_rows(tm, D)] * 4 + [_per_seq(*mod.shape[1:])] + [_whole(c.shape) for c in consts],
        out_specs=_rows(tm, D), out_shape=f32, compiler_params=_params(),
        name="rwkv_post_sample" if sample else "rwkv_post_prompt",
    )(y, g, bonus, h, mod, *consts)
    return h_new, new_shift, new_s


GLA_LEVELS = (1, 2, 4, 8, 16, 32)


def _gla_pre_kernel(h_ref, mod_ref, ng_ref, win_ref, wa1_ref, wa2_ref, ba_ref,
                    q_ref, k_ref, v_ref, og_ref, lg_ref, *, sample):
    mod = lambda i: _mod(mod_ref, i, sample)
    u = (_rms(h_ref[...], ng_ref[0:1, :]) * (1.0 + mod(1)) + mod(0)).astype(BF16)
    dk, dv = GLA_DK_TOTAL, GLA_DV_TOTAL
    q_ref[...] = jnp.dot(u, win_ref[:, :dk], preferred_element_type=F32) * (GLA_DK ** -0.5)
    k_ref[...] = jnp.dot(u, win_ref[:, dk:2 * dk], preferred_element_type=F32)
    v_ref[...] = jnp.dot(u, win_ref[:, 2 * dk:2 * dk + dv], preferred_element_type=F32)
    og_ref[...] = jnp.dot(u, win_ref[:, 2 * dk + dv:], preferred_element_type=F32)
    gate = _bdot(jnp.dot(u, wa1_ref[...], preferred_element_type=F32), wa2_ref[...]) + ba_ref[...]
    lg_ref[...] = -_softplus(-gate) * (1.0 / GLA_GATE_NORMALIZER)


def _gla_core_kernel(q_ref, k_ref, v_ref, lg_ref, lvl_ref, o_ref, sout_ref, s_ref, *, nt):
    t = pl.program_id(1)

    @pl.when(t == 0)
    def _():
        s_ref[...] = jnp.zeros_like(s_ref)

    lg = lg_ref[...]
    pq = _dot01(lvl_ref[...], lg)
    row = lax.broadcasted_iota(jnp.int32, (CHUNK, 1), 0)
    ri = lax.broadcasted_iota(jnp.int32, (CHUNK, CHUNK), 0)
    ci = lax.broadcasted_iota(jnp.int32, (CHUNK, CHUNK), 1)
    nl = len(GLA_LEVELS)
    for hd in range(GLA_HEADS):
        sl = slice(hd * GLA_DK, (hd + 1) * GLA_DK)
        sv = slice(hd * GLA_DV, (hd + 1) * GLA_DV)
        q, k = q_ref[:, sl], k_ref[:, sl]
        vb = v_ref[:, sv].astype(BF16)
        a = jnp.where(ri == ci, _dot_nt(q.astype(BF16), k.astype(BF16)), 0.0)
        for li, m in enumerate(GLA_LEVELS):
            if m == 1:
                pre, suf = lg[:, sl], jnp.zeros((CHUNK, GLA_DK), F32)
            else:
                base = (li - 1) * 2 * CHUNK
                pre, suf = pq[base:base + CHUNK, sl], pq[base + CHUNK:base + 2 * CHUNK, sl]
            lm = m.bit_length() - 1
            second = ((row >> lm) & 1) == 1
            qe = jnp.where(second, q * jnp.exp(pre), 0.0).astype(BF16)
            ke = jnp.where(second, 0.0, k * jnp.exp(suf)).astype(BF16)
            a = a + jnp.where((ri >> (lm + 1)) == (ci >> (lm + 1)), _dot_nt(qe, ke), 0.0)
        base = (nl - 1) * 2 * CHUNK
        g_inc, g_rest = pq[base:base + CHUNK, sl], pq[base + CHUNK:base + 2 * CHUNK, sl]
        s = s_ref[hd]
        o_ref[:, sv] = (jnp.dot(a.astype(BF16), vb, preferred_element_type=F32)
                        + _dot_nt((q * jnp.exp(g_inc)).astype(BF16), s.astype(BF16)))
        s_ref[hd] = (s * jnp.exp(g_inc[CHUNK - 1:CHUNK, :])
                     + _dot_tn(vb, (k * jnp.exp(g_rest)).astype(BF16)))

    @pl.when(t == nt - 1)
    def _():
        sout_ref[...] = s_ref[...]


def _gla_core_sample_kernel(q_ref, k_ref, v_ref, lg_ref, s_ref, o_ref, sout_ref):
    n = GLA_DK
    eye = (lax.broadcasted_iota(jnp.int32, (n, n), 0) == lax.broadcasted_iota(jnp.int32, (n, n), 1)).astype(F32)
    col = lambda x: jnp.sum(eye * x, axis=-1, keepdims=True)
    q, k, v = q_ref[...], k_ref[...], v_ref[...]
    s = s_ref[...]
    decay = jnp.exp(lg_ref[...])
    qk = jnp.sum(q * k, axis=-1, keepdims=True)
    o_ref[...] = qk * v + jnp.sum(col(q * decay) * s, axis=-2, keepdims=True)
    sout_ref[...] = col(decay) * s + col(k) * v


def _gla_post_kernel(o_ref, og_ref, h_ref, mod_ref, ng_ref, on_ref, wo_ref, out_ref, *, sample):
    mod = lambda i: _mod(mod_ref, i, sample)
    og = og_ref[...]
    parts = []
    for hd in range(GLA_HEADS):
        sv = slice(hd * GLA_DV, (hd + 1) * GLA_DV)
        parts.append(_rms(o_ref[:, sv], on_ref[...]) * _silu(og[:, sv]))
    out = _bdot(jnp.concatenate(parts, axis=1), wo_ref[...])
    out_ref[...] = h_ref[...] + mod(2) * _rms(out, ng_ref[1:2, :])


def _gla_level_matrix():
    ti = np.arange(CHUNK)
    blocks = []
    for m in GLA_LEVELS[1:] + (CHUNK,):
        same = ti[:, None] // m == ti[None, :] // m
        blocks.append(same & (ti[None, :] <= ti[:, None]))
        blocks.append(same & (ti[None, :] > ti[:, None]))
    return _np_bf16(np.concatenate(blocks, axis=0))


def _gla_layer(h, mod, ng, w, state, *, sample):
    nb, rows, _ = h.shape
    tm = rows if sample else TM
    nt = rows // tm
    dk, dv, hh = GLA_DK_TOTAL, GLA_DV_TOTAL, GLA_HEADS
    wa1 = jnp.pad(w["wa1"], ((0, 0), (0, LANES - w["wa1"].shape[1]))).astype(BF16)
    wa2 = jnp.pad(w["wa2"], ((0, LANES - w["wa2"].shape[0]), (0, 0))).astype(BF16)
    consts = [ng, w["w_in"].astype(BF16), wa1, wa2, w["ba"].reshape(1, dk)]
    shp = lambda width: jax.ShapeDtypeStruct((nb, rows, width), F32)
    q, k, v, og, lg = pl.pallas_call(
        functools.partial(_gla_pre_kernel, sample=sample),
        grid=(nb, nt),
        in_specs=[_rows(tm, D), _per_seq(*mod.shape[1:])] + [_whole(c.shape) for c in consts],
        out_specs=[_rows(tm, dk), _rows(tm, dk), _rows(tm, dv), _rows(tm, dv), _rows(tm, dk)],
        out_shape=[shp(dk), shp(dk), shp(dv), shp(dv), shp(dk)],
        compiler_params=_params(),
        name="gla_pre_sample" if sample else "gla_pre_prompt",
    )(h, mod, *consts)

    if sample:
        sb = 8
        nseq = rows
        hk = lambda x: x.reshape(nseq, hh, 1, GLA_DK)
        kspec = pl.BlockSpec((sb, hh, 1, GLA_DK), lambda i: (i, 0, 0, 0))
        vspec = pl.BlockSpec((sb, hh, 1, GLA_DV), lambda i: (i, 0, 0, 0))
        sspec = pl.BlockSpec((sb, hh, GLA_DK, GLA_DV), lambda i: (i, 0, 0, 0))
        o, new_s = pl.pallas_call(
            _gla_core_sample_kernel,
            grid=(nseq // sb,),
            in_specs=[kspec, kspec, vspec, kspec, sspec],
            out_specs=[vspec, sspec],
            out_shape=[jax.ShapeDtypeStruct((nseq, hh, 1, GLA_DV), F32),
                       jax.ShapeDtypeStruct((nseq, hh, GLA_DK, GLA_DV), F32)],
            compiler_params=pltpu.CompilerParams(dimension_semantics=("arbitrary",),
                                                 vmem_limit_bytes=VMEM_LIMIT),
            name="gla_core_sample",
        )(hk(q), hk(k), v.reshape(nseq, hh, 1, GLA_DV), hk(lg), state)
        o = o.reshape(nb, rows, dv)
    else:
        ncht = rows // CHUNK
        lvl = _gla_level_matrix()
        o, s_t = pl.pallas_call(
            functools.partial(_gla_core_kernel, nt=ncht),
            grid=(nb, ncht),
            in_specs=[_rows(CHUNK, dk), _rows(CHUNK, dk), _rows(CHUNK, dv), _rows(CHUNK, dk),
                      _whole(lvl.shape)],
            out_specs=[_rows(CHUNK, dv),
                       pl.BlockSpec((None, hh, GLA_DV, GLA_DK), lambda b, t: (b, 0, 0, 0))],
            out_shape=[shp(dv), jax.ShapeDtypeStruct((nb, hh, GLA_DV, GLA_DK), F32)],
            scratch_shapes=[pltpu.VMEM((hh, GLA_DV, GLA_DK), F32)],
            compiler_params=_params(),
            name="gla_core_prompt",
        )(q, k, v, lg, lvl)
        new_s = jnp.swapaxes(s_t, -1, -2)

    consts = [ng, w["onorm_g"].reshape(1, GLA_DV), w["w_o"].astype(BF16)]
    h_new = pl.pallas_call(
        functools.partial(_gla_post_kernel, sample=sample),
        grid=(nb, nt),
        in_specs=[_rows(tm, dv), _rows(tm, dv), _rows(tm, D), _per_seq(*mod.shape[1:])]
                 + [_whole(c.shape) for c in consts],
        out_specs=_rows(tm, D), out_shape=shp(D), compiler_params=_params(),
        name="gla_post_sample" if sample else "gla_post_prompt",
    )(o, og, h, mod, *consts)
    return h_new, new_s


def _trunk(h, mod, p, states, *, sample):
    st_a, st_sb, st_wb, st_gc, st_f = states
    nb, rows, _ = h.shape
    new_a, new_sb, new_wb, new_gc, new_f = [], [], [], [], []
    for l in range(DEPTH):
        kind, j = l % 3, l // 3
        ng = p["norm_g"][l]
        if kind == 0:
            st = st_a[j].reshape(1, rows, 2 * D) if sample else None
            h, nbuf = _sconv_layer(h, mod[l], ng, p["sconv_w_in"][j].astype(BF16), p["sconv_conv_w"][j],
                                   p["sconv_w_out"][j].astype(BF16), st, sample=sample)
            new_a.append(nbuf.reshape(rows, 2, D) if sample else nbuf)
        elif kind == 1:
            w = {n: p["rwkv_" + n][j] for n in ("mu", "w_rkv", "w0", "w1", "w2", "a0", "a1", "a2", "g1", "g2",
                                                  "k_k", "k_a", "r_k", "lnx_w", "lnx_b", "w_o")}
            sh = st_sb[j].reshape(1, rows, D) if sample else None
            wkv = st_wb[j] if sample else None
            h, nsh, ns = _rwkv_layer(h, mod[l], ng, w, sh, wkv, sample=sample)
            new_sb.append(nsh.reshape(rows, D) if sample else nsh.reshape(nb, D))
            new_wb.append(ns)
        else:
            w = {n: p["gla_" + n][j] for n in ("w_in", "wa1", "wa2", "ba", "onorm_g", "w_o")}
            h, ns = _gla_layer(h, mod[l], ng, w, st_gc[j] if sample else None, sample=sample)
            new_gc.append(ns)
        st = st_f[l].reshape(1, rows, 2 * D_FF) if sample else None
        h, nbuf = _ffn_layer(h, mod[l], ng, p["ffn_w_up"][l].astype(BF16), p["ffn_conv_w"][l],
                             p["ffn_conv_b"][l].reshape(1, D_FF), p["ffn_w_down"][l].astype(BF16), st,
                             sample=sample)
        new_f.append(nbuf.reshape(rows, 2, D_FF) if sample else nbuf)
    return (h, jnp.stack(new_a), jnp.stack(new_sb), jnp.stack(new_wb), jnp.stack(new_gc), jnp.stack(new_f))


def kernel(x_prompt, x_sample, state_conv_a, state_shift_b, state_wkv_b, state_gla_c, state_conv_ffn, c_prompt, c_sample, ada_w, ada_b, norm_g, sconv_w_in, sconv_conv_w, sconv_w_out, rwkv_mu, rwkv_w_rkv, rwkv_w0, rwkv_w1, rwkv_w2, rwkv_a0, rwkv_a1, rwkv_a2, rwkv_g1, rwkv_g2, rwkv_k_k, rwkv_k_a, rwkv_r_k, rwkv_lnx_w, rwkv_lnx_b, rwkv_w_o, gla_w_in, gla_wa1, gla_wa2, gla_ba, gla_onorm_g, gla_w_o, ffn_w_up, ffn_conv_w, ffn_conv_b, ffn_w_down):
    p = dict(norm_g=norm_g, sconv_w_in=sconv_w_in, sconv_conv_w=sconv_conv_w, sconv_w_out=sconv_w_out,
             rwkv_mu=rwkv_mu, rwkv_w_rkv=rwkv_w_rkv, rwkv_w0=rwkv_w0, rwkv_w1=rwkv_w1, rwkv_w2=rwkv_w2,
             rwkv_a0=rwkv_a0, rwkv_a1=rwkv_a1, rwkv_a2=rwkv_a2, rwkv_g1=rwkv_g1, rwkv_g2=rwkv_g2,
             rwkv_k_k=rwkv_k_k, rwkv_k_a=rwkv_k_a, rwkv_r_k=rwkv_r_k.reshape(-1, D), rwkv_lnx_w=rwkv_lnx_w,
             rwkv_lnx_b=rwkv_lnx_b, rwkv_w_o=rwkv_w_o,
             gla_w_in=gla_w_in, gla_wa1=gla_wa1, gla_wa2=gla_wa2, gla_ba=gla_ba,
             gla_onorm_g=gla_onorm_g, gla_w_o=gla_w_o,
             ffn_w_up=ffn_w_up, ffn_conv_w=ffn_conv_w, ffn_conv_b=ffn_conv_b, ffn_w_down=ffn_w_down)
    bp, bs = x_prompt.shape[0], x_sample.shape[0]
    mod = _modulation(jnp.concatenate([c_prompt, c_sample], axis=0), ada_w, ada_b)
    mod_p = mod[:, :bp].reshape(DEPTH, bp, N_MOD, D)
    mod_s = mod[:, bp:].reshape(DEPTH, 1, bs, N_MOD * D)
    y_p, ca_p, sb_p, wb_p, gc_p, cf_p = _trunk(x_prompt, mod_p, p, (None,) * 5, sample=False)
    y_s, ca_s, sb_s, wb_s, gc_s, cf_s = _trunk(
        x_sample.reshape(1, bs, D), mod_s, p,
        (state_conv_a, state_shift_b, state_wkv_b, state_gla_c, state_conv_ffn), sample=True)
    return (y_p, y_s.reshape(bs, 1, D), ca_p, ca_s, sb_p, sb_s, wb_p, wb_s, gc_p, gc_s, cf_p, cf_s)
```

```python
import functools

import jax
import jax.numpy as jnp
import numpy as np
from jax import lax
from jax.experimental import pallas as pl
from jax.experimental.pallas import tpu as pltpu

F32, BF16 = jnp.float32, jnp.bfloat16

D = 1024
DEPTH = 4
N_MOD = 6
CONV_W = 3
RMS_EPS = 1e-6
RWKV_HEADS, RWKV_HEAD = 16, 64
RWKV_GN_EPS = 64e-5
GLA_HEADS, GLA_DK, GLA_DV = 4, 128, 256
GLA_DK_TOTAL, GLA_DV_TOTAL = 512, 1024
GLA_GATE_NORMALIZER = 16.0
D_FF = 2816

LANES = 128
CARRY_ROWS = 8
CHUNK = 64
TM = 256
VMEM_LIMIT = 56 * 1024 * 1024


def _rms(x, g):
    return x * lax.rsqrt(jnp.mean(x * x, axis=-1, keepdims=True) + RMS_EPS) * g


def _silu(x):
    return x * jax.nn.sigmoid(x)


def _softplus(x):
    return jnp.maximum(x, 0.0) + jnp.log1p(jnp.exp(-jnp.abs(x)))


def _bdot(a, w):
    return jnp.dot(a.astype(BF16), w, preferred_element_type=F32)


def _dot_nt(a, b):
    return lax.dot_general(a, b, (((1,), (1,)), ((), ())), preferred_element_type=F32)


def _dot_tn(a, b):
    return lax.dot_general(a, b, (((0,), (0,)), ((), ())), preferred_element_type=F32)


_NN = (((1,), (0,)), ((), ()))
_NT = (((1,), (1,)), ((), ()))


def _split2(x):
    hi = x.astype(BF16)
    return hi, (x - hi.astype(F32)).astype(BF16)


def _dot3s(a, b, dims=_NN):
    dg = functools.partial(lax.dot_general, dimension_numbers=dims, preferred_element_type=F32)
    return dg(a[0], b[0]) + (dg(a[0], b[1]) + dg(a[1], b[0]))


def _split3(x):
    hi = x.astype(BF16)
    r1 = x - hi.astype(F32)
    mid = r1.astype(BF16)
    lo = (r1 - mid.astype(F32)).astype(BF16)
    return hi, mid, lo


def _dot01(m01, x):
    hi, mid, lo = _split3(x)
    return (jnp.dot(m01, hi, preferred_element_type=F32)
            + jnp.dot(m01, mid, preferred_element_type=F32)
            + jnp.dot(m01, lo, preferred_element_type=F32))


def _x01(x, m01):
    hi, mid, lo = _split3(x)
    return (jnp.dot(hi, m01, preferred_element_type=F32)
            + jnp.dot(mid, m01, preferred_element_type=F32)
            + jnp.dot(lo, m01, preferred_element_type=F32))


def _segsum(x, seg1, seg2):
    return _x01(_x01(x, seg1), seg2)


def _mod(mod_ref, i, sample):
    return mod_ref[:, i * D:(i + 1) * D] if sample else mod_ref[i:i + 1, :]


def _shift_rows(x, k, carry):
    row = lax.broadcasted_iota(jnp.int32, (x.shape[0], 1), 0)
    y = pltpu.roll(x, k, axis=0)
    for j in range(k):
        src = CARRY_ROWS - k + j
        y = jnp.where(row == j, carry[src:src + 1, :], y)
    return y


def _whole(shape):
    nd = len(shape)
    return pl.BlockSpec(tuple(shape), lambda b, t: (0,) * nd, pipeline_mode=pl.Buffered(1))


def _rows(tm, width):
    return pl.BlockSpec((None, tm, width), lambda b, t: (b, t, 0))


def _per_seq(r, width):
    return pl.BlockSpec((None, r, width), lambda b, t: (b, 0, 0))


def _params():
    return pltpu.CompilerParams(dimension_semantics=("arbitrary", "arbitrary"),
                                vmem_limit_bytes=VMEM_LIMIT)


def _mod_kernel(c_ref, w_ref, b_ref, o_ref):
    c = c_ref[...]
    o_ref[...] = _bdot(_silu(c), w_ref[...].astype(BF16)) + b_ref[...]


def _modulation(c_all, ada_w, ada_b):
    n = c_all.shape[0]
    tn = 1536
    return pl.pallas_call(
        _mod_kernel,
        grid=(DEPTH, N_MOD * D // tn),
        in_specs=[pl.BlockSpec((n, D), lambda l, j: (0, 0)),
                  pl.BlockSpec((None, D, tn), lambda l, j: (l, 0, j)),
                  pl.BlockSpec((None, 1, tn), lambda l, j: (l, 0, j))],
        out_specs=pl.BlockSpec((None, n, tn), lambda l, j: (l, 0, j)),
        out_shape=jax.ShapeDtypeStruct((DEPTH, n, N_MOD * D), F32),
        compiler_params=_params(),
        name="adaln_mod",
    )(c_all, ada_w, ada_b.reshape(DEPTH, 1, N_MOD * D))


def _sconv_kernel(*refs, sample, tm, nt):
    if sample:
        h_ref, mod_ref, ng_ref, win_ref, cw_ref, wout_ref, st_ref, o_ref, nb_ref = refs
    else:
        h_ref, mod_ref, ng_ref, win_ref, cw_ref, wout_ref, o_ref, nb_ref, carry_ref = refs
    mod = lambda i: _mod(mod_ref, i, sample)
    h = h_ref[...]
    u = _rms(h, ng_ref[0:1, :]) * (1.0 + mod(1)) + mod(0)
    p = _bdot(u, win_ref[...])
    bg = p[:, :D]
    z = p[:, D:2 * D] * p[:, 2 * D:]
    cw = cw_ref[...]
    if sample:
        y = st_ref[:, :D] * cw[0:1] + st_ref[:, D:] * cw[1:2] + z * cw[2:3]
        nb_ref[:, :D] = st_ref[:, D:]
        nb_ref[:, D:] = z
    else:
        t = pl.program_id(1)

        @pl.when(t == 0)
        def _():
            carry_ref[...] = jnp.zeros_like(carry_ref)

        c = carry_ref[...]
        y = _shift_rows(z, 2, c) * cw[0:1] + _shift_rows(z, 1, c) * cw[1:2] + z * cw[2:3]
        carry_ref[...] = z[tm - CARRY_ROWS:, :]

        @pl.when(t == nt - 1)
        def _():
            nb_ref[...] = carry_ref[CARRY_ROWS - 2:, :]

    out = _bdot(bg * y, wout_ref[...])
    o_ref[...] = h + mod(2) * _rms(out, ng_ref[1:2, :])


def _sconv_layer(h, mod, ng, w_in, conv_w, w_out, state, *, sample):
    nb, rows, _ = h.shape
    tm = rows if sample else TM
    nt = rows // tm
    in_specs = [_rows(tm, D), _per_seq(*mod.shape[1:]), _whole(ng.shape), _whole(w_in.shape),
                _whole(conv_w.shape), _whole(w_out.shape)]
    args = [h, mod, ng, w_in, conv_w, w_out]
    if sample:
        in_specs.append(_per_seq(rows, 2 * D))
        args.append(state)
        nb_shape, nb_spec, scratch = (nb, rows, 2 * D), _per_seq(rows, 2 * D), []
    else:
        nb_shape, nb_spec = (nb, 2, D), _per_seq(2, D)
        scratch = [pltpu.VMEM((CARRY_ROWS, D), F32)]
    return pl.pallas_call(
        functools.partial(_sconv_kernel, sample=sample, tm=tm, nt=nt),
        grid=(nb, nt), in_specs=in_specs,
        out_specs=[_rows(tm, D), nb_spec],
        out_shape=[jax.ShapeDtypeStruct(h.shape, F32), jax.ShapeDtypeStruct(nb_shape, F32)],
        scratch_shapes=scratch, compiler_params=_params(),
        name="sconv_sample" if sample else "sconv_prompt",
    )(*args)


def _ffn_kernel(*refs, sample, tm, nt):
    if sample:
        h_ref, mod_ref, ng_ref, wup_ref, cw_ref, cb_ref, wdn_ref, st_ref, o_ref, nb_ref = refs
    else:
        h_ref, mod_ref, ng_ref, wup_ref, cw_ref, cb_ref, wdn_ref, o_ref, nb_ref, carry_ref = refs
    mod = lambda i: _mod(mod_ref, i, sample)
    cw = cw_ref[...]
    bdot = functools.partial(jnp.dot, preferred_element_type=F32)

    def pre(rows):
        h = h_ref[rows, :]
        return h, (_rms(h, ng_ref[2:3, :]) * (1.0 + mod(4)) + mod(3)).astype(BF16)

    def act(hc, val):
        return (_silu(hc + cb_ref[...]) * val).astype(BF16)

    def post(rows, h, out):
        o_ref[rows, :] = h + mod(5) * _rms(out, ng_ref[3:4, :])

    if sample:
        rows = slice(0, tm)
        h, u = pre(rows)
        g = bdot(u, wup_ref[:, :D_FF])
        val = bdot(u, wup_ref[:, D_FF:])
        hc = st_ref[:, :D_FF] * cw[0:1] + st_ref[:, D_FF:] * cw[1:2] + g * cw[2:3]
        nb_ref[:, :D_FF] = st_ref[:, D_FF:]
        nb_ref[:, D_FF:] = g
        post(rows, h, bdot(act(hc, val), wdn_ref[...]))
        return

    t = pl.program_id(1)

    @pl.when(t == 0)
    def _():
        carry_ref[...] = jnp.zeros_like(carry_ref)

    def conv(g, c):
        return _shift_rows(g, 2, c) * cw[0:1] + _shift_rows(g, 1, c) * cw[1:2] + g * cw[2:3]

    half = tm // 2
    rows_a, rows_b = slice(0, half), slice(half, tm)
    h_a, u_a = pre(rows_a)
    g_a = bdot(u_a, wup_ref[:, :D_FF])
    h_b, u_b = pre(rows_b)
    v_a = bdot(u_a, wup_ref[:, D_FF:])
    g_b = bdot(u_b, wup_ref[:, :D_FF])
    act_a = act(conv(g_a, carry_ref[...]), v_a)
    v_b = bdot(u_b, wup_ref[:, D_FF:])
    d_a = bdot(act_a, wdn_ref[...])
    act_b = act(conv(g_b, g_a[half - CARRY_ROWS:, :]), v_b)
    carry_ref[...] = g_b[half - CARRY_ROWS:, :]
    d_b = bdot(act_b, wdn_ref[...])
    post(rows_a, h_a, d_a)
    post(rows_b, h_b, d_b)

    @pl.when(t == nt - 1)
    def _():
        nb_ref[...] = carry_ref[CARRY_ROWS - 2:, :]


def _ffn_layer(h, mod, ng, w_up, conv_w, conv_b, w_down, state, *, sample):
    nb, rows, _ = h.shape
    tm = rows if sample else 2 * TM
    nt = rows // tm
    in_specs = [_rows(tm, D), _per_seq(*mod.shape[1:]), _whole(ng.shape), _whole(w_up.shape),
                _whole(conv_w.shape), _whole(conv_b.shape), _whole(w_down.shape)]
    args = [h, mod, ng, w_up, conv_w, conv_b, w_down]
    if sample:
        in_specs.append(_per_seq(rows, 2 * D_FF))
        args.append(state)
        nb_shape, nb_spec, scratch = (nb, rows, 2 * D_FF), _per_seq(rows, 2 * D_FF), []
    else:
        nb_shape, nb_spec = (nb, 2, D_FF), _per_seq(2, D_FF)
        scratch = [pltpu.VMEM((CARRY_ROWS, D_FF), F32)]
    return pl.pallas_call(
        functools.partial(_ffn_kernel, sample=sample, tm=tm, nt=nt),
        grid=(nb, nt), in_specs=in_specs,
        out_specs=[_rows(tm, D), nb_spec],
        out_shape=[jax.ShapeDtypeStruct(h.shape, F32), jax.ShapeDtypeStruct(nb_shape, F32)],
        scratch_shapes=scratch, compiler_params=_params(),
        name="ffn_sample" if sample else "ffn_prompt",
    )(*args)


def _rwkv_pre_kernel(*refs, sample, tm, nt):
    (h_ref, mod_ref, ng_ref, mu_ref, wrkv_ref, w0_ref, w1_ref, w2_ref, a0_ref, a1_ref, a2_ref,
     g1_ref, g2_ref, kk_ref, ka_ref, rk_ref, seg1_ref, seg2_ref) = refs[:18]
    if sample:
        shift_ref = refs[18]
        outs = refs[19:]
    else:
        tri_ref, sel_ref = refs[18:20]
        outs = refs[20:-1]
        carry_ref = refs[-1]
    at_ref, bt_ref, kt_ref, rt_ref, v_ref, g_ref, bonus_ref, wc_ref, sh_ref = outs
    mod = lambda i: _mod(mod_ref, i, sample)
    seg1, seg2 = seg1_ref[...], seg2_ref[...]

    u = _rms(h_ref[...], ng_ref[0:1, :]) * (1.0 + mod(1)) + mod(0)
    if sample:
        prev = shift_ref[...]
        sh_ref[...] = u
    else:
        t = pl.program_id(1)

        @pl.when(t == 0)
        def _():
            carry_ref[...] = jnp.zeros_like(carry_ref)

        prev = _shift_rows(u, 1, carry_ref[...])
        carry_ref[...] = u[tm - CARRY_ROWS:, :]

        @pl.when(t == nt - 1)
        def _():
            sh_ref[...] = carry_ref[CARRY_ROWS - 1:, :]

    xx = prev - u
    mix = lambda i: u + xx * mu_ref[i:i + 1, :]
    r = _bdot(mix(0), wrkv_ref[0])
    k = _bdot(mix(2), wrkv_ref[1])
    v = _bdot(mix(3), wrkv_ref[2])
    z = w0_ref[...] + _bdot(jnp.tanh(_bdot(mix(1), w1_ref[...])), w2_ref[...])
    lw = -jnp.exp(-_softplus(-z) - 0.5)
    a = jax.nn.sigmoid(a0_ref[...] + _bdot(_bdot(mix(4), a1_ref[...]), a2_ref[...]))
    g_ref[...] = _bdot(jax.nn.sigmoid(_bdot(mix(5), g1_ref[...])), g2_ref[...])

    kk = k * kk_ref[...]
    kk = kk / jnp.maximum(jnp.sqrt(_segsum(kk * kk, seg1, seg2)), 1e-12)
    k2 = k * (1.0 + (a - 1.0) * ka_ref[...])
    bonus_ref[...] = _segsum(r * k2 * rk_ref[...], seg1, seg2) * v

    if sample:
        lc = lw
        wc_ref[...] = jnp.exp(lw)
    else:
        lc = _dot01(tri_ref[...], lw)
        wc_ref[...] = jnp.exp(_dot01(sel_ref[...], lc))
    e_neg = jnp.exp(-lc)
    at_ref[...] = -kk * jnp.exp(lc - lw)
    bt_ref[...] = kk * a * e_neg
    kt_ref[...] = k2 * e_neg
    rt_ref[...] = (r * jnp.exp(lc)).astype(rt_ref.dtype)
    v_ref[...] = v


def _rwkv_core_kernel(at_ref, bt_ref, kt_ref, rt_ref, v_ref, wc_ref, y_ref, sout_ref, s_ref, *, nt):
    t = pl.program_id(1)

    @pl.when(t == 0)
    def _():
        s_ref[...] = jnp.zeros_like(s_ref)

    rows2 = 2 * CHUNK
    lane_lo = lax.broadcasted_iota(jnp.int32, (1, LANES), 1) < RWKV_HEAD
    ri = lax.broadcasted_iota(jnp.int32, (rows2, rows2), 0)
    ci = lax.broadcasted_iota(jnp.int32, (rows2, rows2), 1)
    shift = CHUNK.bit_length() - 1
    same = (ri >> shift) == (ci >> shift)
    m_strict = same & (ci < ri)
    m_incl = same & (ci <= ri)
    eye = (ri == ci).astype(F32)

    def stack(x):
        zero = jnp.zeros_like(x)
        return jnp.concatenate([jnp.where(lane_lo, x, zero), jnp.where(lane_lo, zero, x)], axis=0)

    pairs = range(RWKV_HEADS // 2)
    lanes = [slice(p * LANES, (p + 1) * LANES) for p in pairs]
    bdot = functools.partial(jnp.dot, preferred_element_type=F32)

    r1 = rows2
    cat0 = lambda *xs: jnp.concatenate(xs, axis=0)
    cat1 = lambda *xs: jnp.concatenate(xs, axis=1)
    zero = jnp.zeros((r1, r1), BF16)
    halves = lambda w: w[:, :r1] + w[:, r1:]

    def rhs3(y):
        return cat0(cat1(y[0], y[1]), cat1(y[0], zero))

    def rhs3_t(y):
        return cat0(cat1(y[0], y[0]), cat1(y[1], zero))

    a_s = [_split2(stack(at_ref[:, sl])) for sl in lanes]
    a_k = [cat1(*a_s[p]) for p in pairs]
    b_s = [_split2(stack(bt_ref[:, sl])) for sl in lanes]
    k_s = [_split2(stack(kt_ref[:, sl])) for sl in lanes]
    v_s = [_split2(stack(v_ref[:, sl])) for sl in lanes]
    r_s = [stack(rt_ref[:, sl]) for sl in lanes]
    bk_hi = [cat0(b_s[p][0], k_s[p][0]) for p in pairs]
    a_ak = [_split2(jnp.where(m_strict, halves(_dot_nt(a_k[p], rhs3_t(k_s[p]))), 0.0)) for p in pairs]
    sc_r = [_dot_nt(r_s[p], bk_hi[p]) for p in pairs]
    a_r = [cat1(jnp.where(m_incl, sc_r[p][:, :r1], 0.0).astype(BF16),
                jnp.where(m_incl, sc_r[p][:, r1:], 0.0).astype(BF16)) for p in pairs]
    pw = [jnp.where(m_strict, halves(_dot_nt(a_k[p], rhs3_t(b_s[p]))), 0.0) for p in pairs]
    tinv = [eye + pw[p] for p in pairs]
    pws = [_split2(pw[p]) for p in pairs]
    pw = [halves(bdot(cat1(*pws[p]), rhs3(pws[p]))) for p in pairs]
    for step in range(5):
        pws = [_split2(pw[p]) for p in pairs]
        tis = [_split2(tinv[p]) for p in pairs]
        if step < 4:
            w = [bdot(cat0(cat1(*pws[p]), cat1(*tis[p])), rhs3(pws[p])) for p in pairs]
            pw = [halves(w[p][:r1]) for p in pairs]
            tinv = [tinv[p] + halves(w[p][r1:]) for p in pairs]
        else:
            tinv = [tinv[p] + halves(bdot(cat1(*tis[p]), rhs3(pws[p]))) for p in pairs]
    s = [s_ref[p] for p in pairs]
    ss = [_split2(s[p]) for p in pairs]
    x = [halves(_dot_nt(a_k[p], rhs3_t(ss[p]))) + halves(bdot(cat1(*a_ak[p]), rhs3(v_s[p]))) for p in pairs]
    ub = [halves(bdot(cat1(*_split2(tinv[p])), rhs3(_split2(x[p])))).astype(BF16) for p in pairs]
    uv = [cat0(ub[p], v_s[p][0]) for p in pairs]
    for p in pairs:
        y = _dot_nt(r_s[p], ss[p][0]) + bdot(a_r[p], uv[p])
        y_ref[:, lanes[p]] = y[:CHUNK, :] + y[CHUNK:, :]
    for p in pairs:
        s_ref[p] = (s[p] + _dot_tn(uv[p], bk_hi[p])) * wc_ref[:, lanes[p]]

    @pl.when(t == nt - 1)
    def _():
        sout_ref[...] = s_ref[...]


def _rwkv_core_sample_kernel(at_ref, bt_ref, kt_ref, rt_ref, v_ref, wc_ref, s_ref, y_ref, sout_ref):
    n = RWKV_HEAD
    eye = (lax.broadcasted_iota(jnp.int32, (n, n), 0) == lax.broadcasted_iota(jnp.int32, (n, n), 1)).astype(F32)
    s = s_ref[...]
    u = jnp.sum(s * at_ref[...], axis=-1, keepdims=True)
    v_col = jnp.sum(eye * v_ref[...], axis=-1, keepdims=True)
    s1 = s + u * bt_ref[...] + v_col * kt_ref[...]
    y_col = jnp.sum(s1 * rt_ref[...], axis=-1, keepdims=True)
    y_ref[...] = jnp.sum(eye * y_col, axis=-2, keepdims=True)
    sout_ref[...] = s1 * wc_ref[...]


def _rwkv_post_kernel(y_ref, g_ref, bonus_ref, h_ref, mod_ref, ng_ref, lnw_ref, lnb_ref, wo_ref,
                      seg1_ref, seg2_ref, o_ref, *, sample):
    mod = lambda i: _mod(mod_ref, i, sample)
    seg1, seg2 = seg1_ref[...], seg2_ref[...]
    y = y_ref[...]
    yc = y - _segsum(y, seg1, seg2) * (1.0 / RWKV_HEAD)
    var = _segsum(yc * yc, seg1, seg2) * (1.0 / RWKV_HEAD)
    yn = yc * lax.rsqrt(var + RWKV_GN_EPS) * lnw_ref[...] + lnb_ref[...] + bonus_ref[...]
    out = _bdot(yn * g_ref[...], wo_ref[...])
    o_ref[...] = h_ref[...] + mod(2) * _rms(out, ng_ref[1:2, :])


def _np_bf16(a):
    return jnp.asarray(np.asarray(a, np.float32), BF16)


def _rwkv_layer(h, mod, ng, w, state_shift, state_wkv, *, sample):
    nb, rows, _ = h.shape
    tm = rows if sample else TM
    nt = rows // tm
    lane_head = np.arange(D) // RWKV_HEAD
    seg1 = _np_bf16(lane_head[:, None] == np.arange(LANES)[None, :])
    seg2 = _np_bf16(np.arange(LANES)[:, None] == lane_head[None, :])
    pad_c = lambda x: jnp.pad(x, ((0, 0), (0, LANES - x.shape[1]))).astype(BF16)
    pad_r = lambda x: jnp.pad(x, ((0, LANES - x.shape[0]), (0, 0))).astype(BF16)
    row1 = lambda x: x.reshape(1, D)
    consts = [ng, w["mu"], w["w_rkv"].astype(BF16), row1(w["w0"]), pad_c(w["w1"]), pad_r(w["w2"]),
              row1(w["a0"]), pad_c(w["a1"]), pad_r(w["a2"]), w["g1"].astype(BF16), w["g2"].astype(BF16),
              row1(w["k_k"]), row1(w["k_a"]), row1(w["r_k"]), seg1, seg2]
    in_specs = [_rows(tm, D), _per_seq(*mod.shape[1:])] + [_whole(c.shape) for c in consts]
    args = [h, mod] + consts
    act_dt = F32 if sample else BF16
    nchunk = 1 if sample else tm // CHUNK
    if sample:
        in_specs.append(_per_seq(rows, D))
        args.append(state_shift)
        wc_shape, wc_spec = (nb, rows, D), _rows(tm, D)
        sh_shape, sh_spec = (nb, rows, D), _rows(tm, D)
        scratch = []
    else:
        ti = np.arange(tm)
        tri = _np_bf16((ti[:, None] // CHUNK == ti[None, :] // CHUNK) & (ti[None, :] <= ti[:, None]))
        sel = _np_bf16(ti[None, :] == (np.arange(CARRY_ROWS)[:, None] * CHUNK + CHUNK - 1))
        in_specs += [_whole(tri.shape), _whole(sel.shape)]
        args += [tri, sel]
        wc_shape = (nb, nt, CARRY_ROWS, D)
        wc_spec = pl.BlockSpec((None, None, CARRY_ROWS, D), lambda b, t: (b, t, 0, 0))
        sh_shape, sh_spec = (nb, 1, D), _per_seq(1, D)
        scratch = [pltpu.VMEM((CARRY_ROWS, D), F32)]
    act = jax.ShapeDtypeStruct(h.shape, act_dt)
    f32 = jax.ShapeDtypeStruct(h.shape, F32)
    at, bt, kt, rt, vv, g, bonus, wc, new_shift = pl.pallas_call(
        functools.partial(_rwkv_pre_kernel, sample=sample, tm=tm, nt=nt),
        grid=(nb, nt), in_specs=in_specs,
        out_specs=[_rows(tm, D)] * 7 + [wc_spec, sh_spec],
        out_shape=[f32, f32, f32, act, f32, f32, f32, jax.ShapeDtypeStruct(wc_shape, F32),
                                jax.ShapeDtypeStruct(sh_shape, F32)],
        scratch_shapes=scratch, compiler_params=_params(),
        name="rwkv_pre_sample" if sample else "rwkv_pre_prompt",
    )(*args)

    hh, n = RWKV_HEADS, RWKV_HEAD
    if sample:
        sb = 8
        nseq = rows
        hv = lambda x: x.reshape(nseq, hh, 1, n)
        vec_spec = pl.BlockSpec((sb, hh, 1, n), lambda i: (i, 0, 0, 0))
        st_spec = pl.BlockSpec((sb, hh, n, n), lambda i: (i, 0, 0, 0))
        y, new_s = pl.pallas_call(
            _rwkv_core_sample_kernel,
            grid=(nseq // sb,),
            in_specs=[vec_spec] * 6 + [st_spec],
            out_specs=[vec_spec, st_spec],
            out_shape=[jax.ShapeDtypeStruct((nseq, hh, 1, n), F32),
                       jax.ShapeDtypeStruct((nseq, hh, n, n), F32)],
            compiler_params=pltpu.CompilerParams(dimension_semantics=("arbitrary",),
                                                 vmem_limit_bytes=VMEM_LIMIT),
            name="rwkv_core_sample",
        )(hv(at), hv(bt), hv(kt), hv(rt), hv(vv), hv(wc), state_wkv)
        y = y.reshape(nb, rows, D)
    else:
        ncht = rows // CHUNK
        wc_rows = wc[:, :, :nchunk, :].reshape(nb, ncht, 1, D)
        npair = hh // 2
        y, s_bd = pl.pallas_call(
            functools.partial(_rwkv_core_kernel, nt=ncht),
            grid=(nb, ncht),
            in_specs=[_rows(CHUNK, D)] * 5 + [pl.BlockSpec((None, None, 1, D), lambda b, t: (b, t, 0, 0))],
            out_specs=[_rows(CHUNK, D),
                       pl.BlockSpec((None, npair, LANES, LANES), lambda b, t: (b, 0, 0, 0))],
            out_shape=[f32, jax.ShapeDtypeStruct((nb, npair, LANES, LANES), F32)],
            scratch_shapes=[pltpu.VMEM((npair, LANES, LANES), F32)],
            compiler_params=_params(),
            name="rwkv_core_prompt",
        )(at, bt, kt, rt, vv, wc_rows)
        s5 = s_bd.reshape(nb, npair, 2, n, 2, n)
        new_s = jnp.stack([s5[:, :, 0, :, 0, :], s5[:, :, 1, :, 1, :]], axis=2).reshape(nb, hh, n, n)

    consts = [ng, row1(w["lnx_w"]), row1(w["lnx_b"]), w["w_o"].astype(BF16), seg1, seg2]
    h_new = pl.pallas_call(
        functools.partial(_rwkv_post_kernel, sample=sample),
        grid=(nb, nt),
        in_specs=[_rows(tm, D)] * 4 + [_per_seq(*mod.shape[1:])] + [_whole(c.shape) for c in consts],
        out_specs=_rows(tm, D), out_shape=f32, compiler_params=_params(),
        name="rwkv_post_sample" if sample else "rwkv_post_prompt",
    )(y, g, bonus, h, mod, *consts)
    return h_new, new_shift, new_s


GLA_LEVELS = (1, 2, 4, 8, 16, 32)


def _gla_pre_kernel(h_ref, mod_ref, ng_ref, win_ref, wa1_ref, wa2_ref, ba_ref,
                    q_ref, k_ref, v_ref, og_ref, lg_ref, *, sample):
    mod = lambda i: _mod(mod_ref, i, sample)
    u = (_rms(h_ref[...], ng_ref[0:1, :]) * (1.0 + mod(1)) + mod(0)).astype(BF16)
    dk, dv = GLA_DK_TOTAL, GLA_DV_TOTAL
    q_ref[...] = jnp.dot(u, win_ref[:, :dk], preferred_element_type=F32) * (GLA_DK ** -0.5)
    k_ref[...] = jnp.dot(u, win_ref[:, dk:2 * dk], preferred_element_type=F32)
    v_ref[...] = jnp.dot(u, win_ref[:, 2 * dk:2 * dk + dv], preferred_element_type=F32)
    og_ref[...] = jnp.dot(u, win_ref[:, 2 * dk + dv:], preferred_element_type=F32)
    gate = _bdot(jnp.dot(u, wa1_ref[...], preferred_element_type=F32), wa2_ref[...]) + ba_ref[...]
    lg_ref[...] = -_softplus(-gate) * (1.0 / GLA_GATE_NORMALIZER)


def _gla_core_kernel(q_ref, k_ref, v_ref, lg_ref, lvl_ref, o_ref, sout_ref, s_ref, *, nt):
    t = pl.program_id(1)

    @pl.when(t == 0)
    def _():
        s_ref[...] = jnp.zeros_like(s_ref)

    lg = lg_ref[...]
    pq = _dot01(lvl_ref[...], lg)
    row = lax.broadcasted_iota(jnp.int32, (CHUNK, 1), 0)
    ri = lax.broadcasted_iota(jnp.int32, (CHUNK, CHUNK), 0)
    ci = lax.broadcasted_iota(jnp.int32, (CHUNK, CHUNK), 1)
    nl = len(GLA_LEVELS)
    for hd in range(GLA_HEADS):
        sl = slice(hd * GLA_DK, (hd + 1) * GLA_DK)
        sv = slice(hd * GLA_DV, (hd + 1) * GLA_DV)
        q, k = q_ref[:, sl], k_ref[:, sl]
        vb = v_ref[:, sv].astype(BF16)
        a = jnp.where(ri == ci, _dot_nt(q.astype(BF16), k.astype(BF16)), 0.0)
        for li, m in enumerate(GLA_LEVELS):
            if m == 1:
                pre, suf = lg[:, sl], jnp.zeros((CHUNK, GLA_DK), F32)
            else:
                base = (li - 1) * 2 * CHUNK
                pre, suf = pq[base:base + CHUNK, sl], pq[base + CHUNK:base + 2 * CHUNK, sl]
            lm = m.bit_length() - 1
            second = ((row >> lm) & 1) == 1
            qe = jnp.where(second, q * jnp.exp(pre), 0.0).astype(BF16)
            ke = jnp.where(second, 0.0, k * jnp.exp(suf)).astype(BF16)
            a = a + jnp.where((ri >> (lm + 1)) == (ci >> (lm + 1)), _dot_nt(qe, ke), 0.0)
        base = (nl - 1) * 2 * CHUNK
        g_inc, g_rest = pq[base:base + CHUNK, sl], pq[base + CHUNK:base + 2 * CHUNK, sl]
        s = s_ref[hd]
        o_ref[:, sv] = (jnp.dot(a.astype(BF16), vb, preferred_element_type=F32)
                        + _dot_nt((q * jnp.exp(g_inc)).astype(BF16), s.astype(BF16)))
        s_ref[hd] = (s * jnp.exp(g_inc[CHUNK - 1:CHUNK, :])
                     + _dot_tn(vb, (k * jnp.exp(g_rest)).astype(BF16)))

    @pl.when(t == nt - 1)
    def _():
        sout_ref[...] = s_ref[...]


def _gla_core_sample_kernel(q_ref, k_ref, v_ref, lg_ref, s_ref, o_ref, sout_ref):
    n = GLA_DK
    eye = (lax.broadcasted_iota(jnp.int32, (n, n), 0) == lax.broadcasted_iota(jnp.int32, (n, n), 1)).astype(F32)
    col = lambda x: jnp.sum(eye * x, axis=-1, keepdims=True)
    q, k, v = q_ref[...], k_ref[...], v_ref[...]
    s = s_ref[...]
    decay = jnp.exp(lg_ref[...])
    qk = jnp.sum(q * k, axis=-1, keepdims=True)
    o_ref[...] = qk * v + jnp.sum(col(q * decay) * s, axis=-2, keepdims=True)
    sout_ref[...] = col(decay) * s + col(k) * v


def _gla_post_kernel(o_ref, og_ref, h_ref, mod_ref, ng_ref, on_ref, wo_ref, out_ref, *, sample):
    mod = lambda i: _mod(mod_ref, i, sample)
    og = og_ref[...]
    parts = []
    for hd in range(GLA_HEADS):
        sv = slice(hd * GLA_DV, (hd + 1) * GLA_DV)
        parts.append(_rms(o_ref[:, sv], on_ref[...]) * _silu(og[:, sv]))
    out = _bdot(jnp.concatenate(parts, axis=1), wo_ref[...])
    out_ref[...] = h_ref[...] + mod(2) * _rms(out, ng_ref[1:2, :])


def _gla_level_matrix():
    ti = np.arange(CHUNK)
    blocks = []
    for m in GLA_LEVELS[1:] + (CHUNK,):
        same = ti[:, None] // m == ti[None, :] // m
        blocks.append(same & (ti[None, :] <= ti[:, None]))
        blocks.append(same & (ti[None, :] > ti[:, None]))
    return _np_bf16(np.concatenate(blocks, axis=0))


def _gla_layer(h, mod, ng, w, state, *, sample):
    nb, rows, _ = h.shape
    tm = rows if sample else TM
    nt = rows // tm
    dk, dv, hh = GLA_DK_TOTAL, GLA_DV_TOTAL, GLA_HEADS
    wa1 = jnp.pad(w["wa1"], ((0, 0), (0, LANES - w["wa1"].shape[1]))).astype(BF16)
    wa2 = jnp.pad(w["wa2"], ((0, LANES - w["wa2"].shape[0]), (0, 0))).astype(BF16)
    consts = [ng, w["w_in"].astype(BF16), wa1, wa2, w["ba"].reshape(1, dk)]
    shp = lambda width: jax.ShapeDtypeStruct((nb, rows, width), F32)
    q, k, v, og, lg = pl.pallas_call(
        functools.partial(_gla_pre_kernel, sample=sample),
        grid=(nb, nt),
        in_specs=[_rows(tm, D), _per_seq(*mod.shape[1:])] + [_whole(c.shape) for c in consts],
        out_specs=[_rows(tm, dk), _rows(tm, dk), _rows(tm, dv), _rows(tm, dv), _rows(tm, dk)],
        out_shape=[shp(dk), shp(dk), shp(dv), shp(dv), shp(dk)],
        compiler_params=_params(),
        name="gla_pre_sample" if sample else "gla_pre_prompt",
    )(h, mod, *consts)

    if sample:
        sb = 8
        nseq = rows
        hk = lambda x: x.reshape(nseq, hh, 1, GLA_DK)
        kspec = pl.BlockSpec((sb, hh, 1, GLA_DK), lambda i: (i, 0, 0, 0))
        vspec = pl.BlockSpec((sb, hh, 1, GLA_DV), lambda i: (i, 0, 0, 0))
        sspec = pl.BlockSpec((sb, hh, GLA_DK, GLA_DV), lambda i: (i, 0, 0, 0))
        o, new_s = pl.pallas_call(
            _gla_core_sample_kernel,
            grid=(nseq // sb,),
            in_specs=[kspec, kspec, vspec, kspec, sspec],
            out_specs=[vspec, sspec],
            out_shape=[jax.ShapeDtypeStruct((nseq, hh, 1, GLA_DV), F32),
                       jax.ShapeDtypeStruct((nseq, hh, GLA_DK, GLA_DV), F32)],
            compiler_params=pltpu.CompilerParams(dimension_semantics=("arbitrary",),
                                                 vmem_limit_bytes=VMEM_LIMIT),
            name="gla_core_sample",
        )(hk(q), hk(k), v.reshape(nseq, hh, 1, GLA_DV), hk(lg), state)
        o = o.reshape(nb, rows, dv)
    else:
        ncht = rows // CHUNK
        lvl = _gla_level_matrix()
        o, s_t = pl.pallas_call(
            functools.partial(_gla_core_kernel, nt=ncht),
            grid=(nb, ncht),
            in_specs=[_rows(CHUNK, dk), _rows(CHUNK, dk), _rows(CHUNK, dv), _rows(CHUNK, dk),
                      _whole(lvl.shape)],
            out_specs=[_rows(CHUNK, dv),
                       pl.BlockSpec((None, hh, GLA_DV, GLA_DK), lambda b, t: (b, 0, 0, 0))],
            out_shape=[shp(dv), jax.ShapeDtypeStruct((nb, hh, GLA_DV, GLA_DK), F32)],
            scratch_shapes=[pltpu.VMEM((hh, GLA_DV, GLA_DK), F32)],
            compiler_params=_params(),
            name="gla_core_prompt",
        )(q, k, v, lg, lvl)
        new_s = jnp.swapaxes(s_t, -1, -2)

    consts = [ng, w["onorm_g"].reshape(1, GLA_DV), w["w_o"].astype(BF16)]
    h_new = pl.pallas_call(
        functools.partial(_gla_post_kernel, sample=sample),
        grid=(nb, nt),
        in_specs=[_rows(tm, dv), _rows(tm, dv), _rows(tm, D), _per_seq(*mod.shape[1:])]
                 + [_whole(c.shape) for c in consts],
        out_specs=_rows(tm, D), out_shape=shp(D), compiler_params=_params(),
        name="gla_post_sample" if sample else "gla_post_prompt",
    )(o, og, h, mod, *consts)
    return h_new, new_s


def _trunk(h, mod, p, states, *, sample):
    st_a, st_sb, st_wb, st_gc, st_f = states
    nb, rows, _ = h.shape
    new_a, new_sb, new_wb, new_gc, new_f = [], [], [], [], []
    for l in range(DEPTH):
        kind, j = l % 3, l // 3
        ng = p["norm_g"][l]
        if kind == 0:
            st = st_a[j].reshape(1, rows, 2 * D) if sample else None
            h, nbuf = _sconv_layer(h, mod[l], ng, p["sconv_w_in"][j].astype(BF16), p["sconv_conv_w"][j],
                                   p["sconv_w_out"][j].astype(BF16), st, sample=sample)
            new_a.append(nbuf.reshape(rows, 2, D) if sample else nbuf)
        elif kind == 1:
            w = {n: p["rwkv_" + n][j] for n in ("mu", "w_rkv", "w0", "w1", "w2", "a0", "a1", "a2", "g1", "g2",
                                                  "k_k", "k_a", "r_k", "lnx_w", "lnx_b", "w_o")}
            sh = st_sb[j].reshape(1, rows, D) if sample else None
            wkv = st_wb[j] if sample else None
            h, nsh, ns = _rwkv_layer(h, mod[l], ng, w, sh, wkv, sample=sample)
            new_sb.append(nsh.reshape(rows, D) if sample else nsh.reshape(nb, D))
            new_wb.append(ns)
        else:
            w = {n: p["gla_" + n][j] for n in ("w_in", "wa1", "wa2", "ba", "onorm_g", "w_o")}
            h, ns = _gla_layer(h, mod[l], ng, w, st_gc[j] if sample else None, sample=sample)
            new_gc.append(ns)
        st = st_f[l].reshape(1, rows, 2 * D_FF) if sample else None
        h, nbuf = _ffn_layer(h, mod[l], ng, p["ffn_w_up"][l].astype(BF16), p["ffn_conv_w"][l],
                             p["ffn_conv_b"][l].reshape(1, D_FF), p["ffn_w_down"][l].astype(BF16), st,
                             sample=sample)
        new_f.append(nbuf.reshape(rows, 2, D_FF) if sample else nbuf)
    return (h, jnp.stack(new_a), jnp.stack(new_sb), jnp.stack(new_wb), jnp.stack(new_gc), jnp.stack(new_f))


def kernel(x_prompt, x_sample, state_conv_a, state_shift_b, state_wkv_b, state_gla_c, state_conv_ffn, c_prompt, c_sample, ada_w, ada_b, norm_g, sconv_w_in, sconv_conv_w, sconv_w_out, rwkv_mu, rwkv_w_rkv, rwkv_w0, rwkv_w1, rwkv_w2, rwkv_a0, rwkv_a1, rwkv_a2, rwkv_g1, rwkv_g2, rwkv_k_k, rwkv_k_a, rwkv_r_k, rwkv_lnx_w, rwkv_lnx_b, rwkv_w_o, gla_w_in, gla_wa1, gla_wa2, gla_ba, gla_onorm_g, gla_w_o, ffn_w_up, ffn_conv_w, ffn_conv_b, ffn_w_down):
    p = dict(norm_g=norm_g, sconv_w_in=sconv_w_in, sconv_conv_w=sconv_conv_w, sconv_w_out=sconv_w_out,
             rwkv_mu=rwkv_mu, rwkv_w_rkv=rwkv_w_rkv, rwkv_w0=rwkv_w0, rwkv_w1=rwkv_w1, rwkv_w2=rwkv_w2,
             rwkv_a0=rwkv_a0, rwkv_a1=rwkv_a1, rwkv_a2=rwkv_a2, rwkv_g1=rwkv_g1, rwkv_g2=rwkv_g2,
             rwkv_k_k=rwkv_k_k, rwkv_k_a=rwkv_k_a, rwkv_r_k=rwkv_r_k.reshape(-1, D), rwkv_lnx_w=rwkv_lnx_w,
             rwkv_lnx_b=rwkv_lnx_b, rwkv_w_o=rwkv_w_o,
             gla_w_in=gla_w_in, gla_wa1=gla_wa1, gla_wa2=gla_wa2, gla_ba=gla_ba,
             gla_onorm_g=gla_onorm_g, gla_w_o=gla_w_o,
             ffn_w_up=ffn_w_up, ffn_conv_w=ffn_conv_w, ffn_conv_b=ffn_conv_b, ffn_w_down=ffn_w_down)
    bp, bs = x_prompt.shape[0], x_sample.shape[0]
    mod = _modulation(jnp.concatenate([c_prompt, c_sample], axis=0), ada_w, ada_b)
    mod_p = mod[:, :bp].reshape(DEPTH, bp, N_MOD, D)
    mod_s = mod[:, bp:].reshape(DEPTH, 1, bs, N_MOD * D)
    y_p, ca_p, sb_p, wb_p, gc_p, cf_p = _trunk(x_prompt, mod_p, p, (None,) * 5, sample=False)
    y_s, ca_s, sb_s, wb_s, gc_s, cf_s = _trunk(
        x_sample.reshape(1, bs, D), mod_s, p,
        (state_conv_a, state_shift_b, state_wkv_b, state_gla_c, state_conv_ffn), sample=True)
    return (y_p, y_s.reshape(bs, 1, D), ca_p, ca_s, sb_p, sb_s, wb_p, wb_s, gc_p, gc_s, cf_p, cf_s)
```

```python
import functools

import jax
import jax.numpy as jnp
import numpy as np
from jax import lax
from jax.experimental import pallas as pl
from jax.experimental.pallas import tpu as pltpu

F32, BF16 = jnp.float32, jnp.bfloat16

D = 1024
DEPTH = 4
N_MOD = 6
RMS_EPS = 1e-6
RWKV_HEADS, RWKV_HEAD = 16, 64
RWKV_GN_EPS = 64e-5
GLA_HEADS, GLA_DK, GLA_DV = 4, 128, 256
GLA_DK_TOTAL, GLA_DV_TOTAL = 512, 1024
GLA_GATE_NORMALIZER = 16.0
D_FF = 2816

LANES = 128
CARRY_ROWS = 8
CHUNK = 64
HALF = 256
RWKV_CHUNKS_PER_STEP = 2
GLA_CHUNKS_PER_STEP = 4
SEG_STRIDE = 16
VMEM_LIMIT = 56 * 1024 * 1024


def _rms(x, g):
    return x * lax.rsqrt(jnp.mean(x * x, axis=-1, keepdims=True) + RMS_EPS) * g


def _silu(x):
    return x * jax.nn.sigmoid(x)


def _softplus(x):
    return jnp.maximum(x, 0.0) + jnp.log1p(jnp.exp(-jnp.abs(x)))


_bdot = functools.partial(jnp.dot, preferred_element_type=F32)


def _dot_nt(a, b):
    return lax.dot_general(a, b, (((1,), (1,)), ((), ())), preferred_element_type=F32)


def _dot_tn(a, b):
    return lax.dot_general(a, b, (((0,), (0,)), ((), ())), preferred_element_type=F32)


def _cat0(*xs):
    return jnp.concatenate(xs, axis=0)


def _cat1(*xs):
    return jnp.concatenate(xs, axis=1)


def _split2(x):
    hi = x.astype(BF16)
    return hi, (x - hi.astype(F32)).astype(BF16)


def _split3(x):
    hi = x.astype(BF16)
    r1 = x - hi.astype(F32)
    mid = r1.astype(BF16)
    return hi, mid, (r1 - mid.astype(F32)).astype(BF16)


def _dot01(m01, x, terms):
    parts = (_split2 if terms == 2 else _split3)(x)
    acc = _bdot(m01, parts[0])
    for part in parts[1:]:
        acc = acc + _bdot(m01, part)
    return acc


def _segsum(x, seg1, seg2p):
    hi, lo = _split2(x)
    s = _bdot(hi, seg1) + _bdot(lo, seg1)
    s_hi, s_mid, s_lo = (part.astype(F32) for part in _split3(s))
    packed = s_hi + pltpu.roll(s_mid, SEG_STRIDE, axis=1) + pltpu.roll(s_lo, 2 * SEG_STRIDE, axis=1)
    return _bdot(packed.astype(BF16), seg2p)


def _mod(mod_ref, i, sample):
    return mod_ref[:, i * D:(i + 1) * D] if sample else mod_ref[i:i + 1, :]


def _shift_rows(x, k, carry):
    row = lax.broadcasted_iota(jnp.int32, (x.shape[0], 1), 0)
    y = pltpu.roll(x, k, axis=0)
    for j in range(k):
        src = CARRY_ROWS - k + j
        y = jnp.where(row == j, carry[src:src + 1, :], y)
    return y


def _half_rows(tm, sample):
    return [slice(0, tm)] if sample else [slice(0, tm // 2), slice(tm // 2, tm)]


def _interleave(gens):
    live = list(gens)
    while live:
        for g in list(live):
            if next(g, StopIteration) is StopIteration:
                live.remove(g)


def _whole(shape):
    nd = len(shape)
    return pl.BlockSpec(tuple(shape), lambda b, t: (0,) * nd, pipeline_mode=pl.Buffered(1))


def _layer(arr, l):
    nd = arr.ndim - 1
    return pl.BlockSpec((None,) + tuple(arr.shape[1:]), lambda b, t: (l,) + (0,) * nd,
                        pipeline_mode=pl.Buffered(1))


def _mod_spec(mod, l):
    return pl.BlockSpec((None, None) + tuple(mod.shape[2:]), lambda b, t: (l, b, 0, 0))


def _rows(tm, width):
    return pl.BlockSpec((None, tm, width), lambda b, t: (b, t, 0))


def _per_seq(r, width):
    return pl.BlockSpec((None, r, width), lambda b, t: (b, 0, 0))


def _params(n=2):
    return pltpu.CompilerParams(dimension_semantics=("arbitrary",) * n, vmem_limit_bytes=VMEM_LIMIT)


def _np_bf16(a):
    return jnp.asarray(np.asarray(a, np.float32), BF16)


def _mod_kernel(c_ref, w_ref, b_ref, o_ref):
    o_ref[...] = _bdot(_silu(c_ref[...]).astype(BF16), w_ref[...].astype(BF16)) + b_ref[...]


def _modulation(c_all, ada_w, ada_b):
    n = c_all.shape[0]
    tn = 1536
    return pl.pallas_call(
        _mod_kernel,
        grid=(DEPTH, N_MOD * D // tn),
        in_specs=[pl.BlockSpec((n, D), lambda l, j: (0, 0)),
                  pl.BlockSpec((None, D, tn), lambda l, j: (l, 0, j)),
                  pl.BlockSpec((None, 1, tn), lambda l, j: (l, 0, j))],
        out_specs=pl.BlockSpec((None, n, tn), lambda l, j: (l, 0, j)),
        out_shape=jax.ShapeDtypeStruct((DEPTH, n, N_MOD * D), F32),
        compiler_params=_params(),
        name="adaln_mod",
    )(c_all, ada_w, ada_b.reshape(DEPTH, 1, N_MOD * D))


def _sconv_kernel(*refs, sample, tm, nt):
    if sample:
        h_ref, mod_ref, ng_ref, win_ref, cw_ref, wout_ref, st_ref, o_ref, nb_ref = refs
    else:
        h_ref, mod_ref, ng_ref, win_ref, cw_ref, wout_ref, o_ref, nb_ref, carry_ref = refs
        t = pl.program_id(1)

        @pl.when(t == 0)
        def _():
            carry_ref[...] = jnp.zeros_like(carry_ref)

    mod = lambda i: _mod(mod_ref, i, sample)
    cw = cw_ref[...]
    tails = {}

    def half(i, rows):
        n = rows.stop - rows.start
        h = h_ref[rows, :]
        u = (_rms(h, ng_ref[0:1, :]) * (1.0 + mod(1)) + mod(0)).astype(BF16)
        p = _bdot(u, win_ref[...])
        yield
        bg = p[:, :D]
        z = p[:, D:2 * D] * p[:, 2 * D:]
        if sample:
            y = st_ref[:, :D] * cw[0:1] + st_ref[:, D:] * cw[1:2] + z * cw[2:3]
            nb_ref[:, :D] = st_ref[:, D:]
            nb_ref[:, D:] = z
        else:
            c = carry_ref[...] if i == 0 else tails[i - 1]
            tails[i] = z[n - CARRY_ROWS:, :]
            y = _shift_rows(z, 2, c) * cw[0:1] + _shift_rows(z, 1, c) * cw[1:2] + z * cw[2:3]
        out = _bdot((bg * y).astype(BF16), wout_ref[...])
        yield
        o_ref[rows, :] = h + mod(2) * _rms(out, ng_ref[1:2, :])

    halves = _half_rows(tm, sample)
    _interleave([half(i, rows) for i, rows in enumerate(halves)])
    if not sample:
        carry_ref[...] = tails[len(halves) - 1]

        @pl.when(t == nt - 1)
        def _():
            nb_ref[...] = carry_ref[CARRY_ROWS - 2:, :]


def _sconv_layer(h, mod, l, j, P, state, *, sample):
    nb, rows, _ = h.shape
    tm = rows if sample else 2 * HALF
    nt = rows // tm
    consts = [(P["norm_g"], l), (P["sconv_w_in"], j), (P["sconv_conv_w"], j), (P["sconv_w_out"], j)]
    in_specs = [_rows(tm, D), _mod_spec(mod, l)] + [_layer(a, i) for a, i in consts]
    args = [h, mod] + [a for a, _ in consts]
    if sample:
        in_specs.append(_per_seq(rows, 2 * D))
        args.append(state)
        nb_shape, nb_spec, scratch = (nb, rows, 2 * D), _per_seq(rows, 2 * D), []
    else:
        nb_shape, nb_spec = (nb, 2, D), _per_seq(2, D)
        scratch = [pltpu.VMEM((CARRY_ROWS, D), F32)]
    return pl.pallas_call(
        functools.partial(_sconv_kernel, sample=sample, tm=tm, nt=nt),
        grid=(nb, nt), in_specs=in_specs,
        out_specs=[_rows(tm, D), nb_spec],
        out_shape=[jax.ShapeDtypeStruct(h.shape, F32), jax.ShapeDtypeStruct(nb_shape, F32)],
        scratch_shapes=scratch, compiler_params=_params(),
        name="sconv_sample" if sample else "sconv_prompt",
    )(*args)


def _ffn_kernel(*refs, sample, tm, nt):
    if sample:
        h_ref, mod_ref, ng_ref, wup_ref, cw_ref, cb_ref, wdn_ref, st_ref, o_ref, nb_ref = refs
    else:
        h_ref, mod_ref, ng_ref, wup_ref, cw_ref, cb_ref, wdn_ref, o_ref, nb_ref, carry_ref = refs
    mod = lambda i: _mod(mod_ref, i, sample)
    cw = cw_ref[...]

    def pre(rows):
        h = h_ref[rows, :]
        return h, (_rms(h, ng_ref[2:3, :]) * (1.0 + mod(4)) + mod(3)).astype(BF16)

    def act(hc, val):
        return (_silu(hc + cb_ref[...]) * val).astype(BF16)

    def post(rows, h, out):
        o_ref[rows, :] = h + mod(5) * _rms(out, ng_ref[3:4, :])

    if sample:
        rows = slice(0, tm)
        h, u = pre(rows)
        g = _bdot(u, wup_ref[:, :D_FF])
        val = _bdot(u, wup_ref[:, D_FF:])
        hc = st_ref[:, :D_FF] * cw[0:1] + st_ref[:, D_FF:] * cw[1:2] + g * cw[2:3]
        nb_ref[:, :D_FF] = st_ref[:, D_FF:]
        nb_ref[:, D_FF:] = g
        post(rows, h, _bdot(act(hc, val), wdn_ref[...]))
        return

    t = pl.program_id(1)

    @pl.when(t == 0)
    def _():
        carry_ref[...] = jnp.zeros_like(carry_ref)

    def conv(g, c):
        return _shift_rows(g, 2, c) * cw[0:1] + _shift_rows(g, 1, c) * cw[1:2] + g * cw[2:3]

    half = tm // 2
    rows_a, rows_b = slice(0, half), slice(half, tm)
    h_a, u_a = pre(rows_a)
    g_a = _bdot(u_a, wup_ref[:, :D_FF])
    h_b, u_b = pre(rows_b)
    v_a = _bdot(u_a, wup_ref[:, D_FF:])
    g_b = _bdot(u_b, wup_ref[:, :D_FF])
    act_a = act(conv(g_a, carry_ref[...]), v_a)
    v_b = _bdot(u_b, wup_ref[:, D_FF:])
    d_a = _bdot(act_a, wdn_ref[...])
    act_b = act(conv(g_b, g_a[half - CARRY_ROWS:, :]), v_b)
    carry_ref[...] = g_b[half - CARRY_ROWS:, :]
    d_b = _bdot(act_b, wdn_ref[...])
    post(rows_a, h_a, d_a)
    post(rows_b, h_b, d_b)

    @pl.when(t == nt - 1)
    def _():
        nb_ref[...] = carry_ref[CARRY_ROWS - 2:, :]


def _ffn_layer(h, mod, l, P, state, *, sample):
    nb, rows, _ = h.shape
    tm = rows if sample else 2 * HALF
    nt = rows // tm
    consts = [P["norm_g"], P["ffn_w_up"], P["ffn_conv_w"], P["ffn_conv_b"], P["ffn_w_down"]]
    in_specs = [_rows(tm, D), _mod_spec(mod, l)] + [_layer(a, l) for a in consts]
    args = [h, mod] + consts
    if sample:
        in_specs.append(_per_seq(rows, 2 * D_FF))
        args.append(state)
        nb_shape, nb_spec, scratch = (nb, rows, 2 * D_FF), _per_seq(rows, 2 * D_FF), []
    else:
        nb_shape, nb_spec = (nb, 2, D_FF), _per_seq(2, D_FF)
        scratch = [pltpu.VMEM((CARRY_ROWS, D_FF), F32)]
    return pl.pallas_call(
        functools.partial(_ffn_kernel, sample=sample, tm=tm, nt=nt),
        grid=(nb, nt), in_specs=in_specs,
        out_specs=[_rows(tm, D), nb_spec],
        out_shape=[jax.ShapeDtypeStruct(h.shape, F32), jax.ShapeDtypeStruct(nb_shape, F32)],
        scratch_shapes=scratch, compiler_params=_params(),
        name="ffn_sample" if sample else "ffn_prompt",
    )(*args)


def _rwkv_pre_kernel(*refs, sample, tm, nt):
    (h_ref, mod_ref, ng_ref, mu_ref, wrkv_ref, w0_ref, w1_ref, w2_ref, a0_ref, a1_ref, a2_ref,
     g1_ref, g2_ref, kk_ref, ka_ref, rk_ref, seg1_ref, seg2_ref) = refs[:18]
    if sample:
        shift_ref = refs[18]
        outs = refs[19:]
    else:
        tri_ref, sel_ref = refs[18:20]
        outs = refs[20:-1]
        carry_ref = refs[-1]
        t = pl.program_id(1)

        @pl.when(t == 0)
        def _():
            carry_ref[...] = jnp.zeros_like(carry_ref)

    at_ref, bt_ref, kt_ref, rt_ref, v_ref, g_ref, bonus_ref, wc_ref, sh_ref = outs
    mod = lambda i: _mod(mod_ref, i, sample)
    seg = lambda x: _segsum(x, seg1_ref[...], seg2_ref[...])
    tails = {}

    def half(i, rows):
        n = rows.stop - rows.start
        u = _rms(h_ref[rows, :], ng_ref[0:1, :]) * (1.0 + mod(1)) + mod(0)
        if sample:
            prev = shift_ref[...]
            sh_ref[...] = u
        else:
            tails[i] = u[n - CARRY_ROWS:, :]
            prev = _shift_rows(u, 1, carry_ref[...] if i == 0 else tails[i - 1])
        xx = prev - u
        mix = lambda m: (u + xx * mu_ref[m:m + 1, :]).astype(BF16)
        r = _bdot(mix(0), wrkv_ref[0])
        k = _bdot(mix(2), wrkv_ref[1])
        v = _bdot(mix(3), wrkv_ref[2])
        zw = _bdot(mix(1), w1_ref[...])
        za = _bdot(mix(4), a1_ref[...])
        zg = _bdot(mix(5), g1_ref[...])
        yield
        v_ref[rows, :] = v
        z = w0_ref[...] + _bdot(jnp.tanh(zw).astype(BF16), w2_ref[...])
        a = jax.nn.sigmoid(a0_ref[...] + _bdot(za.astype(BF16), a2_ref[...]))
        g_ref[rows, :] = _bdot(jax.nn.sigmoid(zg).astype(BF16), g2_ref[...])
        lw = -jnp.exp(-_softplus(-z) - 0.5)
        kk = k * kk_ref[...]
        k2 = k * (1.0 + (a - 1.0) * ka_ref[...])
        ss = seg(kk * kk)
        rk = seg(r * k2 * rk_ref[...])
        if sample:
            lc = lw
        else:
            lc = _dot01(tri_ref[...], lw, 2)
        yield
        kk = kk / jnp.maximum(jnp.sqrt(ss), 1e-12)
        bonus_ref[rows, :] = rk * v
        if sample:
            wc_ref[...] = jnp.exp(lw)
        else:
            wc_ref[i] = jnp.exp(_dot01(sel_ref[...], lc, 3))
        e_neg = jnp.exp(-lc)
        at_ref[rows, :] = -kk * jnp.exp(lc - lw)
        bt_ref[rows, :] = kk * a * e_neg
        kt_ref[rows, :] = k2 * e_neg
        rt_ref[rows, :] = (r * jnp.exp(lc)).astype(rt_ref.dtype)

    halves = _half_rows(tm, sample)
    _interleave([half(i, rows) for i, rows in enumerate(halves)])
    if not sample:
        carry_ref[...] = tails[len(halves) - 1]

        @pl.when(t == nt - 1)
        def _():
            sh_ref[...] = carry_ref[CARRY_ROWS - 1:, :]


def _rwkv_core_kernel(at_ref, bt_ref, kt_ref, rt_ref, v_ref, wc_ref, y_ref, sout_ref, s_ref, *, nt, nchunk):
    t = pl.program_id(1)

    @pl.when(t == 0)
    def _():
        s_ref[...] = jnp.zeros_like(s_ref)

    r1 = 2 * CHUNK
    lane_lo = lax.broadcasted_iota(jnp.int32, (1, LANES), 1) < RWKV_HEAD
    ri = lax.broadcasted_iota(jnp.int32, (r1, r1), 0)
    ci = lax.broadcasted_iota(jnp.int32, (r1, r1), 1)
    shift = CHUNK.bit_length() - 1
    same = (ri >> shift) == (ci >> shift)
    m_strict = same & (ci < ri)
    m_incl = same & (ci <= ri)
    eye = (ri == ci).astype(F32)

    def stack(x):
        zero = jnp.zeros_like(x)
        return _cat0(jnp.where(lane_lo, x, zero), jnp.where(lane_lo, zero, x))

    pairs = range(RWKV_HEADS // 2)
    lanes = [slice(p * LANES, (p + 1) * LANES) for p in pairs]

    zero = jnp.zeros((r1, r1), BF16)
    halves = lambda w: w[:, :r1] + w[:, r1:]

    def rhs3(y):
        return _cat0(_cat1(y[0], y[1]), _cat1(y[0], zero))

    def rhs3_t(y):
        return _cat0(_cat1(y[0], y[0]), _cat1(y[1], zero))

    npair = len(lanes)
    units = range(nchunk * npair)
    tok = [slice((i // npair) * CHUNK, (i // npair + 1) * CHUNK) for i in units]
    ln = [lanes[i % npair] for i in units]
    a_s = [_split2(stack(at_ref[tok[i], ln[i]])) for i in units]
    a_k = [_cat1(*a_s[i]) for i in units]
    b_s = [_split2(stack(bt_ref[tok[i], ln[i]])) for i in units]
    k_s = [_split2(stack(kt_ref[tok[i], ln[i]])) for i in units]
    v_s = [_split2(stack(v_ref[tok[i], ln[i]])) for i in units]
    r_s = [stack(rt_ref[tok[i], ln[i]]) for i in units]
    bk_hi = [_cat0(b_s[i][0], k_s[i][0]) for i in units]
    a_ak = [_split2(jnp.where(m_strict, halves(_dot_nt(a_k[i], rhs3_t(k_s[i]))), 0.0)) for i in units]
    sc_r = [_dot_nt(r_s[i], bk_hi[i]) for i in units]
    a_r = [_cat1(jnp.where(m_incl, sc_r[i][:, :r1], 0.0).astype(BF16),
                 jnp.where(m_incl, sc_r[i][:, r1:], 0.0).astype(BF16)) for i in units]
    pw = [jnp.where(m_strict, halves(_dot_nt(a_k[i], rhs3_t(b_s[i]))), 0.0) for i in units]
    tinv = [eye + pw[i] for i in units]
    pws = [_split2(pw[i]) for i in units]
    pw = [halves(_bdot(_cat1(*pws[i]), rhs3(pws[i]))) for i in units]
    for step in range(5):
        pws = [_split2(pw[i]) for i in units]
        tis = [_split2(tinv[i]) for i in units]
        if step < 4:
            w = [_bdot(_cat0(_cat1(*pws[i]), _cat1(*tis[i])), rhs3(pws[i])) for i in units]
            pw = [halves(w[i][:r1]) for i in units]
            tinv = [tinv[i] + halves(w[i][r1:]) for i in units]
        else:
            tinv = [tinv[i] + halves(_bdot(_cat1(*tis[i]), rhs3(pws[i]))) for i in units]
    tis = [_cat1(*_split2(tinv[i])) for i in units]
    s = [s_ref[p] for p in pairs]
    for c in range(nchunk):
        un = [c * npair + p for p in pairs]
        ss = [_split2(s[p]) for p in pairs]
        x = [halves(_dot_nt(a_k[i], rhs3_t(ss[p]))) + halves(_bdot(_cat1(*a_ak[i]), rhs3(v_s[i])))
             for p, i in enumerate(un)]
        ub = [halves(_bdot(tis[i], rhs3(_split2(x[p])))).astype(BF16) for p, i in enumerate(un)]
        uv = [_cat0(ub[p], v_s[i][0]) for p, i in enumerate(un)]
        for p, i in enumerate(un):
            y = _dot_nt(r_s[i], ss[p][0]) + _bdot(a_r[i], uv[p])
            y_ref[tok[i], lanes[p]] = y[:CHUNK, :] + y[CHUNK:, :]
        s = [(s[p] + _dot_tn(uv[p], bk_hi[i])) * wc_ref[c][:, lanes[p]] for p, i in enumerate(un)]
    for p in pairs:
        s_ref[p] = s[p]

    @pl.when(t == nt - 1)
    def _():
        sout_ref[...] = s_ref[...]


def _rwkv_core_sample_kernel(at_ref, bt_ref, kt_ref, rt_ref, v_ref, wc_ref, s_ref, y_ref, sout_ref):
    n = RWKV_HEAD
    eye = (lax.broadcasted_iota(jnp.int32, (n, n), 0) == lax.broadcasted_iota(jnp.int32, (n, n), 1)).astype(F32)
    s = s_ref[...]
    u = jnp.sum(s * at_ref[...], axis=-1, keepdims=True)
    v_col = jnp.sum(eye * v_ref[...], axis=-1, keepdims=True)
    s1 = s + u * bt_ref[...] + v_col * kt_ref[...]
    y_col = jnp.sum(s1 * rt_ref[...], axis=-1, keepdims=True)
    y_ref[...] = jnp.sum(eye * y_col, axis=-2, keepdims=True)
    sout_ref[...] = s1 * wc_ref[...]


def _rwkv_post_kernel(y_ref, g_ref, bonus_ref, h_ref, mod_ref, ng_ref, lnw_ref, lnb_ref, wo_ref,
                      seg1_ref, seg2_ref, o_ref, *, sample, tm):
    mod = lambda i: _mod(mod_ref, i, sample)
    seg = lambda x: _segsum(x, seg1_ref[...], seg2_ref[...])

    def half(rows):
        y = y_ref[rows, :]
        mean = seg(y) * (1.0 / RWKV_HEAD)
        yield
        yc = y - mean
        var = seg(yc * yc) * (1.0 / RWKV_HEAD)
        yield
        yn = yc * lax.rsqrt(var + RWKV_GN_EPS) * lnw_ref[...] + lnb_ref[...] + bonus_ref[rows, :]
        out = _bdot((yn * g_ref[rows, :]).astype(BF16), wo_ref[...])
        yield
        o_ref[rows, :] = h_ref[rows, :] + mod(2) * _rms(out, ng_ref[1:2, :])

    _interleave([half(rows) for rows in _half_rows(tm, sample)])


def _rwkv_layer(h, mod, l, j, P, state_shift, state_wkv, *, sample):
    nb, rows, _ = h.shape
    tm = rows if sample else 2 * HALF
    nt = rows // tm
    names = ("rwkv_mu", "rwkv_w_rkv", "rwkv_w0", "rwkv_w1", "rwkv_w2", "rwkv_a0", "rwkv_a1", "rwkv_a2",
             "rwkv_g1", "rwkv_g2", "rwkv_k_k", "rwkv_k_a", "rwkv_r_k")
    consts = [(P["norm_g"], l)] + [(P[n], j) for n in names]
    seg1, seg2p = P["seg1"], P["seg2p"]
    in_specs = ([_rows(tm, D), _mod_spec(mod, l)] + [_layer(a, i) for a, i in consts]
                + [_whole(seg1.shape), _whole(seg2p.shape)])
    args = [h, mod] + [a for a, _ in consts] + [seg1, seg2p]
    if sample:
        in_specs.append(_per_seq(rows, D))
        args.append(state_shift)
        wc_shape, wc_spec = (nb, rows, D), _rows(tm, D)
        sh_shape, sh_spec = (nb, rows, D), _rows(tm, D)
        scratch = []
    else:
        ti = np.arange(HALF)
        tri = _np_bf16((ti[:, None] // CHUNK == ti[None, :] // CHUNK) & (ti[None, :] <= ti[:, None]))
        sel = _np_bf16(ti[None, :] == (np.arange(CARRY_ROWS)[:, None] * CHUNK + CHUNK - 1))
        in_specs += [_whole(tri.shape), _whole(sel.shape)]
        args += [tri, sel]
        wc_shape = (nb, rows // HALF, CARRY_ROWS, D)
        wc_spec = pl.BlockSpec((None, tm // HALF, CARRY_ROWS, D), lambda b, t: (b, t, 0, 0))
        sh_shape, sh_spec = (nb, 1, D), _per_seq(1, D)
        scratch = [pltpu.VMEM((CARRY_ROWS, D), F32)]
    f32 = jax.ShapeDtypeStruct(h.shape, F32)
    rt_shape = f32 if sample else jax.ShapeDtypeStruct(h.shape, BF16)
    at, bt, kt, rt, vv, g, bonus, wc, new_shift = pl.pallas_call(
        functools.partial(_rwkv_pre_kernel, sample=sample, tm=tm, nt=nt),
        grid=(nb, nt), in_specs=in_specs,
        out_specs=[_rows(tm, D)] * 7 + [wc_spec, sh_spec],
        out_shape=[f32, f32, f32, rt_shape, f32, f32, f32, jax.ShapeDtypeStruct(wc_shape, F32),
                   jax.ShapeDtypeStruct(sh_shape, F32)],
        scratch_shapes=scratch, compiler_params=_params(),
        name="rwkv_pre_sample" if sample else "rwkv_pre_prompt",
    )(*args)

    hh, n = RWKV_HEADS, RWKV_HEAD
    if sample:
        sb = 8
        nseq = rows
        hv = lambda x: x.reshape(nseq, hh, 1, n)
        vec_spec = pl.BlockSpec((sb, hh, 1, n), lambda i: (i, 0, 0, 0))
        st_spec = pl.BlockSpec((sb, hh, n, n), lambda i: (i, 0, 0, 0))
        y, new_s = pl.pallas_call(
            _rwkv_core_sample_kernel,
            grid=(nseq // sb,),
            in_specs=[vec_spec] * 6 + [st_spec],
            out_specs=[vec_spec, st_spec],
            out_shape=[jax.ShapeDtypeStruct((nseq, hh, 1, n), F32),
                       jax.ShapeDtypeStruct((nseq, hh, n, n), F32)],
            compiler_params=_params(1),
            name="rwkv_core_sample",
        )(hv(at), hv(bt), hv(kt), hv(rt), hv(vv), hv(wc), state_wkv)
        y = y.reshape(nb, rows, D)
    else:
        ncht = rows // CHUNK
        wc_rows = wc[:, :, :HALF // CHUNK, :].reshape(nb, ncht, 1, D)
        npair = hh // 2
        nc = RWKV_CHUNKS_PER_STEP
        y, s_bd = pl.pallas_call(
            functools.partial(_rwkv_core_kernel, nt=ncht // nc, nchunk=nc),
            grid=(nb, ncht // nc),
            in_specs=[_rows(nc * CHUNK, D)] * 5 + [pl.BlockSpec((None, nc, 1, D), lambda b, t: (b, t, 0, 0))],
            out_specs=[_rows(nc * CHUNK, D),
                       pl.BlockSpec((None, npair, LANES, LANES), lambda b, t: (b, 0, 0, 0))],
            out_shape=[f32, jax.ShapeDtypeStruct((nb, npair, LANES, LANES), F32)],
            scratch_shapes=[pltpu.VMEM((npair, LANES, LANES), F32)],
            compiler_params=_params(),
            name="rwkv_core_prompt",
        )(at, bt, kt, rt, vv, wc_rows)
        s5 = s_bd.reshape(nb, npair, 2, n, 2, n)
        new_s = jnp.stack([s5[:, :, 0, :, 0, :], s5[:, :, 1, :, 1, :]], axis=2).reshape(nb, hh, n, n)

    consts = [(P["norm_g"], l), (P["rwkv_lnx_w"], j), (P["rwkv_lnx_b"], j), (P["rwkv_w_o"], j)]
    h_new = pl.pallas_call(
        functools.partial(_rwkv_post_kernel, sample=sample, tm=tm),
        grid=(nb, nt),
        in_specs=([_rows(tm, D)] * 4 + [_mod_spec(mod, l)] + [_layer(a, i) for a, i in consts]
                  + [_whole(seg1.shape), _whole(seg2p.shape)]),
        out_specs=_rows(tm, D), out_shape=f32, compiler_params=_params(),
        name="rwkv_post_sample" if sample else "rwkv_post_prompt",
    )(y, g, bonus, h, mod, *[a for a, _ in consts], seg1, seg2p)
    return h_new, new_shift, new_s


GLA_LEVELS = (1, 2, 4, 8, 16, 32)


def _gla_pre_kernel(h_ref, mod_ref, ng_ref, win_ref, wa1_ref, wa2_ref, ba_ref,
                    q_ref, k_ref, v_ref, og_ref, lg_ref, *, sample, tm):
    mod = lambda i: _mod(mod_ref, i, sample)
    dk, dv = GLA_DK_TOTAL, GLA_DV_TOTAL

    def half(rows):
        u = (_rms(h_ref[rows, :], ng_ref[0:1, :]) * (1.0 + mod(1)) + mod(0)).astype(BF16)
        p = _bdot(u, win_ref[...])
        gate = _bdot(_bdot(u, wa1_ref[...]).astype(BF16), wa2_ref[...]) + ba_ref[...]
        yield
        q_ref[rows, :] = p[:, :dk] * (GLA_DK ** -0.5)
        k_ref[rows, :] = p[:, dk:2 * dk]
        v_ref[rows, :] = p[:, 2 * dk:2 * dk + dv]
        og_ref[rows, :] = p[:, 2 * dk + dv:]
        lg_ref[rows, :] = -_softplus(-gate) * (1.0 / GLA_GATE_NORMALIZER)

    _interleave([half(rows) for rows in _half_rows(tm, sample)])


def _gla_core_kernel(q_ref, k_ref, v_ref, lg_ref, lvl_ref, o_ref, sout_ref, s_ref, *, nt, nchunk):
    t = pl.program_id(1)

    @pl.when(t == 0)
    def _():
        s_ref[...] = jnp.zeros_like(s_ref)

    row = lax.broadcasted_iota(jnp.int32, (CHUNK, 1), 0)
    ri = lax.broadcasted_iota(jnp.int32, (CHUNK, CHUNK), 0)
    ci = lax.broadcasted_iota(jnp.int32, (CHUNK, CHUNK), 1)
    heads = range(GLA_HEADS)
    units = range(nchunk * GLA_HEADS)
    tok = [slice((i // GLA_HEADS) * CHUNK, (i // GLA_HEADS + 1) * CHUNK) for i in units]
    sl = [slice((i % GLA_HEADS) * GLA_DK, (i % GLA_HEADS + 1) * GLA_DK) for i in units]
    sv = [slice((i % GLA_HEADS) * GLA_DV, (i % GLA_HEADS + 1) * GLA_DV) for i in units]
    lg = [lg_ref[c * CHUNK:(c + 1) * CHUNK, :] for c in range(nchunk)]
    pq = [_bdot(lvl_ref[...], _cat0(*_split3(lg[c]))) for c in range(nchunk)]
    q = [q_ref[tok[i], sl[i]] for i in units]
    k = [k_ref[tok[i], sl[i]] for i in units]
    vb = [v_ref[tok[i], sv[i]].astype(BF16) for i in units]
    a = [jnp.where(ri == ci, _dot_nt(q[i].astype(BF16), k[i].astype(BF16)), 0.0) for i in units]
    for li, m in enumerate(GLA_LEVELS):
        lm = m.bit_length() - 1
        second = ((row >> lm) & 1) == 1
        block = (ri >> (lm + 1)) == (ci >> (lm + 1))
        base = (li - 1) * 2 * CHUNK
        for i in units:
            c = i // GLA_HEADS
            if m == 1:
                qe = jnp.where(second, q[i] * jnp.exp(lg[c][:, sl[i]]), 0.0)
                ke = jnp.where(second, 0.0, k[i])
            else:
                qe = jnp.where(second, q[i] * jnp.exp(pq[c][base:base + CHUNK, sl[i]]), 0.0)
                ke = jnp.where(second, 0.0, k[i] * jnp.exp(pq[c][base + CHUNK:base + 2 * CHUNK, sl[i]]))
            a[i] = a[i] + jnp.where(block, _dot_nt(qe.astype(BF16), ke.astype(BF16)), 0.0)
    base = (len(GLA_LEVELS) - 1) * 2 * CHUNK
    g_inc = [pq[i // GLA_HEADS][base:base + CHUNK, sl[i]] for i in units]
    g_rest = [pq[i // GLA_HEADS][base + CHUNK:base + 2 * CHUNK, sl[i]] for i in units]
    av = [_bdot(a[i].astype(BF16), vb[i]) for i in units]
    qg = [(q[i] * jnp.exp(g_inc[i])).astype(BF16) for i in units]
    kg = [(k[i] * jnp.exp(g_rest[i])).astype(BF16) for i in units]
    kv = [_dot_tn(vb[i], kg[i]) for i in units]
    s = [s_ref[hd] for hd in heads]
    for c in range(nchunk):
        for hd in heads:
            i = c * GLA_HEADS + hd
            o_ref[tok[i], sv[i]] = av[i] + _dot_nt(qg[i], s[hd].astype(BF16))
        s = [s[hd] * jnp.exp(g_inc[c * GLA_HEADS + hd][CHUNK - 1:CHUNK, :]) + kv[c * GLA_HEADS + hd]
             for hd in heads]
    for hd in heads:
        s_ref[hd] = s[hd]

    @pl.when(t == nt - 1)
    def _():
        sout_ref[...] = s_ref[...]


def _gla_core_sample_kernel(q_ref, k_ref, v_ref, lg_ref, s_ref, o_ref, sout_ref):
    n = GLA_DK
    eye = (lax.broadcasted_iota(jnp.int32, (n, n), 0) == lax.broadcasted_iota(jnp.int32, (n, n), 1)).astype(F32)
    col = lambda x: jnp.sum(eye * x, axis=-1, keepdims=True)
    q, k, v = q_ref[...], k_ref[...], v_ref[...]
    s = s_ref[...]
    decay = jnp.exp(lg_ref[...])
    qk = jnp.sum(q * k, axis=-1, keepdims=True)
    o_ref[...] = qk * v + jnp.sum(col(q * decay) * s, axis=-2, keepdims=True)
    sout_ref[...] = col(decay) * s + col(k) * v


def _gla_post_kernel(o_ref, og_ref, h_ref, mod_ref, ng_ref, on_ref, wo_ref, out_ref, *, sample):
    mod = lambda i: _mod(mod_ref, i, sample)
    og = og_ref[...]
    parts = []
    for hd in range(GLA_HEADS):
        sv = slice(hd * GLA_DV, (hd + 1) * GLA_DV)
        parts.append((_rms(o_ref[:, sv], on_ref[...]) * _silu(og[:, sv])).astype(BF16))
    out = _bdot(_cat1(*parts), wo_ref[...])
    out_ref[...] = h_ref[...] + mod(2) * _rms(out, ng_ref[1:2, :])


def _gla_level_matrix():
    ti = np.arange(CHUNK)
    blocks = []
    for m in GLA_LEVELS[1:] + (CHUNK,):
        same = ti[:, None] // m == ti[None, :] // m
        blocks.append(same & (ti[None, :] <= ti[:, None]))
        blocks.append(same & (ti[None, :] > ti[:, None]))
    one = np.concatenate(blocks, axis=0)
    return _np_bf16(np.concatenate([one, one, one], axis=1))


def _gla_layer(h, mod, l, j, P, state, *, sample):
    nb, rows, _ = h.shape
    tm = rows if sample else 2 * HALF
    nt = rows // tm
    dk, dv, hh = GLA_DK_TOTAL, GLA_DV_TOTAL, GLA_HEADS
    consts = [(P["norm_g"], l), (P["gla_w_in"], j), (P["gla_wa1"], j), (P["gla_wa2"], j), (P["gla_ba"], j)]
    shp = lambda width: jax.ShapeDtypeStruct((nb, rows, width), F32)
    q, k, v, og, lg = pl.pallas_call(
        functools.partial(_gla_pre_kernel, sample=sample, tm=tm),
        grid=(nb, nt),
        in_specs=[_rows(tm, D), _mod_spec(mod, l)] + [_layer(a, i) for a, i in consts],
        out_specs=[_rows(tm, dk), _rows(tm, dk), _rows(tm, dv), _rows(tm, dv), _rows(tm, dk)],
        out_shape=[shp(dk), shp(dk), shp(dv), shp(dv), shp(dk)],
        compiler_params=_params(),
        name="gla_pre_sample" if sample else "gla_pre_prompt",
    )(h, mod, *[a for a, _ in consts])

    if sample:
        sb = 8
        nseq = rows
        hk = lambda x: x.reshape(nseq, hh, 1, GLA_DK)
        kspec = pl.BlockSpec((sb, hh, 1, GLA_DK), lambda i: (i, 0, 0, 0))
        vspec = pl.BlockSpec((sb, hh, 1, GLA_DV), lambda i: (i, 0, 0, 0))
        sspec = pl.BlockSpec((sb, hh, GLA_DK, GLA_DV), lambda i: (i, 0, 0, 0))
        o, new_s = pl.pallas_call(
            _gla_core_sample_kernel,
            grid=(nseq // sb,),
            in_specs=[kspec, kspec, vspec, kspec, sspec],
            out_specs=[vspec, sspec],
            out_shape=[jax.ShapeDtypeStruct((nseq, hh, 1, GLA_DV), F32),
                       jax.ShapeDtypeStruct((nseq, hh, GLA_DK, GLA_DV), F32)],
            compiler_params=_params(1),
            name="gla_core_sample",
        )(hk(q), hk(k), v.reshape(nseq, hh, 1, GLA_DV), hk(lg), state)
        o = o.reshape(nb, rows, dv)
    else:
        ncht = rows // CHUNK
        nc = GLA_CHUNKS_PER_STEP
        tc = nc * CHUNK
        lvl = _gla_level_matrix()
        o, s_t = pl.pallas_call(
            functools.partial(_gla_core_kernel, nt=ncht // nc, nchunk=nc),
            grid=(nb, ncht // nc),
            in_specs=[_rows(tc, dk), _rows(tc, dk), _rows(tc, dv), _rows(tc, dk), _whole(lvl.shape)],
            out_specs=[_rows(tc, dv),
                       pl.BlockSpec((None, hh, GLA_DV, GLA_DK), lambda b, t: (b, 0, 0, 0))],
            out_shape=[shp(dv), jax.ShapeDtypeStruct((nb, hh, GLA_DV, GLA_DK), F32)],
            scratch_shapes=[pltpu.VMEM((hh, GLA_DV, GLA_DK), F32)],
            compiler_params=_params(),
            name="gla_core_prompt",
        )(q, k, v, lg, lvl)
        new_s = jnp.swapaxes(s_t, -1, -2)

    tp = rows if sample else HALF
    consts = [(P["norm_g"], l), (P["gla_onorm_g"], j), (P["gla_w_o"], j)]
    h_new = pl.pallas_call(
        functools.partial(_gla_post_kernel, sample=sample),
        grid=(nb, rows // tp),
        in_specs=[_rows(tp, dv), _rows(tp, dv), _rows(tp, D), _mod_spec(mod, l)] + [_layer(a, i) for a, i in consts],
        out_specs=_rows(tp, D), out_shape=shp(D), compiler_params=_params(),
        name="gla_post_sample" if sample else "gla_post_prompt",
    )(o, og, h, mod, *[a for a, _ in consts])
    return h_new, new_s


def _trunk(h, mod, P, states, *, sample):
    st_a, st_sb, st_wb, st_gc, st_f = states
    nb, rows, _ = h.shape
    new_a, new_sb, new_wb, new_gc, new_f = [], [], [], [], []
    for l in range(DEPTH):
        kind, j = l % 3, l // 3
        if kind == 0:
            st = st_a[j].reshape(1, rows, 2 * D) if sample else None
            h, nbuf = _sconv_layer(h, mod, l, j, P, st, sample=sample)
            new_a.append(nbuf.reshape(rows, 2, D) if sample else nbuf)
        elif kind == 1:
            sh = st_sb[j].reshape(1, rows, D) if sample else None
            wkv = st_wb[j] if sample else None
            h, nsh, ns = _rwkv_layer(h, mod, l, j, P, sh, wkv, sample=sample)
            new_sb.append(nsh.reshape(rows, D) if sample else nsh.reshape(nb, D))
            new_wb.append(ns)
        else:
            h, ns = _gla_layer(h, mod, l, j, P, st_gc[j] if sample else None, sample=sample)
            new_gc.append(ns)
        st = st_f[l].reshape(1, rows, 2 * D_FF) if sample else None
        h, nbuf = _ffn_layer(h, mod, l, P, st, sample=sample)
        new_f.append(nbuf.reshape(rows, 2, D_FF) if sample else nbuf)
    return (h, jnp.stack(new_a), jnp.stack(new_sb), jnp.stack(new_wb), jnp.stack(new_gc), jnp.stack(new_f))


def _prepare(p):
    bf = lambda x: x.astype(BF16)
    row = lambda x: x.reshape(x.shape[0], 1, -1)
    pad_c = lambda x: bf(jnp.pad(x, ((0, 0), (0, 0), (0, LANES - x.shape[2]))))
    pad_r = lambda x: bf(jnp.pad(x, ((0, 0), (0, LANES - x.shape[1]), (0, 0))))
    lane_head = np.arange(D) // RWKV_HEAD
    col = np.arange(LANES)
    P = dict(
        norm_g=p["norm_g"],
        sconv_w_in=bf(p["sconv_w_in"]), sconv_conv_w=p["sconv_conv_w"], sconv_w_out=bf(p["sconv_w_out"]),
        rwkv_mu=p["rwkv_mu"], rwkv_w_rkv=bf(p["rwkv_w_rkv"]), rwkv_w0=row(p["rwkv_w0"]),
        rwkv_w1=pad_c(p["rwkv_w1"]), rwkv_w2=pad_r(p["rwkv_w2"]), rwkv_a0=row(p["rwkv_a0"]),
        rwkv_a1=pad_c(p["rwkv_a1"]), rwkv_a2=pad_r(p["rwkv_a2"]), rwkv_g1=bf(p["rwkv_g1"]), rwkv_g2=bf(p["rwkv_g2"]),
        rwkv_k_k=row(p["rwkv_k_k"]), rwkv_k_a=row(p["rwkv_k_a"]), rwkv_r_k=row(p["rwkv_r_k"]),
        rwkv_lnx_w=row(p["rwkv_lnx_w"]), rwkv_lnx_b=row(p["rwkv_lnx_b"]), rwkv_w_o=bf(p["rwkv_w_o"]),
        gla_w_in=bf(p["gla_w_in"]), gla_wa1=pad_c(p["gla_wa1"]), gla_wa2=pad_r(p["gla_wa2"]),
        gla_ba=row(p["gla_ba"]), gla_onorm_g=row(p["gla_onorm_g"]), gla_w_o=bf(p["gla_w_o"]),
        ffn_w_up=bf(p["ffn_w_up"]), ffn_conv_w=p["ffn_conv_w"], ffn_conv_b=row(p["ffn_conv_b"]),
        ffn_w_down=bf(p["ffn_w_down"]),
        seg1=_np_bf16(lane_head[:, None] == col[None, :]),
        seg2p=_np_bf16((col[:, None] % SEG_STRIDE == lane_head[None, :]) & (col[:, None] < 3 * SEG_STRIDE)),
    )
    return P


def kernel(x_prompt, x_sample, state_conv_a, state_shift_b, state_wkv_b, state_gla_c, state_conv_ffn, c_prompt, c_sample, ada_w, ada_b, norm_g, sconv_w_in, sconv_conv_w, sconv_w_out, rwkv_mu, rwkv_w_rkv, rwkv_w0, rwkv_w1, rwkv_w2, rwkv_a0, rwkv_a1, rwkv_a2, rwkv_g1, rwkv_g2, rwkv_k_k, rwkv_k_a, rwkv_r_k, rwkv_lnx_w, rwkv_lnx_b, rwkv_w_o, gla_w_in, gla_wa1, gla_wa2, gla_ba, gla_onorm_g, gla_w_o, ffn_w_up, ffn_conv_w, ffn_conv_b, ffn_w_down):
    P = _prepare(dict(
        norm_g=norm_g, sconv_w_in=sconv_w_in, sconv_conv_w=sconv_conv_w, sconv_w_out=sconv_w_out,
        rwkv_mu=rwkv_mu, rwkv_w_rkv=rwkv_w_rkv, rwkv_w0=rwkv_w0, rwkv_w1=rwkv_w1, rwkv_w2=rwkv_w2,
        rwkv_a0=rwkv_a0, rwkv_a1=rwkv_a1, rwkv_a2=rwkv_a2, rwkv_g1=rwkv_g1, rwkv_g2=rwkv_g2,
        rwkv_k_k=rwkv_k_k, rwkv_k_a=rwkv_k_a, rwkv_r_k=rwkv_r_k, rwkv_lnx_w=rwkv_lnx_w,
        rwkv_lnx_b=rwkv_lnx_b, rwkv_w_o=rwkv_w_o,
        gla_w_in=gla_w_in, gla_wa1=gla_wa1, gla_wa2=gla_wa2, gla_ba=gla_ba,
        gla_onorm_g=gla_onorm_g, gla_w_o=gla_w_o,
        ffn_w_up=ffn_w_up, ffn_conv_w=ffn_conv_w, ffn_conv_b=ffn_conv_b, ffn_w_down=ffn_w_down))
    bp, bs = x_prompt.shape[0], x_sample.shape[0]
    mod = _modulation(jnp.concatenate([c_prompt, c_sample], axis=0), ada_w, ada_b)
    mod_p = mod[:, :bp].reshape(DEPTH, bp, N_MOD, D)
    mod_s = mod[:, bp:].reshape(DEPTH, 1, bs, N_MOD * D)
    y_p, ca_p, sb_p, wb_p, gc_p, cf_p = _trunk(x_prompt, mod_p, P, (None,) * 5, sample=False)
    y_s, ca_s, sb_s, wb_s, gc_s, cf_s = _trunk(
        x_sample.reshape(1, bs, D), mod_s, P,
        (state_conv_a, state_shift_b, state_wkv_b, state_gla_c, state_conv_ffn), sample=True)
    return (y_p, y_s.reshape(bs, 1, D), ca_p, ca_s, sb_p, sb_s, wb_p, wb_s, gc_p, gc_s, cf_p, cf_s)
```

```python
import functools

import jax
import jax.numpy as jnp
import numpy as np
from jax import lax
from jax.experimental import pallas as pl
from jax.experimental.pallas import tpu as pltpu

F32, BF16 = jnp.float32, jnp.bfloat16

D = 1024
DEPTH = 4
N_MOD = 6
RMS_EPS = 1e-6
RWKV_HEADS, RWKV_HEAD = 16, 64
RWKV_GN_EPS = 64e-5
GLA_HEADS, GLA_DK, GLA_DV = 4, 128, 256
GLA_DK_TOTAL, GLA_DV_TOTAL = 512, 1024
GLA_GATE_NORMALIZER = 16.0
D_FF = 2816

LANES = 128
CARRY_ROWS = 8
CHUNK = 64
HALF = 256
RWKV_CHUNKS_PER_STEP = 2
SEG_STRIDE = 16
VMEM_LIMIT = 56 * 1024 * 1024


def _rms(x, g):
    return x * lax.rsqrt(jnp.mean(x * x, axis=-1, keepdims=True) + RMS_EPS) * g


def _silu(x):
    return x * jax.nn.sigmoid(x)


def _softplus(x):
    return jnp.maximum(x, 0.0) + jnp.log1p(jnp.exp(-jnp.abs(x)))


_bdot = functools.partial(jnp.dot, preferred_element_type=F32)


def _dot_nt(a, b):
    return lax.dot_general(a, b, (((1,), (1,)), ((), ())), preferred_element_type=F32)


def _dot_tn(a, b):
    return lax.dot_general(a, b, (((0,), (0,)), ((), ())), preferred_element_type=F32)


def _cat0(*xs):
    return jnp.concatenate(xs, axis=0)


def _cat1(*xs):
    return jnp.concatenate(xs, axis=1)


def _split2(x):
    hi = x.astype(BF16)
    return hi, (x - hi.astype(F32)).astype(BF16)


def _split3(x):
    hi = x.astype(BF16)
    r1 = x - hi.astype(F32)
    mid = r1.astype(BF16)
    return hi, mid, (r1 - mid.astype(F32)).astype(BF16)


def _dot01(m01, x, terms):
    parts = (_split2 if terms == 2 else _split3)(x)
    acc = _bdot(m01, parts[0])
    for part in parts[1:]:
        acc = acc + _bdot(m01, part)
    return acc


def _segsum(x, seg1, seg2p):
    hi, lo = _split2(x)
    s = _bdot(hi, seg1) + _bdot(lo, seg1)
    s_hi, s_mid, s_lo = (part.astype(F32) for part in _split3(s))
    packed = s_hi + pltpu.roll(s_mid, SEG_STRIDE, axis=1) + pltpu.roll(s_lo, 2 * SEG_STRIDE, axis=1)
    return _bdot(packed.astype(BF16), seg2p)


def _mod(mod_ref, i, sample):
    return mod_ref[:, i * D:(i + 1) * D] if sample else mod_ref[i:i + 1, :]


def _shift_rows(x, k, carry):
    row = lax.broadcasted_iota(jnp.int32, (x.shape[0], 1), 0)
    y = pltpu.roll(x, k, axis=0)
    for j in range(k):
        src = CARRY_ROWS - k + j
        y = jnp.where(row == j, carry[src:src + 1, :], y)
    return y


def _half_rows(tm, sample):
    return [slice(0, tm)] if sample else [slice(0, tm // 2), slice(tm // 2, tm)]


def _interleave(gens):
    live = list(gens)
    while live:
        for g in list(live):
            if next(g, StopIteration) is StopIteration:
                live.remove(g)


def _whole(shape):
    nd = len(shape)
    return pl.BlockSpec(tuple(shape), lambda b, t: (0,) * nd, pipeline_mode=pl.Buffered(1))


def _layer(arr, l):
    nd = arr.ndim - 1
    return pl.BlockSpec((None,) + tuple(arr.shape[1:]), lambda b, t: (l,) + (0,) * nd,
                        pipeline_mode=pl.Buffered(1))


def _mod_spec(mod, l):
    return pl.BlockSpec((None, None) + tuple(mod.shape[2:]), lambda b, t: (l, b, 0, 0))


def _state_spec(state, l):
    return pl.BlockSpec((None,) + tuple(state.shape[1:]), lambda b, t: (l, 0, 0, 0))


def _new_state_spec(rows, width):
    return pl.BlockSpec((rows, 2, width), lambda b, t: (0, 0, 0))


def _rows(tm, width):
    return pl.BlockSpec((None, tm, width), lambda b, t: (b, t, 0))


def _per_seq(r, width):
    return pl.BlockSpec((None, r, width), lambda b, t: (b, 0, 0))


def _params(n=2):
    return pltpu.CompilerParams(dimension_semantics=("arbitrary",) * n, vmem_limit_bytes=VMEM_LIMIT)


def _np_bf16(a):
    return jnp.asarray(np.asarray(a, np.float32), BF16)


def _mod_kernel(cp_ref, cs_ref, w_ref, b_ref, op_ref, os_ref):
    w = w_ref[...].astype(BF16)
    op_ref[...] = _bdot(_silu(cp_ref[...]).astype(BF16), w) + b_ref[...]
    os_ref[...] = _bdot(_silu(cs_ref[...]).astype(BF16), w) + b_ref[...]


def _modulation(c_prompt, c_sample, ada_w, ada_b):
    bp, bs = c_prompt.shape[0], c_sample.shape[0]
    tn = 1536
    out = lambda n: (pl.BlockSpec((None, n, tn), lambda l, j: (l, 0, j)),
                     jax.ShapeDtypeStruct((DEPTH, n, N_MOD * D), F32))
    (sp_p, sh_p), (sp_s, sh_s) = out(bp), out(bs)
    return pl.pallas_call(
        _mod_kernel,
        grid=(DEPTH, N_MOD * D // tn),
        in_specs=[pl.BlockSpec((bp, D), lambda l, j: (0, 0)),
                  pl.BlockSpec((bs, D), lambda l, j: (0, 0)),
                  pl.BlockSpec((None, D, tn), lambda l, j: (l, 0, j)),
                  pl.BlockSpec((None, 1, tn), lambda l, j: (l, 0, j))],
        out_specs=[sp_p, sp_s], out_shape=[sh_p, sh_s],
        compiler_params=_params(),
        name="adaln_mod",
    )(c_prompt, c_sample, ada_w, ada_b.reshape(DEPTH, 1, N_MOD * D))


def _sconv_kernel(*refs, sample, tm, nt):
    if sample:
        h_ref, mod_ref, ng_ref, win_ref, cw_ref, wout_ref, st_ref, o_ref, nb_ref = refs
    else:
        h_ref, mod_ref, ng_ref, win_ref, cw_ref, wout_ref, o_ref, nb_ref, carry_ref = refs
        t = pl.program_id(1)

        @pl.when(t == 0)
        def _():
            carry_ref[...] = jnp.zeros_like(carry_ref)

    mod = lambda i: _mod(mod_ref, i, sample)
    cw = cw_ref[...]
    tails = {}

    def half(i, rows):
        n = rows.stop - rows.start
        h = h_ref[rows, :]
        u = (_rms(h, ng_ref[0:1, :]) * (1.0 + mod(1)) + mod(0)).astype(BF16)
        p = _bdot(u, win_ref[...])
        yield
        bg = p[:, :D]
        z = p[:, D:2 * D] * p[:, 2 * D:]
        if sample:
            y = st_ref[:, 0, :] * cw[0:1] + st_ref[:, 1, :] * cw[1:2] + z * cw[2:3]
            nb_ref[:, 0, :] = st_ref[:, 1, :]
            nb_ref[:, 1, :] = z
        else:
            c = carry_ref[...] if i == 0 else tails[i - 1]
            tails[i] = z[n - CARRY_ROWS:, :]
            y = _shift_rows(z, 2, c) * cw[0:1] + _shift_rows(z, 1, c) * cw[1:2] + z * cw[2:3]
        out = _bdot((bg * y).astype(BF16), wout_ref[...])
        yield
        o_ref[rows, :] = h + mod(2) * _rms(out, ng_ref[1:2, :])

    halves = _half_rows(tm, sample)
    _interleave([half(i, rows) for i, rows in enumerate(halves)])
    if not sample:
        carry_ref[...] = tails[len(halves) - 1]

        @pl.when(t == nt - 1)
        def _():
            nb_ref[...] = carry_ref[CARRY_ROWS - 2:, :]


def _sconv_layer(h, mod, l, j, P, state, *, sample):
    nb, rows, _ = h.shape
    tm = rows if sample else 2 * HALF
    nt = rows // tm
    consts = [(P["norm_g"], l), (P["sconv_w_in"], j), (P["sconv_conv_w"], j), (P["sconv_w_out"], j)]
    in_specs = [_rows(tm, D), _mod_spec(mod, l)] + [_layer(a, i) for a, i in consts]
    args = [h, mod] + [a for a, _ in consts]
    if sample:
        in_specs.append(_state_spec(state, j))
        args.append(state)
        nb_shape, nb_spec, scratch = (rows, 2, D), _new_state_spec(rows, D), []
    else:
        nb_shape, nb_spec = (nb, 2, D), _per_seq(2, D)
        scratch = [pltpu.VMEM((CARRY_ROWS, D), F32)]
    return pl.pallas_call(
        functools.partial(_sconv_kernel, sample=sample, tm=tm, nt=nt),
        grid=(nb, nt), in_specs=in_specs,
        out_specs=[_rows(tm, D), nb_spec],
        out_shape=[jax.ShapeDtypeStruct(h.shape, F32), jax.ShapeDtypeStruct(nb_shape, F32)],
        scratch_shapes=scratch, compiler_params=_params(),
        name="sconv_sample" if sample else "sconv_prompt",
    )(*args)


def _ffn_kernel(*refs, sample, tm, nt):
    if sample:
        h_ref, mod_ref, ng_ref, wup_ref, cw_ref, cb_ref, wdn_ref, st_ref, o_ref, nb_ref = refs
    else:
        h_ref, mod_ref, ng_ref, wup_ref, cw_ref, cb_ref, wdn_ref, o_ref, nb_ref, carry_ref = refs
    mod = lambda i: _mod(mod_ref, i, sample)
    cw = cw_ref[...]

    def pre(rows):
        h = h_ref[rows, :]
        return h, (_rms(h, ng_ref[2:3, :]) * (1.0 + mod(4)) + mod(3)).astype(BF16)

    def act(hc, val):
        return (_silu(hc + cb_ref[...]) * val).astype(BF16)

    def post(rows, h, out):
        o_ref[rows, :] = h + mod(5) * _rms(out, ng_ref[3:4, :])

    if sample:
        rows = slice(0, tm)
        h, u = pre(rows)
        g = _bdot(u, wup_ref[:, :D_FF])
        val = _bdot(u, wup_ref[:, D_FF:])
        hc = st_ref[:, 0, :] * cw[0:1] + st_ref[:, 1, :] * cw[1:2] + g * cw[2:3]
        nb_ref[:, 0, :] = st_ref[:, 1, :]
        nb_ref[:, 1, :] = g
        post(rows, h, _bdot(act(hc, val), wdn_ref[...]))
        return

    t = pl.program_id(1)

    @pl.when(t == 0)
    def _():
        carry_ref[...] = jnp.zeros_like(carry_ref)

    def conv(g, c):
        return _shift_rows(g, 2, c) * cw[0:1] + _shift_rows(g, 1, c) * cw[1:2] + g * cw[2:3]

    half = tm // 2
    rows_a, rows_b = slice(0, half), slice(half, tm)
    h_a, u_a = pre(rows_a)
    g_a = _bdot(u_a, wup_ref[:, :D_FF])
    h_b, u_b = pre(rows_b)
    v_a = _bdot(u_a, wup_ref[:, D_FF:])
    g_b = _bdot(u_b, wup_ref[:, :D_FF])
    act_a = act(conv(g_a, carry_ref[...]), v_a)
    v_b = _bdot(u_b, wup_ref[:, D_FF:])
    d_a = _bdot(act_a, wdn_ref[...])
    act_b = act(conv(g_b, g_a[half - CARRY_ROWS:, :]), v_b)
    carry_ref[...] = g_b[half - CARRY_ROWS:, :]
    d_b = _bdot(act_b, wdn_ref[...])
    post(rows_a, h_a, d_a)
    post(rows_b, h_b, d_b)

    @pl.when(t == nt - 1)
    def _():
        nb_ref[...] = carry_ref[CARRY_ROWS - 2:, :]


def _ffn_layer(h, mod, l, P, state, *, sample):
    nb, rows, _ = h.shape
    tm = rows if sample else 2 * HALF
    nt = rows // tm
    consts = [P["norm_g"], P["ffn_w_up"], P["ffn_conv_w"], P["ffn_conv_b"], P["ffn_w_down"]]
    in_specs = [_rows(tm, D), _mod_spec(mod, l)] + [_layer(a, l) for a in consts]
    args = [h, mod] + consts
    if sample:
        in_specs.append(_state_spec(state, l))
        args.append(state)
        nb_shape, nb_spec, scratch = (rows, 2, D_FF), _new_state_spec(rows, D_FF), []
    else:
        nb_shape, nb_spec = (nb, 2, D_FF), _per_seq(2, D_FF)
        scratch = [pltpu.VMEM((CARRY_ROWS, D_FF), F32)]
    return pl.pallas_call(
        functools.partial(_ffn_kernel, sample=sample, tm=tm, nt=nt),
        grid=(nb, nt), in_specs=in_specs,
        out_specs=[_rows(tm, D), nb_spec],
        out_shape=[jax.ShapeDtypeStruct(h.shape, F32), jax.ShapeDtypeStruct(nb_shape, F32)],
        scratch_shapes=scratch, compiler_params=_params(),
        name="ffn_sample" if sample else "ffn_prompt",
    )(*args)


def _rwkv_pre_kernel(*refs, sample, tm, nt):
    (h_ref, mod_ref, ng_ref, mu_ref, wrkv_ref, w0_ref, w1_ref, w2_ref, a0_ref, a1_ref, a2_ref,
     g1_ref, g2_ref, kk_ref, ka_ref, rk_ref, seg1_ref, seg2_ref) = refs[:18]
    if sample:
        shift_ref = refs[18]
        outs = refs[19:]
    else:
        tri_ref, sel_ref = refs[18:20]
        outs = refs[20:-1]
        carry_ref = refs[-1]
        t = pl.program_id(1)

        @pl.when(t == 0)
        def _():
            carry_ref[...] = jnp.zeros_like(carry_ref)

    at_ref, bt_ref, kt_ref, rt_ref, v_ref, g_ref, bonus_ref, wc_ref, sh_ref = outs
    mod = lambda i: _mod(mod_ref, i, sample)
    seg = lambda x: _segsum(x, seg1_ref[...], seg2_ref[...])
    tails = {}

    def half(i, rows):
        n = rows.stop - rows.start
        u = _rms(h_ref[rows, :], ng_ref[0:1, :]) * (1.0 + mod(1)) + mod(0)
        if sample:
            prev = shift_ref[...]
            sh_ref[...] = u
        else:
            tails[i] = u[n - CARRY_ROWS:, :]
            prev = _shift_rows(u, 1, carry_ref[...] if i == 0 else tails[i - 1])
        xx = prev - u
        mix = lambda m: (u + xx * mu_ref[m:m + 1, :]).astype(BF16)
        r = _bdot(mix(0), wrkv_ref[0])
        k = _bdot(mix(2), wrkv_ref[1])
        v = _bdot(mix(3), wrkv_ref[2])
        zw = _bdot(mix(1), w1_ref[...])
        za = _bdot(mix(4), a1_ref[...])
        zg = _bdot(mix(5), g1_ref[...])
        yield
        v_ref[rows, :] = v
        z = w0_ref[...] + _bdot(jnp.tanh(zw).astype(BF16), w2_ref[...])
        a = jax.nn.sigmoid(a0_ref[...] + _bdot(za.astype(BF16), a2_ref[...]))
        g_ref[rows, :] = _bdot(jax.nn.sigmoid(zg).astype(BF16), g2_ref[...])
        lw = -jnp.exp(-_softplus(-z) - 0.5)
        kk = k * kk_ref[...]
        k2 = k * (1.0 + (a - 1.0) * ka_ref[...])
        ss = seg(kk * kk)
        rk = seg(r * k2 * rk_ref[...])
        if sample:
            lc = lw
        else:
            lc = _dot01(tri_ref[...], lw, 2)
        yield
        kk = kk * jnp.minimum(lax.rsqrt(ss), 1e12)
        bonus_ref[rows, :] = rk * v
        if sample:
            wc_ref[...] = jnp.exp(lw)
        else:
            wc_ref[i] = jnp.exp(_dot01(sel_ref[...], lc, 3))
        e_neg = jnp.exp(-lc)
        at_ref[rows, :] = -kk * jnp.exp(lc - lw)
        bt_ref[rows, :] = kk * a * e_neg
        kt_ref[rows, :] = k2 * e_neg
        rt_ref[rows, :] = (r * jnp.exp(lc)).astype(rt_ref.dtype)

    halves = _half_rows(tm, sample)
    _interleave([half(i, rows) for i, rows in enumerate(halves)])
    if not sample:
        carry_ref[...] = tails[len(halves) - 1]

        @pl.when(t == nt - 1)
        def _():
            sh_ref[...] = carry_ref[CARRY_ROWS - 1:, :]


def _rwkv_core_kernel(at_ref, bt_ref, kt_ref, rt_ref, v_ref, wc_ref, y_ref, sout_ref, s_ref, *, nt, nchunk):
    t = pl.program_id(1)

    @pl.when(t == 0)
    def _():
        s_ref[...] = jnp.zeros_like(s_ref)

    r1 = 2 * CHUNK
    lane_lo = lax.broadcasted_iota(jnp.int32, (1, LANES), 1) < RWKV_HEAD
    ri = lax.broadcasted_iota(jnp.int32, (r1, r1), 0)
    ci = lax.broadcasted_iota(jnp.int32, (r1, r1), 1)
    shift = CHUNK.bit_length() - 1
    same = (ri >> shift) == (ci >> shift)
    m_strict = same & (ci < ri)
    m_incl = same & (ci <= ri)
    eye = (ri == ci).astype(F32)

    def stack(x):
        zero = jnp.zeros_like(x)
        return _cat0(jnp.where(lane_lo, x, zero), jnp.where(lane_lo, zero, x))

    pairs = range(RWKV_HEADS // 2)
    lanes = [slice(p * LANES, (p + 1) * LANES) for p in pairs]

    zero = jnp.zeros((r1, r1), BF16)
    halves = lambda w: w[:, :r1] + w[:, r1:]

    def rhs3(y):
        return _cat0(_cat1(y[0], y[1]), _cat1(y[0], zero))

    def rhs3_t(y):
        return _cat0(_cat1(y[0], y[0]), _cat1(y[1], zero))

    npair = len(lanes)
    units = range(nchunk * npair)
    tok = [slice((i // npair) * CHUNK, (i // npair + 1) * CHUNK) for i in units]
    ln = [lanes[i % npair] for i in units]
    a_s = [_split2(stack(at_ref[tok[i], ln[i]])) for i in units]
    a_k = [_cat1(*a_s[i]) for i in units]
    b_s = [_split2(stack(bt_ref[tok[i], ln[i]])) for i in units]
    k_s = [_split2(stack(kt_ref[tok[i], ln[i]])) for i in units]
    v_s = [_split2(stack(v_ref[tok[i], ln[i]])) for i in units]
    r_s = [stack(rt_ref[tok[i], ln[i]]) for i in units]
    bk_hi = [_cat0(b_s[i][0], k_s[i][0]) for i in units]
    a_ak = [_split2(jnp.where(m_strict, halves(_dot_nt(a_k[i], rhs3_t(k_s[i]))), 0.0)) for i in units]
    sc_r = [_dot_nt(r_s[i], bk_hi[i]) for i in units]
    a_r = [_cat1(jnp.where(m_incl, sc_r[i][:, :r1], 0.0).astype(BF16),
                 jnp.where(m_incl, sc_r[i][:, r1:], 0.0).astype(BF16)) for i in units]
    pw = [jnp.where(m_strict, halves(_dot_nt(a_k[i], rhs3_t(b_s[i]))), 0.0) for i in units]
    tinv = [eye + pw[i] for i in units]
    pws = [_split2(pw[i]) for i in units]
    pw = [halves(_bdot(_cat1(*pws[i]), rhs3(pws[i]))) for i in units]
    for step in range(5):
        pws = [_split2(pw[i]) for i in units]
        tis = [_split2(tinv[i]) for i in units]
        if step < 4:
            w = [_bdot(_cat0(_cat1(*pws[i]), _cat1(*tis[i])), rhs3(pws[i])) for i in units]
            pw = [halves(w[i][:r1]) for i in units]
            tinv = [tinv[i] + halves(w[i][r1:]) for i in units]
        else:
            tinv = [tinv[i] + halves(_bdot(_cat1(*tis[i]), rhs3(pws[i]))) for i in units]
    tis = [_cat1(*_split2(tinv[i])) for i in units]
    s = [s_ref[p] for p in pairs]
    for c in range(nchunk):
        un = [c * npair + p for p in pairs]
        ss = [_split2(s[p]) for p in pairs]
        x = [halves(_dot_nt(a_k[i], rhs3_t(ss[p]))) + halves(_bdot(_cat1(*a_ak[i]), rhs3(v_s[i])))
             for p, i in enumerate(un)]
        ub = [halves(_bdot(tis[i], rhs3(_split2(x[p])))).astype(BF16) for p, i in enumerate(un)]
        uv = [_cat0(ub[p], v_s[i][0]) for p, i in enumerate(un)]
        for p, i in enumerate(un):
            y = _dot_nt(r_s[i], ss[p][0]) + _bdot(a_r[i], uv[p])
            y_ref[tok[i], lanes[p]] = y[:CHUNK, :] + y[CHUNK:, :]
        s = [(s[p] + _dot_tn(uv[p], bk_hi[i])) * wc_ref[c][:, lanes[p]] for p, i in enumerate(un)]
    for p in pairs:
        s_ref[p] = s[p]

    @pl.when(t == nt - 1)
    def _():
        for p in pairs:
            sout_ref[2 * p] = s[p][:RWKV_HEAD, :RWKV_HEAD]
            sout_ref[2 * p + 1] = pltpu.roll(s[p], RWKV_HEAD, axis=1)[RWKV_HEAD:, :RWKV_HEAD]


def _rwkv_core_sample_kernel(at_ref, bt_ref, kt_ref, rt_ref, v_ref, wc_ref, s_ref, y_ref, sout_ref):
    n = RWKV_HEAD
    eye = (lax.broadcasted_iota(jnp.int32, (n, n), 0) == lax.broadcasted_iota(jnp.int32, (n, n), 1)).astype(F32)
    s = s_ref[...]
    u = jnp.sum(s * at_ref[...], axis=-1, keepdims=True)
    v_col = jnp.sum(eye * v_ref[...], axis=-1, keepdims=True)
    s1 = s + u * bt_ref[...] + v_col * kt_ref[...]
    y_col = jnp.sum(s1 * rt_ref[...], axis=-1, keepdims=True)
    y_ref[...] = jnp.sum(eye * y_col, axis=-2, keepdims=True)
    sout_ref[...] = s1 * wc_ref[...]


def _rwkv_post_kernel(y_ref, g_ref, bonus_ref, h_ref, mod_ref, ng_ref, lnw_ref, lnb_ref, wo_ref,
                      seg1_ref, seg2_ref, o_ref, *, sample, tm):
    mod = lambda i: _mod(mod_ref, i, sample)
    seg = lambda x: _segsum(x, seg1_ref[...], seg2_ref[...])

    def half(rows):
        y = y_ref[rows, :]
        mean = seg(y) * (1.0 / RWKV_HEAD)
        yield
        yc = y - mean
        var = seg(yc * yc) * (1.0 / RWKV_HEAD)
        yield
        yn = yc * lax.rsqrt(var + RWKV_GN_EPS) * lnw_ref[...] + lnb_ref[...] + bonus_ref[rows, :]
        out = _bdot((yn * g_ref[rows, :]).astype(BF16), wo_ref[...])
        yield
        o_ref[rows, :] = h_ref[rows, :] + mod(2) * _rms(out, ng_ref[1:2, :])

    _interleave([half(rows) for rows in _half_rows(tm, sample)])


def _rwkv_layer(h, mod, l, j, P, state_shift, state_wkv, *, sample):
    nb, rows, _ = h.shape
    tm = rows if sample else 2 * HALF
    nt = rows // tm
    names = ("rwkv_mu", "rwkv_w_rkv", "rwkv_w0", "rwkv_w1", "rwkv_w2", "rwkv_a0", "rwkv_a1", "rwkv_a2",
             "rwkv_g1", "rwkv_g2", "rwkv_k_k", "rwkv_k_a", "rwkv_r_k")
    consts = [(P["norm_g"], l)] + [(P[n], j) for n in names]
    seg1, seg2p = P["seg1"], P["seg2p"]
    in_specs = ([_rows(tm, D), _mod_spec(mod, l)] + [_layer(a, i) for a, i in consts]
                + [_whole(seg1.shape), _whole(seg2p.shape)])
    args = [h, mod] + [a for a, _ in consts] + [seg1, seg2p]
    if sample:
        in_specs.append(_per_seq(rows, D))
        args.append(state_shift)
        wc_shape, wc_spec = (nb, rows, D), _rows(tm, D)
        sh_shape, sh_spec = (nb, rows, D), _rows(tm, D)
        scratch = []
    else:
        ti = np.arange(HALF)
        tri = _np_bf16((ti[:, None] // CHUNK == ti[None, :] // CHUNK) & (ti[None, :] <= ti[:, None]))
        sel = _np_bf16(ti[None, :] == (np.arange(CARRY_ROWS)[:, None] * CHUNK + CHUNK - 1))
        in_specs += [_whole(tri.shape), _whole(sel.shape)]
        args += [tri, sel]
        wc_shape = (nb, rows // HALF, CARRY_ROWS, D)
        wc_spec = pl.BlockSpec((None, tm // HALF, CARRY_ROWS, D), lambda b, t: (b, t, 0, 0))
        sh_shape, sh_spec = (nb, 1, D), _per_seq(1, D)
        scratch = [pltpu.VMEM((CARRY_ROWS, D), F32)]
    f32 = jax.ShapeDtypeStruct(h.shape, F32)
    rt_shape = f32 if sample else jax.ShapeDtypeStruct(h.shape, BF16)
    at, bt, kt, rt, vv, g, bonus, wc, new_shift = pl.pallas_call(
        functools.partial(_rwkv_pre_kernel, sample=sample, tm=tm, nt=nt),
        grid=(nb, nt), in_specs=in_specs,
        out_specs=[_rows(tm, D)] * 7 + [wc_spec, sh_spec],
        out_shape=[f32, f32, f32, rt_shape, f32, f32, f32, jax.ShapeDtypeStruct(wc_shape, F32),
                   jax.ShapeDtypeStruct(sh_shape, F32)],
        scratch_shapes=scratch, compiler_params=_params(),
        name="rwkv_pre_sample" if sample else "rwkv_pre_prompt",
    )(*args)

    hh, n = RWKV_HEADS, RWKV_HEAD
    if sample:
        sb = 8
        nseq = rows
        hv = lambda x: x.reshape(nseq, hh, 1, n)
        vec_spec = pl.BlockSpec((sb, hh, 1, n), lambda i: (i, 0, 0, 0))
        st_spec = pl.BlockSpec((sb, hh, n, n), lambda i: (i, 0, 0, 0))
        y, new_s = pl.pallas_call(
            _rwkv_core_sample_kernel,
            grid=(nseq // sb,),
            in_specs=[vec_spec] * 6 + [st_spec],
            out_specs=[vec_spec, st_spec],
            out_shape=[jax.ShapeDtypeStruct((nseq, hh, 1, n), F32),
                       jax.ShapeDtypeStruct((nseq, hh, n, n), F32)],
            compiler_params=_params(1),
            name="rwkv_core_sample",
        )(hv(at), hv(bt), hv(kt), hv(rt), hv(vv), hv(wc), state_wkv)
        y = y.reshape(nb, rows, D)
    else:
        ncht = rows // CHUNK
        wc_rows = wc[:, :, :HALF // CHUNK, :].reshape(nb, ncht, 1, D)
        npair = hh // 2
        nc = RWKV_CHUNKS_PER_STEP
        y, new_s = pl.pallas_call(
            functools.partial(_rwkv_core_kernel, nt=ncht // nc, nchunk=nc),
            grid=(nb, ncht // nc),
            in_specs=[_rows(nc * CHUNK, D)] * 5 + [pl.BlockSpec((None, nc, 1, D), lambda b, t: (b, t, 0, 0))],
            out_specs=[_rows(nc * CHUNK, D), pl.BlockSpec((None, hh, n, n), lambda b, t: (b, 0, 0, 0))],
            out_shape=[f32, jax.ShapeDtypeStruct((nb, hh, n, n), F32)],
            scratch_shapes=[pltpu.VMEM((npair, LANES, LANES), F32)],
            compiler_params=_params(),
            name="rwkv_core_prompt",
        )(at, bt, kt, rt, vv, wc_rows)

    consts = [(P["norm_g"], l), (P["rwkv_lnx_w"], j), (P["rwkv_lnx_b"], j), (P["rwkv_w_o"], j)]
    h_new = pl.pallas_call(
        functools.partial(_rwkv_post_kernel, sample=sample, tm=tm),
        grid=(nb, nt),
        in_specs=([_rows(tm, D)] * 4 + [_mod_spec(mod, l)] + [_layer(a, i) for a, i in consts]
                  + [_whole(seg1.shape), _whole(seg2p.shape)]),
        out_specs=_rows(tm, D), out_shape=f32, compiler_params=_params(),
        name="rwkv_post_sample" if sample else "rwkv_post_prompt",
    )(y, g, bonus, h, mod, *[a for a, _ in consts], seg1, seg2p)
    return h_new, new_shift, new_s


GLA_LEVELS = (1, 2, 4, 8, 16, 32)


def _gla_pre_kernel(h_ref, mod_ref, ng_ref, win_ref, wa1_ref, wa2_ref, ba_ref,
                    q_ref, k_ref, v_ref, og_ref, lg_ref, *, sample, tm):
    mod = lambda i: _mod(mod_ref, i, sample)
    dk, dv = GLA_DK_TOTAL, GLA_DV_TOTAL

    def half(rows):
        u = (_rms(h_ref[rows, :], ng_ref[0:1, :]) * (1.0 + mod(1)) + mod(0)).astype(BF16)
        p = _bdot(u, win_ref[...])
        gate = _bdot(_bdot(u, wa1_ref[...]).astype(BF16), wa2_ref[...]) + ba_ref[...]
        yield
        q_ref[rows, :] = p[:, :dk] * (GLA_DK ** -0.5)
        k_ref[rows, :] = p[:, dk:2 * dk]
        v_ref[rows, :] = p[:, 2 * dk:2 * dk + dv]
        og_ref[rows, :] = p[:, 2 * dk + dv:]
        lg_ref[rows, :] = -_softplus(-gate) * (1.0 / GLA_GATE_NORMALIZER)

    _interleave([half(rows) for rows in _half_rows(tm, sample)])


def _gla_core_kernel(q_ref, k_ref, v_ref, lg_ref, lvl_ref, o_ref, sout_ref, s_ref, *, nt, nchunk):
    t = pl.program_id(1)

    @pl.when(t == 0)
    def _():
        s_ref[...] = jnp.zeros_like(s_ref)

    row = lax.broadcasted_iota(jnp.int32, (CHUNK, 1), 0)
    ri = lax.broadcasted_iota(jnp.int32, (CHUNK, CHUNK), 0)
    ci = lax.broadcasted_iota(jnp.int32, (CHUNK, CHUNK), 1)
    heads = range(GLA_HEADS)
    units = range(nchunk * GLA_HEADS)
    tok = [slice((i // GLA_HEADS) * CHUNK, (i // GLA_HEADS + 1) * CHUNK) for i in units]
    sl = [slice((i % GLA_HEADS) * GLA_DK, (i % GLA_HEADS + 1) * GLA_DK) for i in units]
    sv = [slice((i % GLA_HEADS) * GLA_DV, (i % GLA_HEADS + 1) * GLA_DV) for i in units]
    lg = [lg_ref[c * CHUNK:(c + 1) * CHUNK, :] for c in range(nchunk)]
    pq = [_bdot(lvl_ref[...], _cat0(*_split3(lg[c]))) for c in range(nchunk)]
    q = [q_ref[tok[i], sl[i]] for i in units]
    k = [k_ref[tok[i], sl[i]] for i in units]
    vb = [v_ref[tok[i], sv[i]].astype(BF16) for i in units]
    a = [jnp.where(ri == ci, _dot_nt(q[i].astype(BF16), k[i].astype(BF16)), 0.0) for i in units]
    for li, m in enumerate(GLA_LEVELS):
        lm = m.bit_length() - 1
        second = ((row >> lm) & 1) == 1
        block = (ri >> (lm + 1)) == (ci >> (lm + 1))
        base = (li - 1) * 2 * CHUNK
        for i in units:
            c = i // GLA_HEADS
            if m == 1:
                qe = jnp.where(second, q[i] * jnp.exp(lg[c][:, sl[i]]), 0.0)
                ke = jnp.where(second, 0.0, k[i])
            else:
                qe = jnp.where(second, q[i] * jnp.exp(pq[c][base:base + CHUNK, sl[i]]), 0.0)
                ke = jnp.where(second, 0.0, k[i] * jnp.exp(pq[c][base + CHUNK:base + 2 * CHUNK, sl[i]]))
            a[i] = a[i] + jnp.where(block, _dot_nt(qe.astype(BF16), ke.astype(BF16)), 0.0)
    base = (len(GLA_LEVELS) - 1) * 2 * CHUNK
    g_inc = [pq[i // GLA_HEADS][base:base + CHUNK, sl[i]] for i in units]
    g_rest = [pq[i // GLA_HEADS][base + CHUNK:base + 2 * CHUNK, sl[i]] for i in units]
    av = [_bdot(a[i].astype(BF16), vb[i]) for i in units]
    qg = [(q[i] * jnp.exp(g_inc[i])).astype(BF16) for i in units]
    kg = [(k[i] * jnp.exp(g_rest[i])).astype(BF16) for i in units]
    kv = [_dot_tn(vb[i], kg[i]) for i in units]
    s = [s_ref[hd] for hd in heads]
    for c in range(nchunk):
        for hd in heads:
            i = c * GLA_HEADS + hd
            o_ref[tok[i], sv[i]] = av[i] + _dot_nt(qg[i], s[hd].astype(BF16))
        s = [s[hd] * jnp.exp(g_inc[c * GLA_HEADS + hd][CHUNK - 1:CHUNK, :]) + kv[c * GLA_HEADS + hd]
             for hd in heads]
    for hd in heads:
        s_ref[hd] = s[hd]

    @pl.when(t == nt - 1)
    def _():
        sout_ref[...] = s_ref[...]


def _gla_core_sample_kernel(q_ref, k_ref, v_ref, lg_ref, s_ref, o_ref, sout_ref):
    n = GLA_DK
    eye = (lax.broadcasted_iota(jnp.int32, (n, n), 0) == lax.broadcasted_iota(jnp.int32, (n, n), 1)).astype(F32)
    col = lambda x: jnp.sum(eye * x, axis=-1, keepdims=True)
    q, k, v = q_ref[...], k_ref[...], v_ref[...]
    s = s_ref[...]
    decay = jnp.exp(lg_ref[...])
    qk = jnp.sum(q * k, axis=-1, keepdims=True)
    o_ref[...] = qk * v + jnp.sum(col(q * decay) * s, axis=-2, keepdims=True)
    sout_ref[...] = col(decay) * s + col(k) * v


def _gla_post_kernel(o_ref, og_ref, h_ref, mod_ref, ng_ref, on_ref, wo_ref, out_ref, *, sample):
    mod = lambda i: _mod(mod_ref, i, sample)
    og = og_ref[...]
    parts = []
    for hd in range(GLA_HEADS):
        sv = slice(hd * GLA_DV, (hd + 1) * GLA_DV)
        parts.append((_rms(o_ref[:, sv], on_ref[...]) * _silu(og[:, sv])).astype(BF16))
    out = _bdot(_cat1(*parts), wo_ref[...])
    out_ref[...] = h_ref[...] + mod(2) * _rms(out, ng_ref[1:2, :])


def _gla_level_matrix():
    ti = np.arange(CHUNK)
    blocks = []
    for m in GLA_LEVELS[1:] + (CHUNK,):
        same = ti[:, None] // m == ti[None, :] // m
        blocks.append(same & (ti[None, :] <= ti[:, None]))
        blocks.append(same & (ti[None, :] > ti[:, None]))
    one = np.concatenate(blocks, axis=0)
    return _np_bf16(np.concatenate([one, one, one], axis=1))


def _gla_fused_kernel(h_ref, mod_ref, ng_ref, win_ref, wa1_ref, wa2_ref, ba_ref, on_ref, wo_ref, lvl_ref,
                      out_ref, sout_ref, q_s, k_s, v_s, og_s, lg_s, o_s, s_ref, *, tm, nt):
    _gla_pre_kernel(h_ref, mod_ref, ng_ref, win_ref, wa1_ref, wa2_ref, ba_ref,
                    q_s, k_s, v_s, og_s, lg_s, sample=False, tm=tm)
    _gla_core_kernel(q_s, k_s, v_s, lg_s, lvl_ref, o_s, sout_ref, s_ref, nt=nt, nchunk=tm // CHUNK)
    _gla_post_kernel(o_s, og_s, h_ref, mod_ref, ng_ref, on_ref, wo_ref, out_ref, sample=False)


def _gla_prompt_layer(h, mod, l, j, P):
    nb, rows, _ = h.shape
    tm = 2 * HALF
    nt = rows // tm
    dk, dv, hh = GLA_DK_TOTAL, GLA_DV_TOTAL, GLA_HEADS
    lvl = _gla_level_matrix()
    consts = [(P["norm_g"], l), (P["gla_w_in"], j), (P["gla_wa1"], j), (P["gla_wa2"], j), (P["gla_ba"], j),
              (P["gla_onorm_g"], j), (P["gla_w_o"], j)]
    h_new, s_t = pl.pallas_call(
        functools.partial(_gla_fused_kernel, tm=tm, nt=nt),
        grid=(nb, nt),
        in_specs=[_rows(tm, D), _mod_spec(mod, l)] + [_layer(a, i) for a, i in consts] + [_whole(lvl.shape)],
        out_specs=[_rows(tm, D), pl.BlockSpec((None, hh, GLA_DV, GLA_DK), lambda b, t: (b, 0, 0, 0))],
        out_shape=[jax.ShapeDtypeStruct(h.shape, F32), jax.ShapeDtypeStruct((nb, hh, GLA_DV, GLA_DK), F32)],
        scratch_shapes=[pltpu.VMEM((tm, dk), F32), pltpu.VMEM((tm, dk), F32), pltpu.VMEM((tm, dv), F32),
                        pltpu.VMEM((tm, dv), F32), pltpu.VMEM((tm, dk), F32), pltpu.VMEM((tm, dv), F32),
                        pltpu.VMEM((hh, GLA_DV, GLA_DK), F32)],
        compiler_params=_params(),
        name="gla_prompt",
    )(h, mod, *[a for a, _ in consts], lvl)
    return h_new, jnp.swapaxes(s_t, -1, -2)


def _gla_layer(h, mod, l, j, P, state, *, sample):
    if not sample:
        return _gla_prompt_layer(h, mod, l, j, P)
    nb, rows, _ = h.shape
    tm = rows
    nt = rows // tm
    dk, dv, hh = GLA_DK_TOTAL, GLA_DV_TOTAL, GLA_HEADS
    consts = [(P["norm_g"], l), (P["gla_w_in"], j), (P["gla_wa1"], j), (P["gla_wa2"], j), (P["gla_ba"], j)]
    shp = lambda width: jax.ShapeDtypeStruct((nb, rows, width), F32)
    q, k, v, og, lg = pl.pallas_call(
        functools.partial(_gla_pre_kernel, sample=sample, tm=tm),
        grid=(nb, nt),
        in_specs=[_rows(tm, D), _mod_spec(mod, l)] + [_layer(a, i) for a, i in consts],
        out_specs=[_rows(tm, dk), _rows(tm, dk), _rows(tm, dv), _rows(tm, dv), _rows(tm, dk)],
        out_shape=[shp(dk), shp(dk), shp(dv), shp(dv), shp(dk)],
        compiler_params=_params(),
        name="gla_pre_sample" if sample else "gla_pre_prompt",
    )(h, mod, *[a for a, _ in consts])

    if sample:
        sb = 8
        nseq = rows
        hk = lambda x: x.reshape(nseq, hh, 1, GLA_DK)
        kspec = pl.BlockSpec((sb, hh, 1, GLA_DK), lambda i: (i, 0, 0, 0))
        vspec = pl.BlockSpec((sb, hh, 1, GLA_DV), lambda i: (i, 0, 0, 0))
        sspec = pl.BlockSpec((sb, hh, GLA_DK, GLA_DV), lambda i: (i, 0, 0, 0))
        o, new_s = pl.pallas_call(
            _gla_core_sample_kernel,
            grid=(nseq // sb,),
            in_specs=[kspec, kspec, vspec, kspec, sspec],
            out_specs=[vspec, sspec],
            out_shape=[jax.ShapeDtypeStruct((nseq, hh, 1, GLA_DV), F32),
                       jax.ShapeDtypeStruct((nseq, hh, GLA_DK, GLA_DV), F32)],
            compiler_params=_params(1),
            name="gla_core_sample",
        )(hk(q), hk(k), v.reshape(nseq, hh, 1, GLA_DV), hk(lg), state)
        o = o.reshape(nb, rows, dv)

    tp = rows
    consts = [(P["norm_g"], l), (P["gla_onorm_g"], j), (P["gla_w_o"], j)]
    h_new = pl.pallas_call(
        functools.partial(_gla_post_kernel, sample=sample),
        grid=(nb, rows // tp),
        in_specs=[_rows(tp, dv), _rows(tp, dv), _rows(tp, D), _mod_spec(mod, l)] + [_layer(a, i) for a, i in consts],
        out_specs=_rows(tp, D), out_shape=shp(D), compiler_params=_params(),
        name="gla_post_sample" if sample else "gla_post_prompt",
    )(o, og, h, mod, *[a for a, _ in consts])
    return h_new, new_s


def _trunk(h, mod, P, states, *, sample):
    st_a, st_sb, st_wb, st_gc, st_f = states
    nb, rows, _ = h.shape
    new_a, new_sb, new_wb, new_gc, new_f = [], [], [], [], []
    for l in range(DEPTH):
        kind, j = l % 3, l // 3
        if kind == 0:
            h, nbuf = _sconv_layer(h, mod, l, j, P, st_a, sample=sample)
            new_a.append(nbuf)
        elif kind == 1:
            sh = st_sb[j].reshape(1, rows, D) if sample else None
            wkv = st_wb[j] if sample else None
            h, nsh, ns = _rwkv_layer(h, mod, l, j, P, sh, wkv, sample=sample)
            new_sb.append(nsh.reshape(rows, D) if sample else nsh.reshape(nb, D))
            new_wb.append(ns)
        else:
            h, ns = _gla_layer(h, mod, l, j, P, st_gc[j] if sample else None, sample=sample)
            new_gc.append(ns)
        h, nbuf = _ffn_layer(h, mod, l, P, st_f, sample=sample)
        new_f.append(nbuf)
    return (h, jnp.stack(new_a), jnp.stack(new_sb), jnp.stack(new_wb), jnp.stack(new_gc), jnp.stack(new_f))


def _prepare(p):
    bf = lambda x: x.astype(BF16)
    row = lambda x: x.reshape(x.shape[0], 1, -1)
    pad_c = lambda x: bf(jnp.pad(x, ((0, 0), (0, 0), (0, LANES - x.shape[2]))))
    pad_r = lambda x: bf(jnp.pad(x, ((0, 0), (0, LANES - x.shape[1]), (0, 0))))
    lane_head = np.arange(D) // RWKV_HEAD
    col = np.arange(LANES)
    P = dict(
        norm_g=p["norm_g"],
        sconv_w_in=bf(p["sconv_w_in"]), sconv_conv_w=p["sconv_conv_w"], sconv_w_out=bf(p["sconv_w_out"]),
        rwkv_mu=p["rwkv_mu"], rwkv_w_rkv=bf(p["rwkv_w_rkv"]), rwkv_w0=row(p["rwkv_w0"]),
        rwkv_w1=pad_c(p["rwkv_w1"]), rwkv_w2=pad_r(p["rwkv_w2"]), rwkv_a0=row(p["rwkv_a0"]),
        rwkv_a1=pad_c(p["rwkv_a1"]), rwkv_a2=pad_r(p["rwkv_a2"]), rwkv_g1=bf(p["rwkv_g1"]), rwkv_g2=bf(p["rwkv_g2"]),
        rwkv_k_k=row(p["rwkv_k_k"]), rwkv_k_a=row(p["rwkv_k_a"]), rwkv_r_k=row(p["rwkv_r_k"]),
        rwkv_lnx_w=row(p["rwkv_lnx_w"]), rwkv_lnx_b=row(p["rwkv_lnx_b"]), rwkv_w_o=bf(p["rwkv_w_o"]),
        gla_w_in=bf(p["gla_w_in"]), gla_wa1=pad_c(p["gla_wa1"]), gla_wa2=pad_r(p["gla_wa2"]),
        gla_ba=row(p["gla_ba"]), gla_onorm_g=row(p["gla_onorm_g"]), gla_w_o=bf(p["gla_w_o"]),
        ffn_w_up=bf(p["ffn_w_up"]), ffn_conv_w=p["ffn_conv_w"], ffn_conv_b=row(p["ffn_conv_b"]),
        ffn_w_down=bf(p["ffn_w_down"]),
        seg1=_np_bf16(lane_head[:, None] == col[None, :]),
        seg2p=_np_bf16((col[:, None] % SEG_STRIDE == lane_head[None, :]) & (col[:, None] < 3 * SEG_STRIDE)),
    )
    return P


def kernel(x_prompt, x_sample, state_conv_a, state_shift_b, state_wkv_b, state_gla_c, state_conv_ffn, c_prompt, c_sample, ada_w, ada_b, norm_g, sconv_w_in, sconv_conv_w, sconv_w_out, rwkv_mu, rwkv_w_rkv, rwkv_w0, rwkv_w1, rwkv_w2, rwkv_a0, rwkv_a1, rwkv_a2, rwkv_g1, rwkv_g2, rwkv_k_k, rwkv_k_a, rwkv_r_k, rwkv_lnx_w, rwkv_lnx_b, rwkv_w_o, gla_w_in, gla_wa1, gla_wa2, gla_ba, gla_onorm_g, gla_w_o, ffn_w_up, ffn_conv_w, ffn_conv_b, ffn_w_down):
    P = _prepare(dict(
        norm_g=norm_g, sconv_w_in=sconv_w_in, sconv_conv_w=sconv_conv_w, sconv_w_out=sconv_w_out,
        rwkv_mu=rwkv_mu, rwkv_w_rkv=rwkv_w_rkv, rwkv_w0=rwkv_w0, rwkv_w1=rwkv_w1, rwkv_w2=rwkv_w2,
        rwkv_a0=rwkv_a0, rwkv_a1=rwkv_a1, rwkv_a2=rwkv_a2, rwkv_g1=rwkv_g1, rwkv_g2=rwkv_g2,
        rwkv_k_k=rwkv_k_k, rwkv_k_a=rwkv_k_a, rwkv_r_k=rwkv_r_k, rwkv_lnx_w=rwkv_lnx_w,
        rwkv_lnx_b=rwkv_lnx_b, rwkv_w_o=rwkv_w_o,
        gla_w_in=gla_w_in, gla_wa1=gla_wa1, gla_wa2=gla_wa2, gla_ba=gla_ba,
        gla_onorm_g=gla_onorm_g, gla_w_o=gla_w_o,
        ffn_w_up=ffn_w_up, ffn_conv_w=ffn_conv_w, ffn_conv_b=ffn_conv_b, ffn_w_down=ffn_w_down))
    bp, bs = x_prompt.shape[0], x_sample.shape[0]
    mod_p, mod_s = _modulation(c_prompt, c_sample, ada_w, ada_b)
    mod_p = mod_p.reshape(DEPTH, bp, N_MOD, D)
    mod_s = mod_s.reshape(DEPTH, 1, bs, N_MOD * D)
    y_p, ca_p, sb_p, wb_p, gc_p, cf_p = _trunk(x_prompt, mod_p, P, (None,) * 5, sample=False)
    y_s, ca_s, sb_s, wb_s, gc_s, cf_s = _trunk(
        x_sample.reshape(1, bs, D), mod_s, P,
        (state_conv_a, state_shift_b, state_wkv_b, state_gla_c, state_conv_ffn), sample=True)
    return (y_p, y_s.reshape(bs, 1, D), ca_p, ca_s, sb_p, sb_s, wb_p, wb_s, gc_p, gc_s, cf_p, cf_s)
```

```python
import functools

import jax
import jax.numpy as jnp
import numpy as np
from jax import lax
from jax.experimental import pallas as pl
from jax.experimental.pallas import tpu as pltpu

F32, BF16 = jnp.float32, jnp.bfloat16

D = 1024
DEPTH = 4
N_MOD = 6
RMS_EPS = 1e-6
RWKV_HEADS, RWKV_HEAD = 16, 64
RWKV_GN_EPS = 64e-5
GLA_HEADS, GLA_DK, GLA_DV = 4, 128, 256
GLA_DK_TOTAL, GLA_DV_TOTAL = 512, 1024
GLA_GATE_NORMALIZER = 16.0
D_FF = 2816

LANES = 128
CARRY_ROWS = 8
CHUNK = 64
HALF = 256
RWKV_CHUNKS_PER_STEP = 4
SEG_STRIDE = 16
VMEM_LIMIT = 56 * 1024 * 1024


def _rms(x, g):
    return x * lax.rsqrt(jnp.mean(x * x, axis=-1, keepdims=True) + RMS_EPS) * g


def _silu(x):
    return x * jax.nn.sigmoid(x)


def _softplus(x):
    return jnp.maximum(x, 0.0) + jnp.log1p(jnp.exp(-jnp.abs(x)))


_bdot = functools.partial(jnp.dot, preferred_element_type=F32)


def _dot_nt(a, b):
    return lax.dot_general(a, b, (((1,), (1,)), ((), ())), preferred_element_type=F32)


def _dot_tn(a, b):
    return lax.dot_general(a, b, (((0,), (0,)), ((), ())), preferred_element_type=F32)


def _cat0(*xs):
    return jnp.concatenate(xs, axis=0)


def _cat1(*xs):
    return jnp.concatenate(xs, axis=1)


def _split2(x):
    hi = x.astype(BF16)
    return hi, (x - hi.astype(F32)).astype(BF16)


def _split3(x):
    hi = x.astype(BF16)
    r1 = x - hi.astype(F32)
    mid = r1.astype(BF16)
    return hi, mid, (r1 - mid.astype(F32)).astype(BF16)


def _dot01(m01, x, terms):
    parts = (_split2 if terms == 2 else _split3)(x)
    acc = _bdot(m01, parts[0])
    for part in parts[1:]:
        acc = acc + _bdot(m01, part)
    return acc


def _segsum(x, seg1, seg2p):
    hi, lo = _split2(x)
    s = _bdot(hi, seg1) + _bdot(lo, seg1)
    s_hi, s_mid, s_lo = (part.astype(F32) for part in _split3(s))
    packed = s_hi + pltpu.roll(s_mid, SEG_STRIDE, axis=1) + pltpu.roll(s_lo, 2 * SEG_STRIDE, axis=1)
    return _bdot(packed.astype(BF16), seg2p)


def _mod(mod_ref, i, sample):
    return mod_ref[:, i * D:(i + 1) * D] if sample else mod_ref[i:i + 1, :]


def _shift_rows(x, k, carry):
    row = lax.broadcasted_iota(jnp.int32, (x.shape[0], 1), 0)
    y = pltpu.roll(x, k, axis=0)
    for j in range(k):
        src = CARRY_ROWS - k + j
        y = jnp.where(row == j, carry[src:src + 1, :], y)
    return y


def _half_rows(tm, sample):
    return [slice(0, tm)] if sample else [slice(0, tm // 2), slice(tm // 2, tm)]


def _interleave(gens):
    live = list(gens)
    while live:
        for g in list(live):
            if next(g, StopIteration) is StopIteration:
                live.remove(g)


def _whole(shape):
    nd = len(shape)
    return pl.BlockSpec(tuple(shape), lambda b, t: (0,) * nd, pipeline_mode=pl.Buffered(1))


def _layer(arr, l):
    nd = arr.ndim - 1
    return pl.BlockSpec((None,) + tuple(arr.shape[1:]), lambda b, t: (l,) + (0,) * nd,
                        pipeline_mode=pl.Buffered(1))


def _mod_spec(mod, l):
    return pl.BlockSpec((None, None) + tuple(mod.shape[2:]), lambda b, t: (l, b, 0, 0))


def _state_spec(state, l):
    return pl.BlockSpec((None,) + tuple(state.shape[1:]), lambda b, t: (l, 0, 0, 0))


def _new_state_spec(rows, width):
    return pl.BlockSpec((rows, 2, width), lambda b, t: (0, 0, 0))


def _rows(tm, width):
    return pl.BlockSpec((None, tm, width), lambda b, t: (b, t, 0))


def _per_seq(r, width):
    return pl.BlockSpec((None, r, width), lambda b, t: (b, 0, 0))


def _params(n=2):
    return pltpu.CompilerParams(dimension_semantics=("arbitrary",) * n, vmem_limit_bytes=VMEM_LIMIT)


def _np_bf16(a):
    return jnp.asarray(np.asarray(a, np.float32), BF16)


def _mod_kernel(cp_ref, cs_ref, w_ref, b_ref, op_ref, os_ref):
    w = w_ref[...].astype(BF16)
    op_ref[...] = _bdot(_silu(cp_ref[...]).astype(BF16), w) + b_ref[...]
    os_ref[...] = _bdot(_silu(cs_ref[...]).astype(BF16), w) + b_ref[...]


def _modulation(c_prompt, c_sample, ada_w, ada_b):
    bp, bs = c_prompt.shape[0], c_sample.shape[0]
    tn = 1536
    out = lambda n: (pl.BlockSpec((None, n, tn), lambda l, j: (l, 0, j)),
                     jax.ShapeDtypeStruct((DEPTH, n, N_MOD * D), F32))
    (sp_p, sh_p), (sp_s, sh_s) = out(bp), out(bs)
    return pl.pallas_call(
        _mod_kernel,
        grid=(DEPTH, N_MOD * D // tn),
        in_specs=[pl.BlockSpec((bp, D), lambda l, j: (0, 0)),
                  pl.BlockSpec((bs, D), lambda l, j: (0, 0)),
                  pl.BlockSpec((None, D, tn), lambda l, j: (l, 0, j)),
                  pl.BlockSpec((None, 1, tn), lambda l, j: (l, 0, j))],
        out_specs=[sp_p, sp_s], out_shape=[sh_p, sh_s],
        compiler_params=_params(),
        name="adaln_mod",
    )(c_prompt, c_sample, ada_w, ada_b.reshape(DEPTH, 1, N_MOD * D))


def _sconv_kernel(*refs, sample, tm, nt):
    if sample:
        h_ref, mod_ref, ng_ref, win_ref, cw_ref, wout_ref, st_ref, o_ref, nb_ref = refs
    else:
        h_ref, mod_ref, ng_ref, win_ref, cw_ref, wout_ref, o_ref, nb_ref, carry_ref = refs
        t = pl.program_id(1)

        @pl.when(t == 0)
        def _():
            carry_ref[...] = jnp.zeros_like(carry_ref)

    mod = lambda i: _mod(mod_ref, i, sample)
    cw = cw_ref[...]
    tails = {}

    def half(i, rows):
        n = rows.stop - rows.start
        h = h_ref[rows, :]
        u = (_rms(h, ng_ref[0:1, :]) * (1.0 + mod(1)) + mod(0)).astype(BF16)
        p = _bdot(u, win_ref[...])
        yield
        bg = p[:, :D]
        z = p[:, D:2 * D] * p[:, 2 * D:]
        if sample:
            y = st_ref[:, 0, :] * cw[0:1] + st_ref[:, 1, :] * cw[1:2] + z * cw[2:3]
            nb_ref[:, 0, :] = st_ref[:, 1, :]
            nb_ref[:, 1, :] = z
        else:
            c = carry_ref[...] if i == 0 else tails[i - 1]
            tails[i] = z[n - CARRY_ROWS:, :]
            y = _shift_rows(z, 2, c) * cw[0:1] + _shift_rows(z, 1, c) * cw[1:2] + z * cw[2:3]
        out = _bdot((bg * y).astype(BF16), wout_ref[...])
        yield
        o_ref[rows, :] = h + mod(2) * _rms(out, ng_ref[1:2, :])

    halves = _half_rows(tm, sample)
    _interleave([half(i, rows) for i, rows in enumerate(halves)])
    if not sample:
        carry_ref[...] = tails[len(halves) - 1]

        @pl.when(t == nt - 1)
        def _():
            nb_ref[...] = carry_ref[CARRY_ROWS - 2:, :]


def _sconv_layer(h, mod, l, j, P, state, *, sample):
    nb, rows, _ = h.shape
    tm = rows if sample else 2 * HALF
    nt = rows // tm
    consts = [(P["norm_g"], l), (P["sconv_w_in"], j), (P["sconv_conv_w"], j), (P["sconv_w_out"], j)]
    in_specs = [_rows(tm, D), _mod_spec(mod, l)] + [_layer(a, i) for a, i in consts]
    args = [h, mod] + [a for a, _ in consts]
    if sample:
        in_specs.append(_state_spec(state, j))
        args.append(state)
        nb_shape, nb_spec, scratch = (rows, 2, D), _new_state_spec(rows, D), []
    else:
        nb_shape, nb_spec = (nb, 2, D), _per_seq(2, D)
        scratch = [pltpu.VMEM((CARRY_ROWS, D), F32)]
    return pl.pallas_call(
        functools.partial(_sconv_kernel, sample=sample, tm=tm, nt=nt),
        grid=(nb, nt), in_specs=in_specs,
        out_specs=[_rows(tm, D), nb_spec],
        out_shape=[jax.ShapeDtypeStruct(h.shape, F32), jax.ShapeDtypeStruct(nb_shape, F32)],
        scratch_shapes=scratch, compiler_params=_params(),
        name="sconv_sample" if sample else "sconv_prompt",
    )(*args)


def _ffn_kernel(*refs, sample, tm, nt):
    if sample:
        h_ref, mod_ref, ng_ref, wup_ref, cw_ref, cb_ref, wdn_ref, st_ref, o_ref, nb_ref = refs
    else:
        h_ref, mod_ref, ng_ref, wup_ref, cw_ref, cb_ref, wdn_ref, o_ref, nb_ref, carry_ref = refs
    mod = lambda i: _mod(mod_ref, i, sample)
    cw = cw_ref[...]

    def pre(rows):
        h = h_ref[rows, :]
        return h, (_rms(h, ng_ref[2:3, :]) * (1.0 + mod(4)) + mod(3)).astype(BF16)

    def act(hc, val):
        return (_silu(hc + cb_ref[...]) * val).astype(BF16)

    def post(rows, h, out):
        o_ref[rows, :] = h + mod(5) * _rms(out, ng_ref[3:4, :])

    if sample:
        rows = slice(0, tm)
        h, u = pre(rows)
        g = _bdot(u, wup_ref[:, :D_FF])
        val = _bdot(u, wup_ref[:, D_FF:])
        hc = st_ref[:, 0, :] * cw[0:1] + st_ref[:, 1, :] * cw[1:2] + g * cw[2:3]
        nb_ref[:, 0, :] = st_ref[:, 1, :]
        nb_ref[:, 1, :] = g
        post(rows, h, _bdot(act(hc, val), wdn_ref[...]))
        return

    t = pl.program_id(1)

    @pl.when(t == 0)
    def _():
        carry_ref[...] = jnp.zeros_like(carry_ref)

    def conv(g, c):
        return _shift_rows(g, 2, c) * cw[0:1] + _shift_rows(g, 1, c) * cw[1:2] + g * cw[2:3]

    half = tm // 2
    rows_a, rows_b = slice(0, half), slice(half, tm)
    h_a, u_a = pre(rows_a)
    g_a = _bdot(u_a, wup_ref[:, :D_FF])
    h_b, u_b = pre(rows_b)
    v_a = _bdot(u_a, wup_ref[:, D_FF:])
    g_b = _bdot(u_b, wup_ref[:, :D_FF])
    act_a = act(conv(g_a, carry_ref[...]), v_a)
    v_b = _bdot(u_b, wup_ref[:, D_FF:])
    d_a = _bdot(act_a, wdn_ref[...])
    act_b = act(conv(g_b, g_a[half - CARRY_ROWS:, :]), v_b)
    carry_ref[...] = g_b[half - CARRY_ROWS:, :]
    d_b = _bdot(act_b, wdn_ref[...])
    post(rows_a, h_a, d_a)
    post(rows_b, h_b, d_b)

    @pl.when(t == nt - 1)
    def _():
        nb_ref[...] = carry_ref[CARRY_ROWS - 2:, :]


def _ffn_layer(h, mod, l, P, state, *, sample):
    nb, rows, _ = h.shape
    tm = rows if sample else 2 * HALF
    nt = rows // tm
    consts = [P["norm_g"], P["ffn_w_up"], P["ffn_conv_w"], P["ffn_conv_b"], P["ffn_w_down"]]
    in_specs = [_rows(tm, D), _mod_spec(mod, l)] + [_layer(a, l) for a in consts]
    args = [h, mod] + consts
    if sample:
        in_specs.append(_state_spec(state, l))
        args.append(state)
        nb_shape, nb_spec, scratch = (rows, 2, D_FF), _new_state_spec(rows, D_FF), []
    else:
        nb_shape, nb_spec = (nb, 2, D_FF), _per_seq(2, D_FF)
        scratch = [pltpu.VMEM((CARRY_ROWS, D_FF), F32)]
    return pl.pallas_call(
        functools.partial(_ffn_kernel, sample=sample, tm=tm, nt=nt),
        grid=(nb, nt), in_specs=in_specs,
        out_specs=[_rows(tm, D), nb_spec],
        out_shape=[jax.ShapeDtypeStruct(h.shape, F32), jax.ShapeDtypeStruct(nb_shape, F32)],
        scratch_shapes=scratch, compiler_params=_params(),
        name="ffn_sample" if sample else "ffn_prompt",
    )(*args)


def _rwkv_pre_kernel(*refs, sample, tm, nt):
    (h_ref, mod_ref, ng_ref, mu_ref, wrkv_ref, w0_ref, w1_ref, w2_ref, a0_ref, a1_ref, a2_ref,
     g1_ref, g2_ref, kk_ref, ka_ref, rk_ref, seg1_ref, seg2_ref) = refs[:18]
    if sample:
        shift_ref = refs[18]
        outs = refs[19:]
    else:
        tri_ref, sel_ref = refs[18:20]
        outs = refs[20:-1]
        carry_ref = refs[-1]
        t = pl.program_id(1)

        @pl.when(t == 0)
        def _():
            carry_ref[...] = jnp.zeros_like(carry_ref)

    at_ref, bt_ref, kt_ref, rt_ref, v_ref, g_ref, bonus_ref, wc_ref, sh_ref = outs
    mod = lambda i: _mod(mod_ref, i, sample)
    seg = lambda x: _segsum(x, seg1_ref[...], seg2_ref[...])
    tails = {}

    def half(i, rows):
        n = rows.stop - rows.start
        u = _rms(h_ref[rows, :], ng_ref[0:1, :]) * (1.0 + mod(1)) + mod(0)
        if sample:
            prev = shift_ref[...]
            sh_ref[...] = u
        else:
            tails[i] = u[n - CARRY_ROWS:, :]
            prev = _shift_rows(u, 1, carry_ref[...] if i == 0 else tails[i - 1])
        xx = prev - u
        mix = lambda m: (u + xx * mu_ref[m:m + 1, :]).astype(BF16)
        r = _bdot(mix(0), wrkv_ref[0])
        k = _bdot(mix(2), wrkv_ref[1])
        v = _bdot(mix(3), wrkv_ref[2])
        zw = _bdot(mix(1), w1_ref[...])
        za = _bdot(mix(4), a1_ref[...])
        zg = _bdot(mix(5), g1_ref[...])
        yield
        v_ref[rows, :] = v
        z = w0_ref[...] + _bdot(jnp.tanh(zw).astype(BF16), w2_ref[...])
        a = jax.nn.sigmoid(a0_ref[...] + _bdot(za.astype(BF16), a2_ref[...]))
        g_ref[rows, :] = _bdot(jax.nn.sigmoid(zg).astype(BF16), g2_ref[...])
        lw = -jnp.exp(-_softplus(-z) - 0.5)
        kk = k * kk_ref[...]
        k2 = k * (1.0 + (a - 1.0) * ka_ref[...])
        ss = seg(kk * kk)
        rk = seg(r * k2 * rk_ref[...])
        if sample:
            lc = lw
        else:
            lc = _dot01(tri_ref[...], lw, 2)
        yield
        kk = kk * jnp.minimum(lax.rsqrt(ss), 1e12)
        bonus_ref[rows, :] = rk * v
        if sample:
            wc_ref[...] = jnp.exp(lw)
        else:
            wc_ref[i] = jnp.exp(_dot01(sel_ref[...], lc, 3))
        e_neg = jnp.exp(-lc)
        at_ref[rows, :] = -kk * jnp.exp(lc - lw)
        bt_ref[rows, :] = kk * a * e_neg
        kt_ref[rows, :] = k2 * e_neg
        rt_ref[rows, :] = (r * jnp.exp(lc)).astype(rt_ref.dtype)

    halves = _half_rows(tm, sample)
    _interleave([half(i, rows) for i, rows in enumerate(halves)])
    if not sample:
        carry_ref[...] = tails[len(halves) - 1]

        @pl.when(t == nt - 1)
        def _():
            sh_ref[...] = carry_ref[CARRY_ROWS - 1:, :]


def _rwkv_core_kernel(at_ref, bt_ref, kt_ref, rt_ref, v_ref, wc_ref, y_ref, sout_ref, s_ref, *, nt, nchunk):
    t = pl.program_id(1)

    @pl.when(t == 0)
    def _():
        s_ref[...] = jnp.zeros_like(s_ref)

    n = RWKV_HEAD
    lane_lo = lax.broadcasted_iota(jnp.int32, (1, LANES), 1) < n
    ri = lax.broadcasted_iota(jnp.int32, (CHUNK, LANES), 0)
    ci = lax.broadcasted_iota(jnp.int32, (CHUNK, LANES), 1) & (n - 1)
    m_strict = ci < ri
    m_incl = ci <= ri
    eye = (ci == ri).astype(F32)
    shift = n.bit_length() - 1
    m_bd = ((lax.broadcasted_iota(jnp.int32, (LANES, LANES), 0) >> shift)
            == (lax.broadcasted_iota(jnp.int32, (LANES, LANES), 1) >> shift))

    def bd(x):
        z = jnp.zeros_like(x)
        return _cat0(jnp.where(lane_lo, x, z), jnp.where(lane_lo, z, x))

    bd2 = lambda xs: (bd(xs[0]), bd(xs[1]))

    zero = jnp.zeros((LANES, LANES), BF16)
    halves = lambda w: w[:, :LANES] + w[:, LANES:]

    def rhs3(y):
        return _cat0(_cat1(y[0], y[1]), _cat1(y[0], zero))

    def rhs3_t(y):
        return _cat0(_cat1(y[0], y[0]), _cat1(y[1], zero))

    pairs = range(RWKV_HEADS // 2)
    lanes = [slice(p * LANES, (p + 1) * LANES) for p in pairs]
    npair = len(lanes)
    units = range(nchunk * npair)
    tok = [slice((i // npair) * CHUNK, (i // npair + 1) * CHUNK) for i in units]
    ln = [lanes[i % npair] for i in units]
    a_k = [_cat1(*_split2(at_ref[tok[i], ln[i]])) for i in units]
    b_s = [_split2(bt_ref[tok[i], ln[i]]) for i in units]
    k_s = [_split2(kt_ref[tok[i], ln[i]]) for i in units]
    v_s = [_split2(v_ref[tok[i], ln[i]]) for i in units]
    b_bd = [bd2(b_s[i]) for i in units]
    k_bd = [bd2(k_s[i]) for i in units]
    v_bd = [bd2(v_s[i]) for i in units]
    r_s = [rt_ref[tok[i], ln[i]] for i in units]
    bk_hi = [_cat0(b_s[i][0], k_s[i][0]) for i in units]
    a_ak = [_split2(jnp.where(m_strict, halves(_dot_nt(a_k[i], rhs3_t(k_bd[i]))), 0.0)) for i in units]
    sc_r = [_dot_nt(r_s[i], _cat0(b_bd[i][0], k_bd[i][0])) for i in units]
    a_r = [_cat1(jnp.where(m_incl, sc_r[i][:, :LANES], 0.0).astype(BF16),
                 jnp.where(m_incl, sc_r[i][:, LANES:], 0.0).astype(BF16)) for i in units]
    pw = [jnp.where(m_strict, halves(_dot_nt(a_k[i], rhs3_t(b_bd[i]))), 0.0) for i in units]
    tinv = [eye + pw[i] for i in units]
    pws = [_split2(pw[i]) for i in units]
    pw = [halves(_bdot(_cat1(*pws[i]), rhs3(bd2(pws[i])))) for i in units]
    for step in range(5):
        pws = [_split2(pw[i]) for i in units]
        tis = [_split2(tinv[i]) for i in units]
        rhs = [rhs3(bd2(pws[i])) for i in units]
        if step < 4:
            w = [_bdot(_cat0(_cat1(*pws[i]), _cat1(*tis[i])), rhs[i]) for i in units]
            pw = [halves(w[i][:CHUNK]) for i in units]
            tinv = [tinv[i] + halves(w[i][CHUNK:]) for i in units]
        else:
            tinv = [tinv[i] + halves(_bdot(_cat1(*tis[i]), rhs[i])) for i in units]
    tis = [_cat1(*_split2(tinv[i])) for i in units]
    s = [s_ref[p] for p in pairs]
    for c in range(nchunk):
        un = [c * npair + p for p in pairs]
        ss = [_split2(s[p]) for p in pairs]
        x = [halves(_dot_nt(a_k[i], rhs3_t(ss[p]))) + halves(_bdot(_cat1(*a_ak[i]), rhs3(v_bd[i])))
             for p, i in enumerate(un)]
        ub = [halves(_bdot(tis[i], rhs3(bd2(_split2(x[p]))))).astype(BF16) for p, i in enumerate(un)]
        for p, i in enumerate(un):
            y_ref[tok[i], lanes[p]] = (_dot_nt(r_s[i], ss[p][0])
                                       + _bdot(a_r[i], _cat0(bd(ub[p]), v_bd[i][0])))
        s = [jnp.where(m_bd, s[p] + _dot_tn(_cat0(ub[p], v_s[i][0]), bk_hi[i]), 0.0) * wc_ref[c][:, lanes[p]]
             for p, i in enumerate(un)]
    for p in pairs:
        s_ref[p] = s[p]

    @pl.when(t == nt - 1)
    def _():
        for p in pairs:
            sout_ref[2 * p] = s[p][:n, :n]
            sout_ref[2 * p + 1] = pltpu.roll(s[p], n, axis=1)[n:, :n]


def _rwkv_core_sample_kernel(at_ref, bt_ref, kt_ref, rt_ref, v_ref, wc_ref, s_ref, y_ref, sout_ref):
    n = RWKV_HEAD
    eye = (lax.broadcasted_iota(jnp.int32, (n, n), 0) == lax.broadcasted_iota(jnp.int32, (n, n), 1)).astype(F32)
    s = s_ref[...]
    u = jnp.sum(s * at_ref[...], axis=-1, keepdims=True)
    v_col = jnp.sum(eye * v_ref[...], axis=-1, keepdims=True)
    s1 = s + u * bt_ref[...] + v_col * kt_ref[...]
    y_col = jnp.sum(s1 * rt_ref[...], axis=-1, keepdims=True)
    y_ref[...] = jnp.sum(eye * y_col, axis=-2, keepdims=True)
    sout_ref[...] = s1 * wc_ref[...]


def _rwkv_post_kernel(y_ref, g_ref, bonus_ref, h_ref, mod_ref, ng_ref, lnw_ref, lnb_ref, wo_ref,
                      seg1_ref, seg2_ref, o_ref, *, sample, tm):
    mod = lambda i: _mod(mod_ref, i, sample)
    seg = lambda x: _segsum(x, seg1_ref[...], seg2_ref[...])

    def half(rows):
        y = y_ref[rows, :]
        mean = seg(y) * (1.0 / RWKV_HEAD)
        yield
        yc = y - mean
        var = seg(yc * yc) * (1.0 / RWKV_HEAD)
        yield
        yn = yc * lax.rsqrt(var + RWKV_GN_EPS) * lnw_ref[...] + lnb_ref[...] + bonus_ref[rows, :]
        out = _bdot((yn * g_ref[rows, :]).astype(BF16), wo_ref[...])
        yield
        o_ref[rows, :] = h_ref[rows, :] + mod(2) * _rms(out, ng_ref[1:2, :])

    _interleave([half(rows) for rows in _half_rows(tm, sample)])


def _rwkv_layer(h, mod, l, j, P, state_shift, state_wkv, *, sample):
    nb, rows, _ = h.shape
    tm = rows if sample else 2 * HALF
    nt = rows // tm
    names = ("rwkv_mu", "rwkv_w_rkv", "rwkv_w0", "rwkv_w1", "rwkv_w2", "rwkv_a0", "rwkv_a1", "rwkv_a2",
             "rwkv_g1", "rwkv_g2", "rwkv_k_k", "rwkv_k_a", "rwkv_r_k")
    consts = [(P["norm_g"], l)] + [(P[n], j) for n in names]
    seg1, seg2p = P["seg1"], P["seg2p"]
    in_specs = ([_rows(tm, D), _mod_spec(mod, l)] + [_layer(a, i) for a, i in consts]
                + [_whole(seg1.shape), _whole(seg2p.shape)])
    args = [h, mod] + [a for a, _ in consts] + [seg1, seg2p]
    if sample:
        in_specs.append(_per_seq(rows, D))
        args.append(state_shift)
        wc_shape, wc_spec = (nb, rows, D), _rows(tm, D)
        sh_shape, sh_spec = (nb, rows, D), _rows(tm, D)
        scratch = []
    else:
        ti = np.arange(HALF)
        tri = _np_bf16((ti[:, None] // CHUNK == ti[None, :] // CHUNK) & (ti[None, :] <= ti[:, None]))
        sel = _np_bf16(ti[None, :] == (np.arange(CARRY_ROWS)[:, None] * CHUNK + CHUNK - 1))
        in_specs += [_whole(tri.shape), _whole(sel.shape)]
        args += [tri, sel]
        wc_shape = (nb, rows // HALF, CARRY_ROWS, D)
        wc_spec = pl.BlockSpec((None, tm // HALF, CARRY_ROWS, D), lambda b, t: (b, t, 0, 0))
        sh_shape, sh_spec = (nb, 1, D), _per_seq(1, D)
        scratch = [pltpu.VMEM((CARRY_ROWS, D), F32)]
    f32 = jax.ShapeDtypeStruct(h.shape, F32)
    rt_shape = f32 if sample else jax.ShapeDtypeStruct(h.shape, BF16)
    at, bt, kt, rt, vv, g, bonus, wc, new_shift = pl.pallas_call(
        functools.partial(_rwkv_pre_kernel, sample=sample, tm=tm, nt=nt),
        grid=(nb, nt), in_specs=in_specs,
        out_specs=[_rows(tm, D)] * 7 + [wc_spec, sh_spec],
        out_shape=[f32, f32, f32, rt_shape, f32, f32, f32, jax.ShapeDtypeStruct(wc_shape, F32),
                   jax.ShapeDtypeStruct(sh_shape, F32)],
        scratch_shapes=scratch, compiler_params=_params(),
        name="rwkv_pre_sample" if sample else "rwkv_pre_prompt",
    )(*args)

    hh, n = RWKV_HEADS, RWKV_HEAD
    if sample:
        sb = 8
        nseq = rows
        hv = lambda x: x.reshape(nseq, hh, 1, n)
        vec_spec = pl.BlockSpec((sb, hh, 1, n), lambda i: (i, 0, 0, 0))
        st_spec = pl.BlockSpec((sb, hh, n, n), lambda i: (i, 0, 0, 0))
        y, new_s = pl.pallas_call(
            _rwkv_core_sample_kernel,
            grid=(nseq // sb,),
            in_specs=[vec_spec] * 6 + [st_spec],
            out_specs=[vec_spec, st_spec],
            out_shape=[jax.ShapeDtypeStruct((nseq, hh, 1, n), F32),
                       jax.ShapeDtypeStruct((nseq, hh, n, n), F32)],
            compiler_params=_params(1),
            name="rwkv_core_sample",
        )(hv(at), hv(bt), hv(kt), hv(rt), hv(vv), hv(wc), state_wkv)
        y = y.reshape(nb, rows, D)
    else:
        ncht = rows // CHUNK
        wc_rows = wc[:, :, :HALF // CHUNK, :].reshape(nb, ncht, 1, D)
        npair = hh // 2
        nc = RWKV_CHUNKS_PER_STEP
        y, new_s = pl.pallas_call(
            functools.partial(_rwkv_core_kernel, nt=ncht // nc, nchunk=nc),
            grid=(nb, ncht // nc),
            in_specs=[_rows(nc * CHUNK, D)] * 5 + [pl.BlockSpec((None, nc, 1, D), lambda b, t: (b, t, 0, 0))],
            out_specs=[_rows(nc * CHUNK, D), pl.BlockSpec((None, hh, n, n), lambda b, t: (b, 0, 0, 0))],
            out_shape=[f32, jax.ShapeDtypeStruct((nb, hh, n, n), F32)],
            scratch_shapes=[pltpu.VMEM((npair, LANES, LANES), F32)],
            compiler_params=_params(),
            name="rwkv_core_prompt",
        )(at, bt, kt, rt, vv, wc_rows)

    consts = [(P["norm_g"], l), (P["rwkv_lnx_w"], j), (P["rwkv_lnx_b"], j), (P["rwkv_w_o"], j)]
    h_new = pl.pallas_call(
        functools.partial(_rwkv_post_kernel, sample=sample, tm=tm),
        grid=(nb, nt),
        in_specs=([_rows(tm, D)] * 4 + [_mod_spec(mod, l)] + [_layer(a, i) for a, i in consts]
                  + [_whole(seg1.shape), _whole(seg2p.shape)]),
        out_specs=_rows(tm, D), out_shape=f32, compiler_params=_params(),
        name="rwkv_post_sample" if sample else "rwkv_post_prompt",
    )(y, g, bonus, h, mod, *[a for a, _ in consts], seg1, seg2p)
    return h_new, new_shift, new_s


GLA_LEVELS = (1, 2, 4, 8, 16, 32)


def _gla_pre_kernel(h_ref, mod_ref, ng_ref, win_ref, wa1_ref, wa2_ref, ba_ref,
                    q_ref, k_ref, v_ref, og_ref, lg_ref, *, sample, tm):
    mod = lambda i: _mod(mod_ref, i, sample)
    dk, dv = GLA_DK_TOTAL, GLA_DV_TOTAL

    def half(rows):
        u = (_rms(h_ref[rows, :], ng_ref[0:1, :]) * (1.0 + mod(1)) + mod(0)).astype(BF16)
        p = _bdot(u, win_ref[...])
        gate = _bdot(_bdot(u, wa1_ref[...]).astype(BF16), wa2_ref[...]) + ba_ref[...]
        yield
        q_ref[rows, :] = p[:, :dk] * (GLA_DK ** -0.5)
        k_ref[rows, :] = p[:, dk:2 * dk]
        v_ref[rows, :] = p[:, 2 * dk:2 * dk + dv]
        og_ref[rows, :] = p[:, 2 * dk + dv:]
        lg_ref[rows, :] = -_softplus(-gate) * (1.0 / GLA_GATE_NORMALIZER)

    _interleave([half(rows) for rows in _half_rows(tm, sample)])


def _gla_core_kernel(q_ref, k_ref, v_ref, lg_ref, lvl_ref, o_ref, sout_ref, s_ref, *, nt, nchunk):
    t = pl.program_id(1)

    @pl.when(t == 0)
    def _():
        s_ref[...] = jnp.zeros_like(s_ref)

    row = lax.broadcasted_iota(jnp.int32, (CHUNK, 1), 0)
    ri = lax.broadcasted_iota(jnp.int32, (CHUNK, CHUNK), 0)
    ci = lax.broadcasted_iota(jnp.int32, (CHUNK, CHUNK), 1)
    heads = range(GLA_HEADS)
    units = range(nchunk * GLA_HEADS)
    tok = [slice((i // GLA_HEADS) * CHUNK, (i // GLA_HEADS + 1) * CHUNK) for i in units]
    sl = [slice((i % GLA_HEADS) * GLA_DK, (i % GLA_HEADS + 1) * GLA_DK) for i in units]
    sv = [slice((i % GLA_HEADS) * GLA_DV, (i % GLA_HEADS + 1) * GLA_DV) for i in units]
    lg = [lg_ref[c * CHUNK:(c + 1) * CHUNK, :] for c in range(nchunk)]
    pq = [_bdot(lvl_ref[...], _cat0(*_split3(lg[c]))) for c in range(nchunk)]
    q = [q_ref[tok[i], sl[i]] for i in units]
    k = [k_ref[tok[i], sl[i]] for i in units]
    vb = [v_ref[tok[i], sv[i]].astype(BF16) for i in units]
    a = [jnp.where(ri == ci, _dot_nt(q[i].astype(BF16), k[i].astype(BF16)), 0.0) for i in units]
    for li, m in enumerate(GLA_LEVELS):
        lm = m.bit_length() - 1
        second = ((row >> lm) & 1) == 1
        block = (ri >> (lm + 1)) == (ci >> (lm + 1))
        base = (li - 1) * 2 * CHUNK
        for i in units:
            c = i // GLA_HEADS
            if m == 1:
                qe = jnp.where(second, q[i] * jnp.exp(lg[c][:, sl[i]]), 0.0)
                ke = jnp.where(second, 0.0, k[i])
            else:
                qe = jnp.where(second, q[i] * jnp.exp(pq[c][base:base + CHUNK, sl[i]]), 0.0)
                ke = jnp.where(second, 0.0, k[i] * jnp.exp(pq[c][base + CHUNK:base + 2 * CHUNK, sl[i]]))
            a[i] = a[i] + jnp.where(block, _dot_nt(qe.astype(BF16), ke.astype(BF16)), 0.0)
    base = (len(GLA_LEVELS) - 1) * 2 * CHUNK
    g_inc = [pq[i // GLA_HEADS][base:base + CHUNK, sl[i]] for i in units]
    g_rest = [pq[i // GLA_HEADS][base + CHUNK:base + 2 * CHUNK, sl[i]] for i in units]
    av = [_bdot(a[i].astype(BF16), vb[i]) for i in units]
    qg = [(q[i] * jnp.exp(g_inc[i])).astype(BF16) for i in units]
    kg = [(k[i] * jnp.exp(g_rest[i])).astype(BF16) for i in units]
    kv = [_dot_tn(vb[i], kg[i]) for i in units]
    s = [s_ref[hd] for hd in heads]
    for c in range(nchunk):
        for hd in heads:
            i = c * GLA_HEADS + hd
            o_ref[tok[i], sv[i]] = av[i] + _dot_nt(qg[i], s[hd].astype(BF16))
        s = [s[hd] * jnp.exp(g_inc[c * GLA_HEADS + hd][CHUNK - 1:CHUNK, :]) + kv[c * GLA_HEADS + hd]
             for hd in heads]
    for hd in heads:
        s_ref[hd] = s[hd]

    @pl.when(t == nt - 1)
    def _():
        sout_ref[...] = s_ref[...]


def _gla_core_sample_kernel(q_ref, k_ref, v_ref, lg_ref, s_ref, o_ref, sout_ref):
    n = GLA_DK
    eye = (lax.broadcasted_iota(jnp.int32, (n, n), 0) == lax.broadcasted_iota(jnp.int32, (n, n), 1)).astype(F32)
    col = lambda x: jnp.sum(eye * x, axis=-1, keepdims=True)
    q, k, v = q_ref[...], k_ref[...], v_ref[...]
    s = s_ref[...]
    decay = jnp.exp(lg_ref[...])
    qk = jnp.sum(q * k, axis=-1, keepdims=True)
    o_ref[...] = qk * v + jnp.sum(col(q * decay) * s, axis=-2, keepdims=True)
    sout_ref[...] = col(decay) * s + col(k) * v


def _gla_post_kernel(o_ref, og_ref, h_ref, mod_ref, ng_ref, on_ref, wo_ref, out_ref, *, sample):
    mod = lambda i: _mod(mod_ref, i, sample)
    og = og_ref[...]
    parts = []
    for hd in range(GLA_HEADS):
        sv = slice(hd * GLA_DV, (hd + 1) * GLA_DV)
        parts.append((_rms(o_ref[:, sv], on_ref[...]) * _silu(og[:, sv])).astype(BF16))
    out = _bdot(_cat1(*parts), wo_ref[...])
    out_ref[...] = h_ref[...] + mod(2) * _rms(out, ng_ref[1:2, :])


def _gla_level_matrix():
    ti = np.arange(CHUNK)
    blocks = []
    for m in GLA_LEVELS[1:] + (CHUNK,):
        same = ti[:, None] // m == ti[None, :] // m
        blocks.append(same & (ti[None, :] <= ti[:, None]))
        blocks.append(same & (ti[None, :] > ti[:, None]))
    one = np.concatenate(blocks, axis=0)
    return _np_bf16(np.concatenate([one, one, one], axis=1))


def _gla_fused_kernel(h_ref, mod_ref, ng_ref, win_ref, wa1_ref, wa2_ref, ba_ref, on_ref, wo_ref, lvl_ref,
                      out_ref, sout_ref, q_s, k_s, v_s, og_s, lg_s, o_s, s_ref, *, tm, nt):
    _gla_pre_kernel(h_ref, mod_ref, ng_ref, win_ref, wa1_ref, wa2_ref, ba_ref,
                    q_s, k_s, v_s, og_s, lg_s, sample=False, tm=tm)
    _gla_core_kernel(q_s, k_s, v_s, lg_s, lvl_ref, o_s, sout_ref, s_ref, nt=nt, nchunk=tm // CHUNK)
    _gla_post_kernel(o_s, og_s, h_ref, mod_ref, ng_ref, on_ref, wo_ref, out_ref, sample=False)


def _gla_prompt_layer(h, mod, l, j, P):
    nb, rows, _ = h.shape
    tm = 2 * HALF
    nt = rows // tm
    dk, dv, hh = GLA_DK_TOTAL, GLA_DV_TOTAL, GLA_HEADS
    lvl = _gla_level_matrix()
    consts = [(P["norm_g"], l), (P["gla_w_in"], j), (P["gla_wa1"], j), (P["gla_wa2"], j), (P["gla_ba"], j),
              (P["gla_onorm_g"], j), (P["gla_w_o"], j)]
    h_new, s_t = pl.pallas_call(
        functools.partial(_gla_fused_kernel, tm=tm, nt=nt),
        grid=(nb, nt),
        in_specs=[_rows(tm, D), _mod_spec(mod, l)] + [_layer(a, i) for a, i in consts] + [_whole(lvl.shape)],
        out_specs=[_rows(tm, D), pl.BlockSpec((None, hh, GLA_DV, GLA_DK), lambda b, t: (b, 0, 0, 0))],
        out_shape=[jax.ShapeDtypeStruct(h.shape, F32), jax.ShapeDtypeStruct((nb, hh, GLA_DV, GLA_DK), F32)],
        scratch_shapes=[pltpu.VMEM((tm, dk), F32), pltpu.VMEM((tm, dk), F32), pltpu.VMEM((tm, dv), F32),
                        pltpu.VMEM((tm, dv), F32), pltpu.VMEM((tm, dk), F32), pltpu.VMEM((tm, dv), F32),
                        pltpu.VMEM((hh, GLA_DV, GLA_DK), F32)],
        compiler_params=_params(),
        name="gla_prompt",
    )(h, mod, *[a for a, _ in consts], lvl)
    return h_new, jnp.swapaxes(s_t, -1, -2)


def _gla_layer(h, mod, l, j, P, state, *, sample):
    if not sample:
        return _gla_prompt_layer(h, mod, l, j, P)
    nb, rows, _ = h.shape
    tm = rows
    nt = rows // tm
    dk, dv, hh = GLA_DK_TOTAL, GLA_DV_TOTAL, GLA_HEADS
    consts = [(P["norm_g"], l), (P["gla_w_in"], j), (P["gla_wa1"], j), (P["gla_wa2"], j), (P["gla_ba"], j)]
    shp = lambda width: jax.ShapeDtypeStruct((nb, rows, width), F32)
    q, k, v, og, lg = pl.pallas_call(
        functools.partial(_gla_pre_kernel, sample=sample, tm=tm),
        grid=(nb, nt),
        in_specs=[_rows(tm, D), _mod_spec(mod, l)] + [_layer(a, i) for a, i in consts],
        out_specs=[_rows(tm, dk), _rows(tm, dk), _rows(tm, dv), _rows(tm, dv), _rows(tm, dk)],
        out_shape=[shp(dk), shp(dk), shp(dv), shp(dv), shp(dk)],
        compiler_params=_params(),
        name="gla_pre_sample" if sample else "gla_pre_prompt",
    )(h, mod, *[a for a, _ in consts])

    if sample:
        sb = 8
        nseq = rows
        hk = lambda x: x.reshape(nseq, hh, 1, GLA_DK)
        kspec = pl.BlockSpec((sb, hh, 1, GLA_DK), lambda i: (i, 0, 0, 0))
        vspec = pl.BlockSpec((sb, hh, 1, GLA_DV), lambda i: (i, 0, 0, 0))
        sspec = pl.BlockSpec((sb, hh, GLA_DK, GLA_DV), lambda i: (i, 0, 0, 0))
        o, new_s = pl.pallas_call(
            _gla_core_sample_kernel,
            grid=(nseq // sb,),
            in_specs=[kspec, kspec, vspec, kspec, sspec],
            out_specs=[vspec, sspec],
            out_shape=[jax.ShapeDtypeStruct((nseq, hh, 1, GLA_DV), F32),
                       jax.ShapeDtypeStruct((nseq, hh, GLA_DK, GLA_DV), F32)],
            compiler_params=_params(1),
            name="gla_core_sample",
        )(hk(q), hk(k), v.reshape(nseq, hh, 1, GLA_DV), hk(lg), state)
        o = o.reshape(nb, rows, dv)

    tp = rows
    consts = [(P["norm_g"], l), (P["gla_onorm_g"], j), (P["gla_w_o"], j)]
    h_new = pl.pallas_call(
        functools.partial(_gla_post_kernel, sample=sample),
        grid=(nb, rows // tp),
        in_specs=[_rows(tp, dv), _rows(tp, dv), _rows(tp, D), _mod_spec(mod, l)] + [_layer(a, i) for a, i in consts],
        out_specs=_rows(tp, D), out_shape=shp(D), compiler_params=_params(),
        name="gla_post_sample" if sample else "gla_post_prompt",
    )(o, og, h, mod, *[a for a, _ in consts])
    return h_new, new_s


def _trunk(h, mod, P, states, *, sample):
    st_a, st_sb, st_wb, st_gc, st_f = states
    nb, rows, _ = h.shape
    new_a, new_sb, new_wb, new_gc, new_f = [], [], [], [], []
    for l in range(DEPTH):
        kind, j = l % 3, l // 3
        if kind == 0:
            h, nbuf = _sconv_layer(h, mod, l, j, P, st_a, sample=sample)
            new_a.append(nbuf)
        elif kind == 1:
            sh = st_sb[j].reshape(1, rows, D) if sample else None
            wkv = st_wb[j] if sample else None
            h, nsh, ns = _rwkv_layer(h, mod, l, j, P, sh, wkv, sample=sample)
            new_sb.append(nsh.reshape(rows, D) if sample else nsh.reshape(nb, D))
            new_wb.append(ns)
        else:
            h, ns = _gla_layer(h, mod, l, j, P, st_gc[j] if sample else None, sample=sample)
            new_gc.append(ns)
        h, nbuf = _ffn_layer(h, mod, l, P, st_f, sample=sample)
        new_f.append(nbuf)
    return (h, jnp.stack(new_a), jnp.stack(new_sb), jnp.stack(new_wb), jnp.stack(new_gc), jnp.stack(new_f))


def _prepare(p):
    bf = lambda x: x.astype(BF16)
    row = lambda x: x.reshape(x.shape[0], 1, -1)
    pad_c = lambda x: bf(jnp.pad(x, ((0, 0), (0, 0), (0, LANES - x.shape[2]))))
    pad_r = lambda x: bf(jnp.pad(x, ((0, 0), (0, LANES - x.shape[1]), (0, 0))))
    lane_head = np.arange(D) // RWKV_HEAD
    col = np.arange(LANES)
    P = dict(
        norm_g=p["norm_g"],
        sconv_w_in=bf(p["sconv_w_in"]), sconv_conv_w=p["sconv_conv_w"], sconv_w_out=bf(p["sconv_w_out"]),
        rwkv_mu=p["rwkv_mu"], rwkv_w_rkv=bf(p["rwkv_w_rkv"]), rwkv_w0=row(p["rwkv_w0"]),
        rwkv_w1=pad_c(p["rwkv_w1"]), rwkv_w2=pad_r(p["rwkv_w2"]), rwkv_a0=row(p["rwkv_a0"]),
        rwkv_a1=pad_c(p["rwkv_a1"]), rwkv_a2=pad_r(p["rwkv_a2"]), rwkv_g1=bf(p["rwkv_g1"]), rwkv_g2=bf(p["rwkv_g2"]),
        rwkv_k_k=row(p["rwkv_k_k"]), rwkv_k_a=row(p["rwkv_k_a"]), rwkv_r_k=row(p["rwkv_r_k"]),
        rwkv_lnx_w=row(p["rwkv_lnx_w"]), rwkv_lnx_b=row(p["rwkv_lnx_b"]), rwkv_w_o=bf(p["rwkv_w_o"]),
        gla_w_in=bf(p["gla_w_in"]), gla_wa1=pad_c(p["gla_wa1"]), gla_wa2=pad_r(p["gla_wa2"]),
        gla_ba=row(p["gla_ba"]), gla_onorm_g=row(p["gla_onorm_g"]), gla_w_o=bf(p["gla_w_o"]),
        ffn_w_up=bf(p["ffn_w_up"]), ffn_conv_w=p["ffn_conv_w"], ffn_conv_b=row(p["ffn_conv_b"]),
        ffn_w_down=bf(p["ffn_w_down"]),
        seg1=_np_bf16(lane_head[:, None] == col[None, :]),
        seg2p=_np_bf16((col[:, None] % SEG_STRIDE == lane_head[None, :]) & (col[:, None] < 3 * SEG_STRIDE)),
    )
    return P


def kernel(x_prompt, x_sample, state_conv_a, state_shift_b, state_wkv_b, state_gla_c, state_conv_ffn, c_prompt, c_sample, ada_w, ada_b, norm_g, sconv_w_in, sconv_conv_w, sconv_w_out, rwkv_mu, rwkv_w_rkv, rwkv_w0, rwkv_w1, rwkv_w2, rwkv_a0, rwkv_a1, rwkv_a2, rwkv_g1, rwkv_g2, rwkv_k_k, rwkv_k_a, rwkv_r_k, rwkv_lnx_w, rwkv_lnx_b, rwkv_w_o, gla_w_in, gla_wa1, gla_wa2, gla_ba, gla_onorm_g, gla_w_o, ffn_w_up, ffn_conv_w, ffn_conv_b, ffn_w_down):
    P = _prepare(dict(
        norm_g=norm_g, sconv_w_in=sconv_w_in, sconv_conv_w=sconv_conv_w, sconv_w_out=sconv_w_out,
        rwkv_mu=rwkv_mu, rwkv_w_rkv=rwkv_w_rkv, rwkv_w0=rwkv_w0, rwkv_w1=rwkv_w1, rwkv_w2=rwkv_w2,
        rwkv_a0=rwkv_a0, rwkv_a1=rwkv_a1, rwkv_a2=rwkv_a2, rwkv_g1=rwkv_g1, rwkv_g2=rwkv_g2,
        rwkv_k_k=rwkv_k_k, rwkv_k_a=rwkv_k_a, rwkv_r_k=rwkv_r_k, rwkv_lnx_w=rwkv_lnx_w,
        rwkv_lnx_b=rwkv_lnx_b, rwkv_w_o=rwkv_w_o,
        gla_w_in=gla_w_in, gla_wa1=gla_wa1, gla_wa2=gla_wa2, gla_ba=gla_ba,
        gla_onorm_g=gla_onorm_g, gla_w_o=gla_w_o,
        ffn_w_up=ffn_w_up, ffn_conv_w=ffn_conv_w, ffn_conv_b=ffn_conv_b, ffn_w_down=ffn_w_down))
    bp, bs = x_prompt.shape[0], x_sample.shape[0]
    mod_p, mod_s = _modulation(c_prompt, c_sample, ada_w, ada_b)
    mod_p = mod_p.reshape(DEPTH, bp, N_MOD, D)
    mod_s = mod_s.reshape(DEPTH, 1, bs, N_MOD * D)
    y_p, ca_p, sb_p, wb_p, gc_p, cf_p = _trunk(x_prompt, mod_p, P, (None,) * 5, sample=False)
    y_s, ca_s, sb_s, wb_s, gc_s, cf_s = _trunk(
        x_sample.reshape(1, bs, D), mod_s, P,
        (state_conv_a, state_shift_b, state_wkv_b, state_gla_c, state_conv_ffn), sample=True)
    return (y_p, y_s.reshape(bs, 1, D), ca_p, ca_s, sb_p, sb_s, wb_p, wb_s, gc_p, gc_s, cf_p, cf_s)
```

```python
import functools

import jax
import jax.numpy as jnp
import numpy as np
from jax import lax
from jax.experimental import pallas as pl
from jax.experimental.pallas import tpu as pltpu

F32, BF16 = jnp.float32, jnp.bfloat16

D = 1024
DEPTH = 4
N_MOD = 6
RMS_EPS = 1e-6
RWKV_HEADS, RWKV_HEAD = 16, 64
RWKV_GN_EPS = 64e-5
GLA_HEADS, GLA_DK, GLA_DV = 4, 128, 256
GLA_DK_TOTAL, GLA_DV_TOTAL = 512, 1024
GLA_GATE_NORMALIZER = 16.0
D_FF = 2816

LANES = 128
CARRY_ROWS = 8
CHUNK = 64
HALF = 256
FFN_CHUNK = 256
RWKV_CHUNKS_PER_STEP = 4
SEG_STRIDE = 16
VMEM_LIMIT = 56 * 1024 * 1024


def _rms(x, g):
    return x * lax.rsqrt(jnp.mean(x * x, axis=-1, keepdims=True) + RMS_EPS) * g


def _silu(x):
    return x * jax.nn.sigmoid(x)


def _softplus(x):
    return jnp.maximum(x, 0.0) + jnp.log1p(jnp.exp(-jnp.abs(x)))


_bdot = functools.partial(jnp.dot, preferred_element_type=F32)


def _dot_nt(a, b):
    return lax.dot_general(a, b, (((1,), (1,)), ((), ())), preferred_element_type=F32)


def _dot_tn(a, b):
    return lax.dot_general(a, b, (((0,), (0,)), ((), ())), preferred_element_type=F32)


def _cat0(*xs):
    return jnp.concatenate(xs, axis=0)


def _cat1(*xs):
    return jnp.concatenate(xs, axis=1)


def _split2(x):
    hi = x.astype(BF16)
    return hi, (x - hi.astype(F32)).astype(BF16)


def _split3(x):
    hi = x.astype(BF16)
    r1 = x - hi.astype(F32)
    mid = r1.astype(BF16)
    return hi, mid, (r1 - mid.astype(F32)).astype(BF16)


def _dot01(m01, x, terms):
    parts = (_split2 if terms == 2 else _split3)(x)
    acc = _bdot(m01, parts[0])
    for part in parts[1:]:
        acc = acc + _bdot(m01, part)
    return acc


def _segsum(x, seg1, seg2p):
    hi, lo = _split2(x)
    s = _bdot(hi, seg1) + _bdot(lo, seg1)
    s_hi, s_mid, s_lo = (part.astype(F32) for part in _split3(s))
    packed = s_hi + pltpu.roll(s_mid, SEG_STRIDE, axis=1) + pltpu.roll(s_lo, 2 * SEG_STRIDE, axis=1)
    return _bdot(packed.astype(BF16), seg2p)


def _mod(mod_ref, i, sample):
    return mod_ref[:, i * D:(i + 1) * D] if sample else mod_ref[i:i + 1, :]


def _shift_rows(x, k, carry):
    row = lax.broadcasted_iota(jnp.int32, (x.shape[0], 1), 0)
    y = pltpu.roll(x, k, axis=0)
    for j in range(k):
        src = CARRY_ROWS - k + j
        y = jnp.where(row == j, carry[src:src + 1, :], y)
    return y


def _half_rows(tm, sample):
    return [slice(0, tm)] if sample else [slice(0, tm // 2), slice(tm // 2, tm)]


def _interleave(gens):
    live = list(gens)
    while live:
        for g in list(live):
            if next(g, StopIteration) is StopIteration:
                live.remove(g)


def _whole(shape):
    nd = len(shape)
    return pl.BlockSpec(tuple(shape), lambda b, t: (0,) * nd, pipeline_mode=pl.Buffered(1))


def _layer(arr, l):
    nd = arr.ndim - 1
    return pl.BlockSpec((None,) + tuple(arr.shape[1:]), lambda b, t: (l,) + (0,) * nd,
                        pipeline_mode=pl.Buffered(1))


def _mod_spec(mod, l):
    return pl.BlockSpec((None, None) + tuple(mod.shape[2:]), lambda b, t: (l, b, 0, 0))


def _state_spec(state, l):
    return pl.BlockSpec((None,) + tuple(state.shape[1:]), lambda b, t: (l, 0, 0, 0))


def _new_state_spec(rows, width):
    return pl.BlockSpec((rows, 2, width), lambda b, t: (0, 0, 0))


def _rows(tm, width):
    return pl.BlockSpec((None, tm, width), lambda b, t: (b, t, 0))


def _per_seq(r, width):
    return pl.BlockSpec((None, r, width), lambda b, t: (b, 0, 0))


def _params(n=2):
    return pltpu.CompilerParams(dimension_semantics=("arbitrary",) * n, vmem_limit_bytes=VMEM_LIMIT)


def _np_bf16(a):
    return jnp.asarray(np.asarray(a, np.float32), BF16)


def _mod_kernel(cp_ref, cs_ref, w_ref, b_ref, op_ref, os_ref):
    w = w_ref[...].astype(BF16)
    op_ref[...] = _bdot(_silu(cp_ref[...]).astype(BF16), w) + b_ref[...]
    os_ref[...] = _bdot(_silu(cs_ref[...]).astype(BF16), w) + b_ref[...]


def _modulation(c_prompt, c_sample, ada_w, ada_b):
    bp, bs = c_prompt.shape[0], c_sample.shape[0]
    tn = 3072
    out = lambda n: (pl.BlockSpec((None, n, tn), lambda l, j: (l, 0, j)),
                     jax.ShapeDtypeStruct((DEPTH, n, N_MOD * D), F32))
    (sp_p, sh_p), (sp_s, sh_s) = out(bp), out(bs)
    return pl.pallas_call(
        _mod_kernel,
        grid=(DEPTH, N_MOD * D // tn),
        in_specs=[pl.BlockSpec((bp, D), lambda l, j: (0, 0)),
                  pl.BlockSpec((bs, D), lambda l, j: (0, 0)),
                  pl.BlockSpec((None, D, tn), lambda l, j: (l, 0, j)),
                  pl.BlockSpec((None, 1, tn), lambda l, j: (l, 0, j))],
        out_specs=[sp_p, sp_s], out_shape=[sh_p, sh_s],
        compiler_params=_params(),
        name="adaln_mod",
    )(c_prompt, c_sample, ada_w, ada_b.reshape(DEPTH, 1, N_MOD * D))


def _sconv_kernel(*refs, sample, tm, nt):
    if sample:
        h_ref, mod_ref, ng_ref, win_ref, cw_ref, wout_ref, st_ref, o_ref, nb_ref = refs
    else:
        h_ref, mod_ref, ng_ref, win_ref, cw_ref, wout_ref, o_ref, nb_ref, carry_ref = refs
        t = pl.program_id(1)

        @pl.when(t == 0)
        def _():
            carry_ref[...] = jnp.zeros_like(carry_ref)

    mod = lambda i: _mod(mod_ref, i, sample)
    cw = cw_ref[...]
    tails = {}

    def half(i, rows):
        n = rows.stop - rows.start
        h = h_ref[rows, :]
        u = (_rms(h, ng_ref[0:1, :]) * (1.0 + mod(1)) + mod(0)).astype(BF16)
        p = _bdot(u, win_ref[...])
        yield
        bg = p[:, :D]
        z = p[:, D:2 * D] * p[:, 2 * D:]
        if sample:
            y = st_ref[:, 0, :] * cw[0:1] + st_ref[:, 1, :] * cw[1:2] + z * cw[2:3]
            nb_ref[:, 0, :] = st_ref[:, 1, :]
            nb_ref[:, 1, :] = z
        else:
            c = carry_ref[...] if i == 0 else tails[i - 1]
            tails[i] = z[n - CARRY_ROWS:, :]
            y = _shift_rows(z, 2, c) * cw[0:1] + _shift_rows(z, 1, c) * cw[1:2] + z * cw[2:3]
        out = _bdot((bg * y).astype(BF16), wout_ref[...])
        yield
        o_ref[rows, :] = h + mod(2) * _rms(out, ng_ref[1:2, :])

    halves = _half_rows(tm, sample)
    _interleave([half(i, rows) for i, rows in enumerate(halves)])
    if not sample:
        carry_ref[...] = tails[len(halves) - 1]

        @pl.when(t == nt - 1)
        def _():
            nb_ref[...] = carry_ref[CARRY_ROWS - 2:, :]


def _sconv_layer(h, mod, l, j, P, state, *, sample):
    nb, rows, _ = h.shape
    tm = rows if sample else 2 * HALF
    nt = rows // tm
    consts = [(P["norm_g"], l), (P["sconv_w_in"], j), (P["sconv_conv_w"], j), (P["sconv_w_out"], j)]
    in_specs = [_rows(tm, D), _mod_spec(mod, l)] + [_layer(a, i) for a, i in consts]
    args = [h, mod] + [a for a, _ in consts]
    if sample:
        in_specs.append(_state_spec(state, j))
        args.append(state)
        nb_shape, nb_spec, scratch = (rows, 2, D), _new_state_spec(rows, D), []
    else:
        nb_shape, nb_spec = (nb, 2, D), _per_seq(2, D)
        scratch = [pltpu.VMEM((CARRY_ROWS, D), F32)]
    return pl.pallas_call(
        functools.partial(_sconv_kernel, sample=sample, tm=tm, nt=nt),
        grid=(nb, nt), in_specs=in_specs,
        out_specs=[_rows(tm, D), nb_spec],
        out_shape=[jax.ShapeDtypeStruct(h.shape, F32), jax.ShapeDtypeStruct(nb_shape, F32)],
        scratch_shapes=scratch, compiler_params=_params(),
        name="sconv_sample" if sample else "sconv_prompt",
    )(*args)


def _ffn_prompt_kernel(h_ref, mod_ref, ng_ref, wg_ref, wv_ref, cw_ref, cb_ref, wdn_ref, o_ref, nb_ref, carry_ref,
                       *, tm, nt):
    mod = lambda i: _mod(mod_ref, i, False)
    cw = cw_ref[...]
    t = pl.program_id(1)

    @pl.when(t == 0)
    def _():
        carry_ref[...] = jnp.zeros_like(carry_ref)

    def pre(rows):
        h = h_ref[rows, :]
        return h, (_rms(h, ng_ref[2:3, :]) * (1.0 + mod(4)) + mod(3)).astype(BF16)

    def act(hc, val):
        return (_silu(hc + cb_ref[...]) * val).astype(BF16)

    def post(rows, h, out):
        o_ref[rows, :] = h + mod(5) * _rms(out, ng_ref[3:4, :])

    def conv(g, c):
        return _shift_rows(g, 2, c) * cw[0:1] + _shift_rows(g, 1, c) * cw[1:2] + g * cw[2:3]

    half = tm // 2
    rows_a, rows_b = slice(0, half), slice(half, tm)
    h_a, u_a = pre(rows_a)
    g_a = _bdot(u_a, wg_ref[...])
    h_b, u_b = pre(rows_b)
    v_a = _bdot(u_a, wv_ref[...])
    g_b = _bdot(u_b, wg_ref[...])
    act_a = act(conv(g_a, carry_ref[...]), v_a)
    v_b = _bdot(u_b, wv_ref[...])
    d_a = _bdot(act_a, wdn_ref[...])
    act_b = act(conv(g_b, g_a[half - CARRY_ROWS:, :]), v_b)
    carry_ref[...] = g_b[half - CARRY_ROWS:, :]
    d_b = _bdot(act_b, wdn_ref[...])
    post(rows_a, h_a, d_a)
    post(rows_b, h_b, d_b)

    @pl.when(t == nt - 1)
    def _():
        nb_ref[...] = carry_ref[CARRY_ROWS - 2:, :]


def _ffn_sample_kernel(h_ref, mod_ref, ng_ref, wg_ref, wv_ref, cw_ref, cb_ref, wd_ref, st_ref,
                       o_ref, nb_ref, wgb_ref, wvb_ref, wdb_ref, u_ref, acc_ref, *, nj):
    j = pl.program_id(0)
    mod = lambda i: _mod(mod_ref, i, True)

    @pl.when(j == 0)
    def _():
        u_ref[...] = (_rms(h_ref[...], ng_ref[2:3, :]) * (1.0 + mod(4)) + mod(3)).astype(BF16)
        acc_ref[...] = jnp.zeros_like(acc_ref)

    wg, wv, wd = wg_ref[...].astype(BF16), wv_ref[...].astype(BF16), wd_ref[...].astype(BF16)
    wgb_ref[...], wvb_ref[...], wdb_ref[...] = wg, wv, wd
    u = u_ref[...]
    g = _bdot(u, wg)
    val = _bdot(u, wv)
    cw = cw_ref[...]
    hc = st_ref[:, 0, :] * cw[0:1] + st_ref[:, 1, :] * cw[1:2] + g * cw[2:3]
    nb_ref[:, 0, :] = st_ref[:, 1, :]
    nb_ref[:, 1, :] = g
    acc_ref[...] += _bdot((_silu(hc + cb_ref[...]) * val).astype(BF16), wd)

    @pl.when(j == nj - 1)
    def _():
        o_ref[...] = h_ref[...] + mod(5) * _rms(acc_ref[...], ng_ref[3:4, :])


def _ffn_prompt_layer(h, mod, l, P, wts):
    nb, rows, _ = h.shape
    tm = 2 * HALF
    nt = rows // tm
    wg, wv, wd = wts
    return pl.pallas_call(
        functools.partial(_ffn_prompt_kernel, tm=tm, nt=nt),
        grid=(nb, nt),
        in_specs=[_rows(tm, D), _mod_spec(mod, l), _layer(P["norm_g"], l), _whole(wg.shape), _whole(wv.shape),
                  _layer(P["ffn_conv_w"], l), _layer(P["ffn_conv_b"], l), _whole(wd.shape)],
        out_specs=[_rows(tm, D), _per_seq(2, D_FF)],
        out_shape=[jax.ShapeDtypeStruct(h.shape, F32), jax.ShapeDtypeStruct((nb, 2, D_FF), F32)],
        scratch_shapes=[pltpu.VMEM((CARRY_ROWS, D_FF), F32)], compiler_params=_params(),
        name="ffn_prompt",
    )(h, mod, P["norm_g"], wg, wv, P["ffn_conv_w"], P["ffn_conv_b"], wd)


def _ffn_sample_layer(h, mod, l, P, state):
    _, rows, _ = h.shape
    c = FFN_CHUNK
    nj = D_FF // c
    w_up, w_down = P["ffn_w_up_f32"], P["ffn_w_down_f32"]
    bf = lambda shape: jax.ShapeDtypeStruct(shape, BF16)
    h_new, nbuf, wg, wv, wd = pl.pallas_call(
        functools.partial(_ffn_sample_kernel, nj=nj),
        grid=(nj,),
        in_specs=[pl.BlockSpec((None, rows, D), lambda j: (0, 0, 0)),
                  pl.BlockSpec((None, None) + tuple(mod.shape[2:]), lambda j: (l, 0, 0, 0)),
                  pl.BlockSpec((None, 4, D), lambda j: (l, 0, 0)),
                  pl.BlockSpec((None, D, c), lambda j: (l, 0, j)),
                  pl.BlockSpec((None, D, c), lambda j: (l, 0, nj + j)),
                  pl.BlockSpec((None, 3, c), lambda j: (l, 0, j)),
                  pl.BlockSpec((None, 1, c), lambda j: (l, 0, j)),
                  pl.BlockSpec((None, c, D), lambda j: (l, j, 0)),
                  pl.BlockSpec((None, rows, 2, c), lambda j: (l, 0, 0, j))],
        out_specs=[pl.BlockSpec((None, rows, D), lambda j: (0, 0, 0)),
                   pl.BlockSpec((rows, 2, c), lambda j: (0, 0, j)),
                   pl.BlockSpec((D, c), lambda j: (0, j)),
                   pl.BlockSpec((D, c), lambda j: (0, j)),
                   pl.BlockSpec((c, D), lambda j: (j, 0))],
        out_shape=[jax.ShapeDtypeStruct(h.shape, F32), jax.ShapeDtypeStruct((rows, 2, D_FF), F32),
                   bf((D, D_FF)), bf((D, D_FF)), bf((D_FF, D))],
        scratch_shapes=[pltpu.VMEM((rows, D), BF16), pltpu.VMEM((rows, D), F32)],
        compiler_params=_params(1),
        name="ffn_sample",
    )(h, mod, P["norm_g"], w_up, w_up, P["ffn_conv_w"], P["ffn_conv_b"], w_down, state)
    return h_new, nbuf, (wg, wv, wd)


def _rwkv_pre_kernel(*refs, sample, tm, nt):
    (h_ref, mod_ref, ng_ref, mu_ref, wrkv_ref, w0_ref, w1_ref, w2_ref, a0_ref, a1_ref, a2_ref,
     g1_ref, g2_ref, kk_ref, ka_ref, rk_ref, seg1_ref, seg2_ref) = refs[:18]
    if sample:
        shift_ref = refs[18]
        outs = refs[19:]
    else:
        tri_ref, sel_ref = refs[18:20]
        outs = refs[20:-1]
        carry_ref = refs[-1]
        t = pl.program_id(1)

        @pl.when(t == 0)
        def _():
            carry_ref[...] = jnp.zeros_like(carry_ref)

    at_ref, bt_ref, kt_ref, rt_ref, v_ref, g_ref, bonus_ref, wc_ref, sh_ref = outs
    mod = lambda i: _mod(mod_ref, i, sample)
    seg = lambda x: _segsum(x, seg1_ref[...], seg2_ref[...])
    tails = {}

    def half(i, rows):
        n = rows.stop - rows.start
        u = _rms(h_ref[rows, :], ng_ref[0:1, :]) * (1.0 + mod(1)) + mod(0)
        if sample:
            prev = shift_ref[...]
            sh_ref[...] = u
        else:
            tails[i] = u[n - CARRY_ROWS:, :]
            prev = _shift_rows(u, 1, carry_ref[...] if i == 0 else tails[i - 1])
        xx = prev - u
        mix = lambda m: (u + xx * mu_ref[m:m + 1, :]).astype(BF16)
        r = _bdot(mix(0), wrkv_ref[0])
        k = _bdot(mix(2), wrkv_ref[1])
        v = _bdot(mix(3), wrkv_ref[2])
        zw = _bdot(mix(1), w1_ref[...])
        za = _bdot(mix(4), a1_ref[...])
        zg = _bdot(mix(5), g1_ref[...])
        yield
        v_ref[rows, :] = v
        z = w0_ref[...] + _bdot(jnp.tanh(zw).astype(BF16), w2_ref[...])
        a = jax.nn.sigmoid(a0_ref[...] + _bdot(za.astype(BF16), a2_ref[...]))
        g_ref[rows, :] = _bdot(jax.nn.sigmoid(zg).astype(BF16), g2_ref[...])
        lw = -jnp.exp(-_softplus(-z) - 0.5)
        kk = k * kk_ref[...]
        k2 = k * (1.0 + (a - 1.0) * ka_ref[...])
        ss = seg(kk * kk)
        rk = seg(r * k2 * rk_ref[...])
        if sample:
            lc = lw
        else:
            lc = _dot01(tri_ref[...], lw, 2)
        yield
        kk = kk * jnp.minimum(lax.rsqrt(ss), 1e12)
        bonus_ref[rows, :] = rk * v
        if sample:
            wc_ref[...] = jnp.exp(lw)
        else:
            wc_ref[i] = jnp.exp(_dot01(sel_ref[...], lc, 3))
        e_neg = jnp.exp(-lc)
        at_ref[rows, :] = -kk * jnp.exp(lc - lw)
        bt_ref[rows, :] = kk * a * e_neg
        kt_ref[rows, :] = k2 * e_neg
        rt_ref[rows, :] = (r * jnp.exp(lc)).astype(rt_ref.dtype)

    halves = _half_rows(tm, sample)
    _interleave([half(i, rows) for i, rows in enumerate(halves)])
    if not sample:
        carry_ref[...] = tails[len(halves) - 1]

        @pl.when(t == nt - 1)
        def _():
            sh_ref[...] = carry_ref[CARRY_ROWS - 1:, :]


def _rwkv_core_kernel(at_ref, bt_ref, kt_ref, rt_ref, v_ref, wc_ref, y_ref, sout_ref, s_ref, *, nt, nchunk):
    t = pl.program_id(1)

    @pl.when(t == 0)
    def _():
        s_ref[...] = jnp.zeros_like(s_ref)

    n = RWKV_HEAD
    lane_lo = lax.broadcasted_iota(jnp.int32, (1, LANES), 1) < n
    ri = lax.broadcasted_iota(jnp.int32, (CHUNK, LANES), 0)
    ci = lax.broadcasted_iota(jnp.int32, (CHUNK, LANES), 1) & (n - 1)
    m_strict = ci < ri
    m_incl = ci <= ri
    eye = (ci == ri).astype(F32)
    shift = n.bit_length() - 1
    m_bd = ((lax.broadcasted_iota(jnp.int32, (LANES, LANES), 0) >> shift)
            == (lax.broadcasted_iota(jnp.int32, (LANES, LANES), 1) >> shift))

    def bd(x):
        z = jnp.zeros_like(x)
        return _cat0(jnp.where(lane_lo, x, z), jnp.where(lane_lo, z, x))

    bd2 = lambda xs: (bd(xs[0]), bd(xs[1]))

    zero = jnp.zeros((LANES, LANES), BF16)
    halves = lambda w: w[:, :LANES] + w[:, LANES:]

    def rhs3(y):
        return _cat0(_cat1(y[0], y[1]), _cat1(y[0], zero))

    def rhs3_t(y):
        return _cat0(_cat1(y[0], y[0]), _cat1(y[1], zero))

    pairs = range(RWKV_HEADS // 2)
    lanes = [slice(p * LANES, (p + 1) * LANES) for p in pairs]
    npair = len(lanes)
    units = range(nchunk * npair)
    tok = [slice((i // npair) * CHUNK, (i // npair + 1) * CHUNK) for i in units]
    ln = [lanes[i % npair] for i in units]
    a_k = [_cat1(*_split2(at_ref[tok[i], ln[i]])) for i in units]
    b_s = [_split2(bt_ref[tok[i], ln[i]]) for i in units]
    k_s = [_split2(kt_ref[tok[i], ln[i]]) for i in units]
    v_s = [_split2(v_ref[tok[i], ln[i]]) for i in units]
    b_bd = [bd2(b_s[i]) for i in units]
    k_bd = [bd2(k_s[i]) for i in units]
    v_bd = [bd2(v_s[i]) for i in units]
    r_s = [rt_ref[tok[i], ln[i]] for i in units]
    bk_hi = [_cat0(b_s[i][0], k_s[i][0]) for i in units]
    a_ak = [_split2(jnp.where(m_strict, halves(_dot_nt(a_k[i], rhs3_t(k_bd[i]))), 0.0)) for i in units]
    sc_r = [_dot_nt(r_s[i], _cat0(b_bd[i][0], k_bd[i][0])) for i in units]
    a_r = [_cat1(jnp.where(m_incl, sc_r[i][:, :LANES], 0.0).astype(BF16),
                 jnp.where(m_incl, sc_r[i][:, LANES:], 0.0).astype(BF16)) for i in units]
    pw = [jnp.where(m_strict, halves(_dot_nt(a_k[i], rhs3_t(b_bd[i]))), 0.0) for i in units]
    tinv = [eye + pw[i] for i in units]
    pws = [_split2(pw[i]) for i in units]
    pw = [halves(_bdot(_cat1(*pws[i]), rhs3(bd2(pws[i])))) for i in units]
    for step in range(5):
        pws = [_split2(pw[i]) for i in units]
        tis = [_split2(tinv[i]) for i in units]
        rhs = [rhs3(bd2(pws[i])) for i in units]
        if step < 4:
            w = [_bdot(_cat0(_cat1(*pws[i]), _cat1(*tis[i])), rhs[i]) for i in units]
            pw = [halves(w[i][:CHUNK]) for i in units]
            tinv = [tinv[i] + halves(w[i][CHUNK:]) for i in units]
        else:
            tinv = [tinv[i] + halves(_bdot(_cat1(*tis[i]), rhs[i])) for i in units]
    tis = [_cat1(*_split2(tinv[i])) for i in units]
    s = [s_ref[p] for p in pairs]
    for c in range(nchunk):
        un = [c * npair + p for p in pairs]
        ss = [_split2(s[p]) for p in pairs]
        x = [halves(_dot_nt(a_k[i], rhs3_t(ss[p]))) + halves(_bdot(_cat1(*a_ak[i]), rhs3(v_bd[i])))
             for p, i in enumerate(un)]
        ub = [halves(_bdot(tis[i], rhs3(bd2(_split2(x[p]))))).astype(BF16) for p, i in enumerate(un)]
        for p, i in enumerate(un):
            y_ref[tok[i], lanes[p]] = (_dot_nt(r_s[i], ss[p][0])
                                       + _bdot(a_r[i], _cat0(bd(ub[p]), v_bd[i][0])))
        s = [jnp.where(m_bd, s[p] + _dot_tn(_cat0(ub[p], v_s[i][0]), bk_hi[i]), 0.0) * wc_ref[c][:, lanes[p]]
             for p, i in enumerate(un)]
    for p in pairs:
        s_ref[p] = s[p]

    @pl.when(t == nt - 1)
    def _():
        for p in pairs:
            sout_ref[2 * p] = s[p][:n, :n]
            sout_ref[2 * p + 1] = pltpu.roll(s[p], n, axis=1)[n:, :n]


def _rwkv_core_sample_kernel(at_ref, bt_ref, kt_ref, rt_ref, v_ref, wc_ref, s_ref, y_ref, sout_ref):
    n = RWKV_HEAD
    eye = (lax.broadcasted_iota(jnp.int32, (n, n), 0) == lax.broadcasted_iota(jnp.int32, (n, n), 1)).astype(F32)
    s = s_ref[...]
    u = jnp.sum(s * at_ref[...], axis=-1, keepdims=True)
    v_col = jnp.sum(eye * v_ref[...], axis=-1, keepdims=True)
    s1 = s + u * bt_ref[...] + v_col * kt_ref[...]
    y_col = jnp.sum(s1 * rt_ref[...], axis=-1, keepdims=True)
    y_ref[...] = jnp.sum(eye * y_col, axis=-2, keepdims=True)
    sout_ref[...] = s1 * wc_ref[...]


def _rwkv_post_kernel(y_ref, g_ref, bonus_ref, h_ref, mod_ref, ng_ref, lnw_ref, lnb_ref, wo_ref,
                      seg1_ref, seg2_ref, o_ref, *, sample, tm):
    mod = lambda i: _mod(mod_ref, i, sample)
    seg = lambda x: _segsum(x, seg1_ref[...], seg2_ref[...])

    def half(rows):
        y = y_ref[rows, :]
        mean = seg(y) * (1.0 / RWKV_HEAD)
        yield
        yc = y - mean
        var = seg(yc * yc) * (1.0 / RWKV_HEAD)
        yield
        yn = yc * lax.rsqrt(var + RWKV_GN_EPS) * lnw_ref[...] + lnb_ref[...] + bonus_ref[rows, :]
        out = _bdot((yn * g_ref[rows, :]).astype(BF16), wo_ref[...])
        yield
        o_ref[rows, :] = h_ref[rows, :] + mod(2) * _rms(out, ng_ref[1:2, :])

    _interleave([half(rows) for rows in _half_rows(tm, sample)])


def _rwkv_layer(h, mod, l, j, P, state_shift, state_wkv, *, sample):
    nb, rows, _ = h.shape
    tm = rows if sample else 2 * HALF
    nt = rows // tm
    names = ("rwkv_mu", "rwkv_w_rkv", "rwkv_w0", "rwkv_w1", "rwkv_w2", "rwkv_a0", "rwkv_a1", "rwkv_a2",
             "rwkv_g1", "rwkv_g2", "rwkv_k_k", "rwkv_k_a", "rwkv_r_k")
    consts = [(P["norm_g"], l)] + [(P[n], j) for n in names]
    seg1, seg2p = P["seg1"], P["seg2p"]
    in_specs = ([_rows(tm, D), _mod_spec(mod, l)] + [_layer(a, i) for a, i in consts]
                + [_whole(seg1.shape), _whole(seg2p.shape)])
    args = [h, mod] + [a for a, _ in consts] + [seg1, seg2p]
    if sample:
        in_specs.append(_per_seq(rows, D))
        args.append(state_shift)
        wc_shape, wc_spec = (nb, rows, D), _rows(tm, D)
        sh_shape, sh_spec = (nb, rows, D), _rows(tm, D)
        scratch = []
    else:
        ti = np.arange(HALF)
        tri = _np_bf16((ti[:, None] // CHUNK == ti[None, :] // CHUNK) & (ti[None, :] <= ti[:, None]))
        sel = _np_bf16(ti[None, :] == (np.arange(CARRY_ROWS)[:, None] * CHUNK + CHUNK - 1))
        in_specs += [_whole(tri.shape), _whole(sel.shape)]
        args += [tri, sel]
        wc_shape = (nb, rows // HALF, CARRY_ROWS, D)
        wc_spec = pl.BlockSpec((None, tm // HALF, CARRY_ROWS, D), lambda b, t: (b, t, 0, 0))
        sh_shape, sh_spec = (nb, 1, D), _per_seq(1, D)
        scratch = [pltpu.VMEM((CARRY_ROWS, D), F32)]
    f32 = jax.ShapeDtypeStruct(h.shape, F32)
    rt_shape = f32 if sample else jax.ShapeDtypeStruct(h.shape, BF16)
    at, bt, kt, rt, vv, g, bonus, wc, new_shift = pl.pallas_call(
        functools.partial(_rwkv_pre_kernel, sample=sample, tm=tm, nt=nt),
        grid=(nb, nt), in_specs=in_specs,
        out_specs=[_rows(tm, D)] * 7 + [wc_spec, sh_spec],
        out_shape=[f32, f32, f32, rt_shape, f32, f32, f32, jax.ShapeDtypeStruct(wc_shape, F32),
                   jax.ShapeDtypeStruct(sh_shape, F32)],
        scratch_shapes=scratch, compiler_params=_params(),
        name="rwkv_pre_sample" if sample else "rwkv_pre_prompt",
    )(*args)

    hh, n = RWKV_HEADS, RWKV_HEAD
    if sample:
        sb = 8
        nseq = rows
        hv = lambda x: x.reshape(nseq, hh, 1, n)
        vec_spec = pl.BlockSpec((sb, hh, 1, n), lambda i: (i, 0, 0, 0))
        st_spec = pl.BlockSpec((sb, hh, n, n), lambda i: (i, 0, 0, 0))
        y, new_s = pl.pallas_call(
            _rwkv_core_sample_kernel,
            grid=(nseq // sb,),
            in_specs=[vec_spec] * 6 + [st_spec],
            out_specs=[vec_spec, st_spec],
            out_shape=[jax.ShapeDtypeStruct((nseq, hh, 1, n), F32),
                       jax.ShapeDtypeStruct((nseq, hh, n, n), F32)],
            compiler_params=_params(1),
            name="rwkv_core_sample",
        )(hv(at), hv(bt), hv(kt), hv(rt), hv(vv), hv(wc), state_wkv)
        y = y.reshape(nb, rows, D)
    else:
        ncht = rows // CHUNK
        wc_rows = wc[:, :, :HALF // CHUNK, :].reshape(nb, ncht, 1, D)
        npair = hh // 2
        nc = RWKV_CHUNKS_PER_STEP
        y, new_s = pl.pallas_call(
            functools.partial(_rwkv_core_kernel, nt=ncht // nc, nchunk=nc),
            grid=(nb, ncht // nc),
            in_specs=[_rows(nc * CHUNK, D)] * 5 + [pl.BlockSpec((None, nc, 1, D), lambda b, t: (b, t, 0, 0))],
            out_specs=[_rows(nc * CHUNK, D), pl.BlockSpec((None, hh, n, n), lambda b, t: (b, 0, 0, 0))],
            out_shape=[f32, jax.ShapeDtypeStruct((nb, hh, n, n), F32)],
            scratch_shapes=[pltpu.VMEM((npair, LANES, LANES), F32)],
            compiler_params=_params(),
            name="rwkv_core_prompt",
        )(at, bt, kt, rt, vv, wc_rows)

    consts = [(P["norm_g"], l), (P["rwkv_lnx_w"], j), (P["rwkv_lnx_b"], j), (P["rwkv_w_o"], j)]
    h_new = pl.pallas_call(
        functools.partial(_rwkv_post_kernel, sample=sample, tm=tm),
        grid=(nb, nt),
        in_specs=([_rows(tm, D)] * 4 + [_mod_spec(mod, l)] + [_layer(a, i) for a, i in consts]
                  + [_whole(seg1.shape), _whole(seg2p.shape)]),
        out_specs=_rows(tm, D), out_shape=f32, compiler_params=_params(),
        name="rwkv_post_sample" if sample else "rwkv_post_prompt",
    )(y, g, bonus, h, mod, *[a for a, _ in consts], seg1, seg2p)
    return h_new, new_shift, new_s


GLA_LEVELS = (1, 2, 4, 8, 16, 32)


def _gla_pre_kernel(h_ref, mod_ref, ng_ref, win_ref, wa1_ref, wa2_ref, ba_ref,
                    q_ref, k_ref, v_ref, og_ref, lg_ref, *, sample, tm):
    mod = lambda i: _mod(mod_ref, i, sample)
    dk, dv = GLA_DK_TOTAL, GLA_DV_TOTAL

    def half(rows):
        u = (_rms(h_ref[rows, :], ng_ref[0:1, :]) * (1.0 + mod(1)) + mod(0)).astype(BF16)
        p = _bdot(u, win_ref[...])
        gate = _bdot(_bdot(u, wa1_ref[...]).astype(BF16), wa2_ref[...]) + ba_ref[...]
        yield
        q_ref[rows, :] = p[:, :dk] * (GLA_DK ** -0.5)
        k_ref[rows, :] = p[:, dk:2 * dk]
        v_ref[rows, :] = p[:, 2 * dk:2 * dk + dv]
        og_ref[rows, :] = p[:, 2 * dk + dv:]
        lg_ref[rows, :] = -_softplus(-gate) * (1.0 / GLA_GATE_NORMALIZER)

    _interleave([half(rows) for rows in _half_rows(tm, sample)])


def _gla_core_kernel(q_ref, k_ref, v_ref, lg_ref, lvl_ref, o_ref, sout_ref, s_ref, *, nt, nchunk):
    t = pl.program_id(1)

    @pl.when(t == 0)
    def _():
        s_ref[...] = jnp.zeros_like(s_ref)

    row = lax.broadcasted_iota(jnp.int32, (CHUNK, 1), 0)
    ri = lax.broadcasted_iota(jnp.int32, (CHUNK, CHUNK), 0)
    ci = lax.broadcasted_iota(jnp.int32, (CHUNK, CHUNK), 1)
    heads = range(GLA_HEADS)
    units = range(nchunk * GLA_HEADS)
    tok = [slice((i // GLA_HEADS) * CHUNK, (i // GLA_HEADS + 1) * CHUNK) for i in units]
    sl = [slice((i % GLA_HEADS) * GLA_DK, (i % GLA_HEADS + 1) * GLA_DK) for i in units]
    sv = [slice((i % GLA_HEADS) * GLA_DV, (i % GLA_HEADS + 1) * GLA_DV) for i in units]
    lg = [lg_ref[c * CHUNK:(c + 1) * CHUNK, :] for c in range(nchunk)]
    pq = [_bdot(lvl_ref[...], _cat0(*_split3(lg[c]))) for c in range(nchunk)]
    q = [q_ref[tok[i], sl[i]] for i in units]
    k = [k_ref[tok[i], sl[i]] for i in units]
    vb = [v_ref[tok[i], sv[i]].astype(BF16) for i in units]
    a = [jnp.where(ri == ci, _dot_nt(q[i].astype(BF16), k[i].astype(BF16)), 0.0) for i in units]
    for li, m in enumerate(GLA_LEVELS):
        lm = m.bit_length() - 1
        second = ((row >> lm) & 1) == 1
        block = (ri >> (lm + 1)) == (ci >> (lm + 1))
        base = (li - 1) * 2 * CHUNK
        for i in units:
            c = i // GLA_HEADS
            if m == 1:
                qe = jnp.where(second, q[i] * jnp.exp(lg[c][:, sl[i]]), 0.0)
                ke = jnp.where(second, 0.0, k[i])
            else:
                qe = jnp.where(second, q[i] * jnp.exp(pq[c][base:base + CHUNK, sl[i]]), 0.0)
                ke = jnp.where(second, 0.0, k[i] * jnp.exp(pq[c][base + CHUNK:base + 2 * CHUNK, sl[i]]))
            a[i] = a[i] + jnp.where(block, _dot_nt(qe.astype(BF16), ke.astype(BF16)), 0.0)
    base = (len(GLA_LEVELS) - 1) * 2 * CHUNK
    g_inc = [pq[i // GLA_HEADS][base:base + CHUNK, sl[i]] for i in units]
    g_rest = [pq[i // GLA_HEADS][base + CHUNK:base + 2 * CHUNK, sl[i]] for i in units]
    av = [_bdot(a[i].astype(BF16), vb[i]) for i in units]
    qg = [(q[i] * jnp.exp(g_inc[i])).astype(BF16) for i in units]
    kg = [(k[i] * jnp.exp(g_rest[i])).astype(BF16) for i in units]
    kv = [_dot_tn(vb[i], kg[i]) for i in units]
    s = [s_ref[hd] for hd in heads]
    for c in range(nchunk):
        for hd in heads:
            i = c * GLA_HEADS + hd
            o_ref[tok[i], sv[i]] = av[i] + _dot_nt(qg[i], s[hd].astype(BF16))
        s = [s[hd] * jnp.exp(g_inc[c * GLA_HEADS + hd][CHUNK - 1:CHUNK, :]) + kv[c * GLA_HEADS + hd]
             for hd in heads]
    for hd in heads:
        s_ref[hd] = s[hd]

    @pl.when(t == nt - 1)
    def _():
        sout_ref[...] = s_ref[...]


def _gla_core_sample_kernel(q_ref, k_ref, v_ref, lg_ref, s_ref, o_ref, sout_ref):
    n = GLA_DK
    eye = (lax.broadcasted_iota(jnp.int32, (n, n), 0) == lax.broadcasted_iota(jnp.int32, (n, n), 1)).astype(F32)
    col = lambda x: jnp.sum(eye * x, axis=-1, keepdims=True)
    q, k, v = q_ref[...], k_ref[...], v_ref[...]
    s = s_ref[...]
    decay = jnp.exp(lg_ref[...])
    qk = jnp.sum(q * k, axis=-1, keepdims=True)
    o_ref[...] = qk * v + jnp.sum(col(q * decay) * s, axis=-2, keepdims=True)
    sout_ref[...] = col(decay) * s + col(k) * v


def _gla_post_kernel(o_ref, og_ref, h_ref, mod_ref, ng_ref, on_ref, wo_ref, out_ref, *, sample):
    mod = lambda i: _mod(mod_ref, i, sample)
    og = og_ref[...]
    parts = []
    for hd in range(GLA_HEADS):
        sv = slice(hd * GLA_DV, (hd + 1) * GLA_DV)
        parts.append((_rms(o_ref[:, sv], on_ref[...]) * _silu(og[:, sv])).astype(BF16))
    out = _bdot(_cat1(*parts), wo_ref[...])
    out_ref[...] = h_ref[...] + mod(2) * _rms(out, ng_ref[1:2, :])


def _gla_level_matrix():
    ti = np.arange(CHUNK)
    blocks = []
    for m in GLA_LEVELS[1:] + (CHUNK,):
        same = ti[:, None] // m == ti[None, :] // m
        blocks.append(same & (ti[None, :] <= ti[:, None]))
        blocks.append(same & (ti[None, :] > ti[:, None]))
    one = np.concatenate(blocks, axis=0)
    return _np_bf16(np.concatenate([one, one, one], axis=1))


def _gla_fused_kernel(h_ref, mod_ref, ng_ref, win_ref, wa1_ref, wa2_ref, ba_ref, on_ref, wo_ref, lvl_ref,
                      out_ref, sout_ref, q_s, k_s, v_s, og_s, lg_s, o_s, s_ref, *, tm, nt):
    _gla_pre_kernel(h_ref, mod_ref, ng_ref, win_ref, wa1_ref, wa2_ref, ba_ref,
                    q_s, k_s, v_s, og_s, lg_s, sample=False, tm=tm)
    _gla_core_kernel(q_s, k_s, v_s, lg_s, lvl_ref, o_s, sout_ref, s_ref, nt=nt, nchunk=tm // CHUNK)
    _gla_post_kernel(o_s, og_s, h_ref, mod_ref, ng_ref, on_ref, wo_ref, out_ref, sample=False)


def _gla_prompt_layer(h, mod, l, j, P):
    nb, rows, _ = h.shape
    tm = 2 * HALF
    nt = rows // tm
    dk, dv, hh = GLA_DK_TOTAL, GLA_DV_TOTAL, GLA_HEADS
    lvl = _gla_level_matrix()
    consts = [(P["norm_g"], l), (P["gla_w_in"], j), (P["gla_wa1"], j), (P["gla_wa2"], j), (P["gla_ba"], j),
              (P["gla_onorm_g"], j), (P["gla_w_o"], j)]
    h_new, s_t = pl.pallas_call(
        functools.partial(_gla_fused_kernel, tm=tm, nt=nt),
        grid=(nb, nt),
        in_specs=[_rows(tm, D), _mod_spec(mod, l)] + [_layer(a, i) for a, i in consts] + [_whole(lvl.shape)],
        out_specs=[_rows(tm, D), pl.BlockSpec((None, hh, GLA_DV, GLA_DK), lambda b, t: (b, 0, 0, 0))],
        out_shape=[jax.ShapeDtypeStruct(h.shape, F32), jax.ShapeDtypeStruct((nb, hh, GLA_DV, GLA_DK), F32)],
        scratch_shapes=[pltpu.VMEM((tm, dk), F32), pltpu.VMEM((tm, dk), F32), pltpu.VMEM((tm, dv), F32),
                        pltpu.VMEM((tm, dv), F32), pltpu.VMEM((tm, dk), F32), pltpu.VMEM((tm, dv), F32),
                        pltpu.VMEM((hh, GLA_DV, GLA_DK), F32)],
        compiler_params=_params(),
        name="gla_prompt",
    )(h, mod, *[a for a, _ in consts], lvl)
    return h_new, jnp.swapaxes(s_t, -1, -2)


def _gla_layer(h, mod, l, j, P, state, *, sample):
    if not sample:
        return _gla_prompt_layer(h, mod, l, j, P)
    nb, rows, _ = h.shape
    tm = rows
    nt = rows // tm
    dk, dv, hh = GLA_DK_TOTAL, GLA_DV_TOTAL, GLA_HEADS
    consts = [(P["norm_g"], l), (P["gla_w_in"], j), (P["gla_wa1"], j), (P["gla_wa2"], j), (P["gla_ba"], j)]
    shp = lambda width: jax.ShapeDtypeStruct((nb, rows, width), F32)
    q, k, v, og, lg = pl.pallas_call(
        functools.partial(_gla_pre_kernel, sample=sample, tm=tm),
        grid=(nb, nt),
        in_specs=[_rows(tm, D), _mod_spec(mod, l)] + [_layer(a, i) for a, i in consts],
        out_specs=[_rows(tm, dk), _rows(tm, dk), _rows(tm, dv), _rows(tm, dv), _rows(tm, dk)],
        out_shape=[shp(dk), shp(dk), shp(dv), shp(dv), shp(dk)],
        compiler_params=_params(),
        name="gla_pre_sample" if sample else "gla_pre_prompt",
    )(h, mod, *[a for a, _ in consts])

    if sample:
        sb = 8
        nseq = rows
        hk = lambda x: x.reshape(nseq, hh, 1, GLA_DK)
        kspec = pl.BlockSpec((sb, hh, 1, GLA_DK), lambda i: (i, 0, 0, 0))
        vspec = pl.BlockSpec((sb, hh, 1, GLA_DV), lambda i: (i, 0, 0, 0))
        sspec = pl.BlockSpec((sb, hh, GLA_DK, GLA_DV), lambda i: (i, 0, 0, 0))
        o, new_s = pl.pallas_call(
            _gla_core_sample_kernel,
            grid=(nseq // sb,),
            in_specs=[kspec, kspec, vspec, kspec, sspec],
            out_specs=[vspec, sspec],
            out_shape=[jax.ShapeDtypeStruct((nseq, hh, 1, GLA_DV), F32),
                       jax.ShapeDtypeStruct((nseq, hh, GLA_DK, GLA_DV), F32)],
            compiler_params=_params(1),
            name="gla_core_sample",
        )(hk(q), hk(k), v.reshape(nseq, hh, 1, GLA_DV), hk(lg), state)
        o = o.reshape(nb, rows, dv)

    tp = rows
    consts = [(P["norm_g"], l), (P["gla_onorm_g"], j), (P["gla_w_o"], j)]
    h_new = pl.pallas_call(
        functools.partial(_gla_post_kernel, sample=sample),
        grid=(nb, rows // tp),
        in_specs=[_rows(tp, dv), _rows(tp, dv), _rows(tp, D), _mod_spec(mod, l)] + [_layer(a, i) for a, i in consts],
        out_specs=_rows(tp, D), out_shape=shp(D), compiler_params=_params(),
        name="gla_post_sample" if sample else "gla_post_prompt",
    )(o, og, h, mod, *[a for a, _ in consts])
    return h_new, new_s


def _trunk(h, mod, P, states, ffn_wts, *, sample):
    st_a, st_sb, st_wb, st_gc, st_f = states
    nb, rows, _ = h.shape
    new_a, new_sb, new_wb, new_gc, new_f = [], [], [], [], []
    for l in range(DEPTH):
        kind, j = l % 3, l // 3
        if kind == 0:
            h, nbuf = _sconv_layer(h, mod, l, j, P, st_a, sample=sample)
            new_a.append(nbuf)
        elif kind == 1:
            sh = st_sb[j].reshape(1, rows, D) if sample else None
            wkv = st_wb[j] if sample else None
            h, nsh, ns = _rwkv_layer(h, mod, l, j, P, sh, wkv, sample=sample)
            new_sb.append(nsh.reshape(rows, D) if sample else nsh.reshape(nb, D))
            new_wb.append(ns)
        else:
            h, ns = _gla_layer(h, mod, l, j, P, st_gc[j] if sample else None, sample=sample)
            new_gc.append(ns)
        if sample:
            h, nbuf, wts = _ffn_sample_layer(h, mod, l, P, st_f)
            ffn_wts.append(wts)
        else:
            h, nbuf = _ffn_prompt_layer(h, mod, l, P, ffn_wts[l])
        new_f.append(nbuf)
    return (h, jnp.stack(new_a), jnp.stack(new_sb), jnp.stack(new_wb), jnp.stack(new_gc), jnp.stack(new_f))


def _prepare(p):
    bf = lambda x: x.astype(BF16)
    row = lambda x: x.reshape(x.shape[0], 1, -1)
    pad_c = lambda x: bf(jnp.pad(x, ((0, 0), (0, 0), (0, LANES - x.shape[2]))))
    pad_r = lambda x: bf(jnp.pad(x, ((0, 0), (0, LANES - x.shape[1]), (0, 0))))
    lane_head = np.arange(D) // RWKV_HEAD
    col = np.arange(LANES)
    P = dict(
        norm_g=p["norm_g"],
        sconv_w_in=bf(p["sconv_w_in"]), sconv_conv_w=p["sconv_conv_w"], sconv_w_out=bf(p["sconv_w_out"]),
        rwkv_mu=p["rwkv_mu"], rwkv_w_rkv=bf(p["rwkv_w_rkv"]), rwkv_w0=row(p["rwkv_w0"]),
        rwkv_w1=pad_c(p["rwkv_w1"]), rwkv_w2=pad_r(p["rwkv_w2"]), rwkv_a0=row(p["rwkv_a0"]),
        rwkv_a1=pad_c(p["rwkv_a1"]), rwkv_a2=pad_r(p["rwkv_a2"]), rwkv_g1=bf(p["rwkv_g1"]), rwkv_g2=bf(p["rwkv_g2"]),
        rwkv_k_k=row(p["rwkv_k_k"]), rwkv_k_a=row(p["rwkv_k_a"]), rwkv_r_k=row(p["rwkv_r_k"]),
        rwkv_lnx_w=row(p["rwkv_lnx_w"]), rwkv_lnx_b=row(p["rwkv_lnx_b"]), rwkv_w_o=bf(p["rwkv_w_o"]),
        gla_w_in=bf(p["gla_w_in"]), gla_wa1=pad_c(p["gla_wa1"]), gla_wa2=pad_r(p["gla_wa2"]),
        gla_ba=row(p["gla_ba"]), gla_onorm_g=row(p["gla_onorm_g"]), gla_w_o=bf(p["gla_w_o"]),
        ffn_w_up_f32=p["ffn_w_up"], ffn_conv_w=p["ffn_conv_w"], ffn_conv_b=row(p["ffn_conv_b"]),
        ffn_w_down_f32=p["ffn_w_down"],
        seg1=_np_bf16(lane_head[:, None] == col[None, :]),
        seg2p=_np_bf16((col[:, None] % SEG_STRIDE == lane_head[None, :]) & (col[:, None] < 3 * SEG_STRIDE)),
    )
    return P


def kernel(x_prompt, x_sample, state_conv_a, state_shift_b, state_wkv_b, state_gla_c, state_conv_ffn, c_prompt, c_sample, ada_w, ada_b, norm_g, sconv_w_in, sconv_conv_w, sconv_w_out, rwkv_mu, rwkv_w_rkv, rwkv_w0, rwkv_w1, rwkv_w2, rwkv_a0, rwkv_a1, rwkv_a2, rwkv_g1, rwkv_g2, rwkv_k_k, rwkv_k_a, rwkv_r_k, rwkv_lnx_w, rwkv_lnx_b, rwkv_w_o, gla_w_in, gla_wa1, gla_wa2, gla_ba, gla_onorm_g, gla_w_o, ffn_w_up, ffn_conv_w, ffn_conv_b, ffn_w_down):
    P = _prepare(dict(
        norm_g=norm_g, sconv_w_in=sconv_w_in, sconv_conv_w=sconv_conv_w, sconv_w_out=sconv_w_out,
        rwkv_mu=rwkv_mu, rwkv_w_rkv=rwkv_w_rkv, rwkv_w0=rwkv_w0, rwkv_w1=rwkv_w1, rwkv_w2=rwkv_w2,
        rwkv_a0=rwkv_a0, rwkv_a1=rwkv_a1, rwkv_a2=rwkv_a2, rwkv_g1=rwkv_g1, rwkv_g2=rwkv_g2,
        rwkv_k_k=rwkv_k_k, rwkv_k_a=rwkv_k_a, rwkv_r_k=rwkv_r_k, rwkv_lnx_w=rwkv_lnx_w,
        rwkv_lnx_b=rwkv_lnx_b, rwkv_w_o=rwkv_w_o,
        gla_w_in=gla_w_in, gla_wa1=gla_wa1, gla_wa2=gla_wa2, gla_ba=gla_ba,
        gla_onorm_g=gla_onorm_g, gla_w_o=gla_w_o,
        ffn_w_up=ffn_w_up, ffn_conv_w=ffn_conv_w, ffn_conv_b=ffn_conv_b, ffn_w_down=ffn_w_down))
    bp, bs = x_prompt.shape[0], x_sample.shape[0]
    mod_p, mod_s = _modulation(c_prompt, c_sample, ada_w, ada_b)
    mod_p = mod_p.reshape(DEPTH, bp, N_MOD, D)
    mod_s = mod_s.reshape(DEPTH, 1, bs, N_MOD * D)
    ffn_wts = []
    y_s, ca_s, sb_s, wb_s, gc_s, cf_s = _trunk(
        x_sample.reshape(1, bs, D), mod_s, P,
        (state_conv_a, state_shift_b, state_wkv_b, state_gla_c, state_conv_ffn), ffn_wts, sample=True)
    y_p, ca_p, sb_p, wb_p, gc_p, cf_p = _trunk(x_prompt, mod_p, P, (None,) * 5, ffn_wts, sample=False)
    return (y_p, y_s.reshape(bs, 1, D), ca_p, ca_s, sb_p, sb_s, wb_p, wb_s, gc_p, gc_s, cf_p, cf_s)
```

```python
import functools

import jax
import jax.numpy as jnp
import numpy as np
from jax import lax
from jax.experimental import pallas as pl
from jax.experimental.pallas import tpu as pltpu

F32, BF16 = jnp.float32, jnp.bfloat16

D = 1024
DEPTH = 4
N_MOD = 6
RMS_EPS = 1e-6
RWKV_HEADS, RWKV_HEAD = 16, 64
RWKV_GN_EPS = 64e-5
DECAY_SCALE = float(np.exp(-0.5))
GLA_HEADS, GLA_DK, GLA_DV = 4, 128, 256
GLA_DK_TOTAL, GLA_DV_TOTAL = 512, 1024
GLA_GATE_NORMALIZER = 16.0
D_FF = 2816

LANES = 128
CARRY_ROWS = 8
CHUNK = 64
HALF = 256
FFN_CHUNK = 256
RWKV_CHUNKS_PER_STEP = 4
SEG_STRIDE = 16
VMEM_LIMIT = 56 * 1024 * 1024


def _rms(x, g):
    return x * lax.rsqrt(jnp.mean(x * x, axis=-1, keepdims=True) + RMS_EPS) * g


def _silu(x):
    return x * jax.nn.sigmoid(x)


def _softplus(x):
    return jnp.maximum(x, 0.0) + jnp.log1p(jnp.exp(-jnp.abs(x)))


_bdot = functools.partial(jnp.dot, preferred_element_type=F32)


def _dot_nt(a, b):
    return lax.dot_general(a, b, (((1,), (1,)), ((), ())), preferred_element_type=F32)


def _dot_tn(a, b):
    return lax.dot_general(a, b, (((0,), (0,)), ((), ())), preferred_element_type=F32)


def _cat0(*xs):
    return jnp.concatenate(xs, axis=0)


def _cat1(*xs):
    return jnp.concatenate(xs, axis=1)


def _split2(x):
    hi = x.astype(BF16)
    return hi, (x - hi.astype(F32)).astype(BF16)


def _split3(x):
    hi = x.astype(BF16)
    r1 = x - hi.astype(F32)
    mid = r1.astype(BF16)
    return hi, mid, (r1 - mid.astype(F32)).astype(BF16)


def _dot01(m01, x, terms):
    parts = (_split2 if terms == 2 else _split3)(x)
    acc = _bdot(m01, parts[0])
    for part in parts[1:]:
        acc = acc + _bdot(m01, part)
    return acc


def _segsum(x, seg1, seg2p):
    hi, lo = _split2(x)
    s = _bdot(hi, seg1) + _bdot(lo, seg1)
    s_hi, s_mid, s_lo = (part.astype(F32) for part in _split3(s))
    packed = s_hi + pltpu.roll(s_mid, SEG_STRIDE, axis=1) + pltpu.roll(s_lo, 2 * SEG_STRIDE, axis=1)
    return _bdot(packed.astype(BF16), seg2p)


def _mod(mod_ref, i, sample):
    return mod_ref[:, i * D:(i + 1) * D] if sample else mod_ref[i:i + 1, :]


def _shift_rows(x, k, carry):
    row = lax.broadcasted_iota(jnp.int32, (x.shape[0], 1), 0)
    y = pltpu.roll(x, k, axis=0)
    for j in range(k):
        src = CARRY_ROWS - k + j
        y = jnp.where(row == j, carry[src:src + 1, :], y)
    return y


def _half_rows(tm, sample):
    return [slice(0, tm)] if sample else [slice(0, tm // 2), slice(tm // 2, tm)]


def _interleave(gens):
    live = list(gens)
    while live:
        for g in list(live):
            if next(g, StopIteration) is StopIteration:
                live.remove(g)


def _whole(shape):
    nd = len(shape)
    return pl.BlockSpec(tuple(shape), lambda b, t: (0,) * nd, pipeline_mode=pl.Buffered(1))


def _layer(arr, l):
    nd = arr.ndim - 1
    return pl.BlockSpec((None,) + tuple(arr.shape[1:]), lambda b, t: (l,) + (0,) * nd,
                        pipeline_mode=pl.Buffered(1))


def _mod_spec(mod, l):
    return pl.BlockSpec((None, None) + tuple(mod.shape[2:]), lambda b, t: (l, b, 0, 0))


def _state_spec(state, l):
    return pl.BlockSpec((None,) + tuple(state.shape[1:]), lambda b, t: (l, 0, 0, 0))


def _new_state_spec(rows, width):
    return pl.BlockSpec((rows, 2, width), lambda b, t: (0, 0, 0))


def _rows(tm, width):
    return pl.BlockSpec((None, tm, width), lambda b, t: (b, t, 0))


def _per_seq(r, width):
    return pl.BlockSpec((None, r, width), lambda b, t: (b, 0, 0))


def _params(n=2):
    return pltpu.CompilerParams(dimension_semantics=("arbitrary",) * n, vmem_limit_bytes=VMEM_LIMIT)


def _np_bf16(a):
    return jnp.asarray(np.asarray(a, np.float32), BF16)


def _mod_kernel(cp_ref, cs_ref, w_ref, b_ref, op_ref, os_ref):
    w = w_ref[...].astype(BF16)
    op_ref[...] = _bdot(_silu(cp_ref[...]).astype(BF16), w) + b_ref[...]
    os_ref[...] = _bdot(_silu(cs_ref[...]).astype(BF16), w) + b_ref[...]


def _modulation(c_prompt, c_sample, ada_w, ada_b):
    bp, bs = c_prompt.shape[0], c_sample.shape[0]
    tn = 3072
    out = lambda n: (pl.BlockSpec((None, n, tn), lambda l, j: (l, 0, j)),
                     jax.ShapeDtypeStruct((DEPTH, n, N_MOD * D), F32))
    (sp_p, sh_p), (sp_s, sh_s) = out(bp), out(bs)
    return pl.pallas_call(
        _mod_kernel,
        grid=(DEPTH, N_MOD * D // tn),
        in_specs=[pl.BlockSpec((bp, D), lambda l, j: (0, 0)),
                  pl.BlockSpec((bs, D), lambda l, j: (0, 0)),
                  pl.BlockSpec((None, D, tn), lambda l, j: (l, 0, j)),
                  pl.BlockSpec((None, 1, tn), lambda l, j: (l, 0, j))],
        out_specs=[sp_p, sp_s], out_shape=[sh_p, sh_s],
        compiler_params=_params(),
        name="adaln_mod",
    )(c_prompt, c_sample, ada_w, ada_b.reshape(DEPTH, 1, N_MOD * D))


def _sconv_kernel(*refs, sample, tm, nt):
    if sample:
        h_ref, mod_ref, ng_ref, win_ref, cw_ref, wout_ref, st_ref, o_ref, nb_ref = refs
    else:
        h_ref, mod_ref, ng_ref, win_ref, cw_ref, wout_ref, o_ref, nb_ref, carry_ref = refs
        t = pl.program_id(1)

        @pl.when(t == 0)
        def _():
            carry_ref[...] = jnp.zeros_like(carry_ref)

    mod = lambda i: _mod(mod_ref, i, sample)
    cw = cw_ref[...]
    tails = {}

    def half(i, rows):
        n = rows.stop - rows.start
        h = h_ref[rows, :]
        u = (_rms(h, ng_ref[0:1, :]) * (1.0 + mod(1)) + mod(0)).astype(BF16)
        p = _bdot(u, win_ref[...])
        yield
        bg = p[:, :D]
        z = p[:, D:2 * D] * p[:, 2 * D:]
        if sample:
            y = st_ref[:, 0, :] * cw[0:1] + st_ref[:, 1, :] * cw[1:2] + z * cw[2:3]
            nb_ref[:, 0, :] = st_ref[:, 1, :]
            nb_ref[:, 1, :] = z
        else:
            c = carry_ref[...] if i == 0 else tails[i - 1]
            tails[i] = z[n - CARRY_ROWS:, :]
            y = _shift_rows(z, 2, c) * cw[0:1] + _shift_rows(z, 1, c) * cw[1:2] + z * cw[2:3]
        out = _bdot((bg * y).astype(BF16), wout_ref[...])
        yield
        o_ref[rows, :] = h + mod(2) * _rms(out, ng_ref[1:2, :])

    halves = _half_rows(tm, sample)
    _interleave([half(i, rows) for i, rows in enumerate(halves)])
    if not sample:
        carry_ref[...] = tails[len(halves) - 1]

        @pl.when(t == nt - 1)
        def _():
            nb_ref[...] = carry_ref[CARRY_ROWS - 2:, :]


def _sconv_layer(h, mod, l, j, P, state, *, sample):
    nb, rows, _ = h.shape
    tm = rows if sample else 2 * HALF
    nt = rows // tm
    consts = [(P["norm_g"], l), (P["sconv_w_in"], j), (P["sconv_conv_w"], j), (P["sconv_w_out"], j)]
    in_specs = [_rows(tm, D), _mod_spec(mod, l)] + [_layer(a, i) for a, i in consts]
    args = [h, mod] + [a for a, _ in consts]
    if sample:
        in_specs.append(_state_spec(state, j))
        args.append(state)
        nb_shape, nb_spec, scratch = (rows, 2, D), _new_state_spec(rows, D), []
    else:
        nb_shape, nb_spec = (nb, 2, D), _per_seq(2, D)
        scratch = [pltpu.VMEM((CARRY_ROWS, D), F32)]
    return pl.pallas_call(
        functools.partial(_sconv_kernel, sample=sample, tm=tm, nt=nt),
        grid=(nb, nt), in_specs=in_specs,
        out_specs=[_rows(tm, D), nb_spec],
        out_shape=[jax.ShapeDtypeStruct(h.shape, F32), jax.ShapeDtypeStruct(nb_shape, F32)],
        scratch_shapes=scratch, compiler_params=_params(),
        name="sconv_sample" if sample else "sconv_prompt",
    )(*args)


def _ffn_prompt_kernel(h_ref, mod_ref, ng_ref, wg_ref, wv_ref, cw_ref, cb_ref, wdn_ref, o_ref, nb_ref, carry_ref,
                       *, tm, nt):
    mod = lambda i: _mod(mod_ref, i, False)
    cw = cw_ref[...]
    t = pl.program_id(1)

    @pl.when(t == 0)
    def _():
        carry_ref[...] = jnp.zeros_like(carry_ref)

    def pre(rows):
        h = h_ref[rows, :]
        return h, (_rms(h, ng_ref[2:3, :]) * (1.0 + mod(4)) + mod(3)).astype(BF16)

    def act(hc, val):
        return (_silu(hc + cb_ref[...]) * val).astype(BF16)

    def post(rows, h, out):
        o_ref[rows, :] = h + mod(5) * _rms(out, ng_ref[3:4, :])

    def conv(g, c):
        return _shift_rows(g, 2, c) * cw[0:1] + _shift_rows(g, 1, c) * cw[1:2] + g * cw[2:3]

    half = tm // 2
    rows_a, rows_b = slice(0, half), slice(half, tm)
    h_a, u_a = pre(rows_a)
    g_a = _bdot(u_a, wg_ref[...])
    h_b, u_b = pre(rows_b)
    v_a = _bdot(u_a, wv_ref[...])
    g_b = _bdot(u_b, wg_ref[...])
    act_a = act(conv(g_a, carry_ref[...]), v_a)
    v_b = _bdot(u_b, wv_ref[...])
    d_a = _bdot(act_a, wdn_ref[...])
    act_b = act(conv(g_b, g_a[half - CARRY_ROWS:, :]), v_b)
    carry_ref[...] = g_b[half - CARRY_ROWS:, :]
    d_b = _bdot(act_b, wdn_ref[...])
    post(rows_a, h_a, d_a)
    post(rows_b, h_b, d_b)

    @pl.when(t == nt - 1)
    def _():
        nb_ref[...] = carry_ref[CARRY_ROWS - 2:, :]


def _ffn_sample_kernel(h_ref, mod_ref, ng_ref, wg_ref, wv_ref, cw_ref, cb_ref, wd_ref, st_ref,
                       o_ref, nb_ref, wgb_ref, wvb_ref, wdb_ref, u_ref, acc_ref, *, nj):
    j = pl.program_id(0)
    mod = lambda i: _mod(mod_ref, i, True)

    @pl.when(j == 0)
    def _():
        u_ref[...] = (_rms(h_ref[...], ng_ref[2:3, :]) * (1.0 + mod(4)) + mod(3)).astype(BF16)
        acc_ref[...] = jnp.zeros_like(acc_ref)

    wg, wv, wd = wg_ref[...].astype(BF16), wv_ref[...].astype(BF16), wd_ref[...].astype(BF16)
    wgb_ref[...], wvb_ref[...], wdb_ref[...] = wg, wv, wd
    u = u_ref[...]
    g = _bdot(u, wg)
    val = _bdot(u, wv)
    cw = cw_ref[...]
    hc = st_ref[:, 0, :] * cw[0:1] + st_ref[:, 1, :] * cw[1:2] + g * cw[2:3]
    nb_ref[:, 0, :] = st_ref[:, 1, :]
    nb_ref[:, 1, :] = g
    acc_ref[...] += _bdot((_silu(hc + cb_ref[...]) * val).astype(BF16), wd)

    @pl.when(j == nj - 1)
    def _():
        o_ref[...] = h_ref[...] + mod(5) * _rms(acc_ref[...], ng_ref[3:4, :])


def _ffn_prompt_layer(h, mod, l, P, wts):
    nb, rows, _ = h.shape
    tm = 2 * HALF
    nt = rows // tm
    wg, wv, wd = wts
    return pl.pallas_call(
        functools.partial(_ffn_prompt_kernel, tm=tm, nt=nt),
        grid=(nb, nt),
        in_specs=[_rows(tm, D), _mod_spec(mod, l), _layer(P["norm_g"], l), _whole(wg.shape), _whole(wv.shape),
                  _layer(P["ffn_conv_w"], l), _layer(P["ffn_conv_b"], l), _whole(wd.shape)],
        out_specs=[_rows(tm, D), _per_seq(2, D_FF)],
        out_shape=[jax.ShapeDtypeStruct(h.shape, F32), jax.ShapeDtypeStruct((nb, 2, D_FF), F32)],
        scratch_shapes=[pltpu.VMEM((CARRY_ROWS, D_FF), F32)], compiler_params=_params(),
        name="ffn_prompt",
    )(h, mod, P["norm_g"], wg, wv, P["ffn_conv_w"], P["ffn_conv_b"], wd)


def _ffn_sample_layer(h, mod, l, P, state):
    _, rows, _ = h.shape
    c = FFN_CHUNK
    nj = D_FF // c
    w_up, w_down = P["ffn_w_up_f32"], P["ffn_w_down_f32"]
    bf = lambda shape: jax.ShapeDtypeStruct(shape, BF16)
    h_new, nbuf, wg, wv, wd = pl.pallas_call(
        functools.partial(_ffn_sample_kernel, nj=nj),
        grid=(nj,),
        in_specs=[pl.BlockSpec((None, rows, D), lambda j: (0, 0, 0)),
                  pl.BlockSpec((None, None) + tuple(mod.shape[2:]), lambda j: (l, 0, 0, 0)),
                  pl.BlockSpec((None, 4, D), lambda j: (l, 0, 0)),
                  pl.BlockSpec((None, D, c), lambda j: (l, 0, j)),
                  pl.BlockSpec((None, D, c), lambda j: (l, 0, nj + j)),
                  pl.BlockSpec((None, 3, c), lambda j: (l, 0, j)),
                  pl.BlockSpec((None, 1, c), lambda j: (l, 0, j)),
                  pl.BlockSpec((None, c, D), lambda j: (l, j, 0)),
                  pl.BlockSpec((None, rows, 2, c), lambda j: (l, 0, 0, j))],
        out_specs=[pl.BlockSpec((None, rows, D), lambda j: (0, 0, 0)),
                   pl.BlockSpec((rows, 2, c), lambda j: (0, 0, j)),
                   pl.BlockSpec((D, c), lambda j: (0, j)),
                   pl.BlockSpec((D, c), lambda j: (0, j)),
                   pl.BlockSpec((c, D), lambda j: (j, 0))],
        out_shape=[jax.ShapeDtypeStruct(h.shape, F32), jax.ShapeDtypeStruct((rows, 2, D_FF), F32),
                   bf((D, D_FF)), bf((D, D_FF)), bf((D_FF, D))],
        scratch_shapes=[pltpu.VMEM((rows, D), BF16), pltpu.VMEM((rows, D), F32)],
        compiler_params=_params(1),
        name="ffn_sample",
    )(h, mod, P["norm_g"], w_up, w_up, P["ffn_conv_w"], P["ffn_conv_b"], w_down, state)
    return h_new, nbuf, (wg, wv, wd)


def _rwkv_pre_kernel(*refs, sample, tm, nt):
    (h_ref, mod_ref, ng_ref, mu_ref, wrkv_ref, w0_ref, w1_ref, w2_ref, a0_ref, a1_ref, a2_ref,
     g1_ref, g2_ref, kk_ref, ka_ref, rk_ref, seg1_ref, seg2_ref) = refs[:18]
    if sample:
        shift_ref = refs[18]
        outs = refs[19:]
    else:
        tri_ref, sel_ref = refs[18:20]
        outs = refs[20:-1]
        carry_ref = refs[-1]
        t = pl.program_id(1)

        @pl.when(t == 0)
        def _():
            carry_ref[...] = jnp.zeros_like(carry_ref)

    at_ref, bt_ref, kt_ref, rt_ref, v_ref, g_ref, bonus_ref, wc_ref, sh_ref = outs
    mod = lambda i: _mod(mod_ref, i, sample)
    seg = lambda x: _segsum(x, seg1_ref[...], seg2_ref[...])
    tails = {}

    def half(i, rows):
        n = rows.stop - rows.start

        def put(ref, val):
            if sample:
                ref[...] = val.T
            else:
                ref[rows, :] = val

        u = _rms(h_ref[rows, :], ng_ref[0:1, :]) * (1.0 + mod(1)) + mod(0)
        if sample:
            prev = shift_ref[...]
            sh_ref[...] = u
        else:
            tails[i] = u[n - CARRY_ROWS:, :]
            prev = _shift_rows(u, 1, carry_ref[...] if i == 0 else tails[i - 1])
        ub, xb = u.astype(BF16), (prev - u).astype(BF16)
        mix = lambda m: ub + xb * mu_ref[m:m + 1, :].astype(BF16)
        r = _bdot(mix(0), wrkv_ref[0])
        k = _bdot(mix(2), wrkv_ref[1])
        v = _bdot(mix(3), wrkv_ref[2])
        zw = _bdot(mix(1), w1_ref[...])
        za = _bdot(mix(4), a1_ref[...])
        zg = _bdot(mix(5), g1_ref[...])
        yield
        put(v_ref, v)
        z = w0_ref[...] + _bdot(jnp.tanh(zw).astype(BF16), w2_ref[...])
        a = jax.nn.sigmoid(a0_ref[...] + _bdot(za.astype(BF16), a2_ref[...]))
        g_ref[rows, :] = _bdot(jax.nn.sigmoid(zg).astype(BF16), g2_ref[...])
        lw = -DECAY_SCALE * jax.nn.sigmoid(z)
        kk = k * kk_ref[...]
        k2 = k * (1.0 + (a - 1.0) * ka_ref[...])
        ss = seg(kk * kk)
        rk = seg(r * k2 * rk_ref[...])
        if sample:
            lc = lw
        else:
            lc = _dot01(tri_ref[...], lw, 2)
        yield
        kk = kk * jnp.minimum(lax.rsqrt(ss), 1e12)
        bonus_ref[rows, :] = rk * v
        if sample:
            wc_ref[...] = jnp.exp(lw).T
        else:
            wc_ref[i] = jnp.exp(_dot01(sel_ref[...], lc, 3))
        e_neg = jnp.exp(-lc)
        put(at_ref, -kk * jnp.exp(lc - lw))
        put(bt_ref, kk * a * e_neg)
        put(kt_ref, k2 * e_neg)
        put(rt_ref, (r * jnp.exp(lc)).astype(rt_ref.dtype))

    halves = _half_rows(tm, sample)
    _interleave([half(i, rows) for i, rows in enumerate(halves)])
    if not sample:
        carry_ref[...] = tails[len(halves) - 1]

        @pl.when(t == nt - 1)
        def _():
            sh_ref[...] = carry_ref[CARRY_ROWS - 1:, :]


def _rwkv_core_kernel(at_ref, bt_ref, kt_ref, rt_ref, v_ref, wc_ref, y_ref, sout_ref, s_ref, *, nt, nchunk):
    t = pl.program_id(1)

    @pl.when(t == 0)
    def _():
        s_ref[...] = jnp.zeros_like(s_ref)

    n = RWKV_HEAD
    lane_lo = lax.broadcasted_iota(jnp.int32, (1, LANES), 1) < n
    ri = lax.broadcasted_iota(jnp.int32, (CHUNK, LANES), 0)
    ci = lax.broadcasted_iota(jnp.int32, (CHUNK, LANES), 1) & (n - 1)
    m_strict = ci < ri
    m_incl = ci <= ri
    eye = (ci == ri).astype(F32)
    shift = n.bit_length() - 1
    m_bd = ((lax.broadcasted_iota(jnp.int32, (LANES, LANES), 0) >> shift)
            == (lax.broadcasted_iota(jnp.int32, (LANES, LANES), 1) >> shift))

    def bd(x):
        z = jnp.zeros_like(x)
        return _cat0(jnp.where(lane_lo, x, z), jnp.where(lane_lo, z, x))

    bd2 = lambda xs: (bd(xs[0]), bd(xs[1]))

    zero = jnp.zeros((LANES, LANES), BF16)
    halves = lambda w: w[:, :LANES] + w[:, LANES:]

    def rhs3(y):
        return _cat0(_cat1(y[0], y[1]), _cat1(y[0], zero))

    def rhs3_t(y):
        return _cat0(_cat1(y[0], y[0]), _cat1(y[1], zero))

    pairs = range(RWKV_HEADS // 2)
    lanes = [slice(p * LANES, (p + 1) * LANES) for p in pairs]
    npair = len(lanes)
    units = range(nchunk * npair)
    tok = [slice((i // npair) * CHUNK, (i // npair + 1) * CHUNK) for i in units]
    ln = [lanes[i % npair] for i in units]
    a_k = [_cat1(*_split2(at_ref[tok[i], ln[i]])) for i in units]
    b_s = [_split2(bt_ref[tok[i], ln[i]]) for i in units]
    k_s = [_split2(kt_ref[tok[i], ln[i]]) for i in units]
    v_s = [_split2(v_ref[tok[i], ln[i]]) for i in units]
    b_bd = [bd2(b_s[i]) for i in units]
    k_bd = [bd2(k_s[i]) for i in units]
    v_bd = [bd2(v_s[i]) for i in units]
    r_s = [rt_ref[tok[i], ln[i]] for i in units]
    bk_hi = [_cat0(b_s[i][0], k_s[i][0]) for i in units]
    a_ak = [_split2(jnp.where(m_strict, halves(_dot_nt(a_k[i], rhs3_t(k_bd[i]))), 0.0)) for i in units]
    sc_r = [_dot_nt(r_s[i], _cat0(b_bd[i][0], k_bd[i][0])) for i in units]
    a_r = [_cat1(jnp.where(m_incl, sc_r[i][:, :LANES], 0.0).astype(BF16),
                 jnp.where(m_incl, sc_r[i][:, LANES:], 0.0).astype(BF16)) for i in units]
    pw = [jnp.where(m_strict, halves(_dot_nt(a_k[i], rhs3_t(b_bd[i]))), 0.0) for i in units]
    tinv = [eye + pw[i] for i in units]
    pws = [_split2(pw[i]) for i in units]
    pw = [halves(_bdot(_cat1(*pws[i]), rhs3(bd2(pws[i])))) for i in units]
    for step in range(5):
        pws = [_split2(pw[i]) for i in units]
        tis = [_split2(tinv[i]) for i in units]
        rhs = [rhs3(bd2(pws[i])) for i in units]
        if step < 4:
            w = [_bdot(_cat0(_cat1(*pws[i]), _cat1(*tis[i])), rhs[i]) for i in units]
            pw = [halves(w[i][:CHUNK]) for i in units]
            tinv = [tinv[i] + halves(w[i][CHUNK:]) for i in units]
        else:
            tinv = [tinv[i] + halves(_bdot(_cat1(*tis[i]), rhs[i])) for i in units]
    tis = [_cat1(*_split2(tinv[i])) for i in units]
    s = [s_ref[p] for p in pairs]
    for c in range(nchunk):
        un = [c * npair + p for p in pairs]
        ss = [_split2(s[p]) for p in pairs]
        x = [halves(_dot_nt(a_k[i], rhs3_t(ss[p]))) + halves(_bdot(_cat1(*a_ak[i]), rhs3(v_bd[i])))
             for p, i in enumerate(un)]
        ub = [halves(_bdot(tis[i], rhs3(bd2(_split2(x[p]))))).astype(BF16) for p, i in enumerate(un)]
        for p, i in enumerate(un):
            y_ref[tok[i], lanes[p]] = (_dot_nt(r_s[i], ss[p][0])
                                       + _bdot(a_r[i], _cat0(bd(ub[p]), v_bd[i][0])))
        s = [jnp.where(m_bd, s[p] + _dot_tn(_cat0(ub[p], v_s[i][0]), bk_hi[i]), 0.0) * wc_ref[c][:, lanes[p]]
             for p, i in enumerate(un)]
    for p in pairs:
        s_ref[p] = s[p]

    @pl.when(t == nt - 1)
    def _():
        for p in pairs:
            sout_ref[2 * p] = s[p][:n, :n]
            sout_ref[2 * p + 1] = pltpu.roll(s[p], n, axis=1)[n:, :n]


def _rwkv_core_sample_kernel(at_ref, bt_ref, kt_ref, rt_ref, v_ref, wc_ref, s_ref, y_ref, sout_ref):
    s = s_ref[...]
    u = jnp.sum(s * at_ref[...][None, :, :], axis=1)
    s1 = s + u[:, None, :] * bt_ref[...][None, :, :] + v_ref[...][:, None, :] * kt_ref[...][None, :, :]
    y_ref[...] = jnp.sum(s1 * rt_ref[...][None, :, :], axis=1)
    sout_ref[...] = s1 * wc_ref[...][None, :, :]


def _rwkv_post_kernel(y_ref, g_ref, bonus_ref, h_ref, mod_ref, ng_ref, lnw_ref, lnb_ref, wo_ref,
                      seg1_ref, seg2_ref, o_ref, *, sample, tm):
    mod = lambda i: _mod(mod_ref, i, sample)
    seg = lambda x: _segsum(x, seg1_ref[...], seg2_ref[...])

    def half(rows):
        y = y_ref[rows, :]
        mean = seg(y) * (1.0 / RWKV_HEAD)
        yield
        yc = y - mean
        var = seg(yc * yc) * (1.0 / RWKV_HEAD)
        yield
        yn = yc * lax.rsqrt(var + RWKV_GN_EPS) * lnw_ref[...] + lnb_ref[...] + bonus_ref[rows, :]
        out = _bdot((yn * g_ref[rows, :]).astype(BF16), wo_ref[...])
        yield
        o_ref[rows, :] = h_ref[rows, :] + mod(2) * _rms(out, ng_ref[1:2, :])

    _interleave([half(rows) for rows in _half_rows(tm, sample)])


def _rwkv_layer(h, mod, l, j, P, state_shift, state_wkv, *, sample):
    nb, rows, _ = h.shape
    tm = rows if sample else 2 * HALF
    nt = rows // tm
    names = ("rwkv_mu", "rwkv_w_rkv", "rwkv_w0", "rwkv_w1", "rwkv_w2", "rwkv_a0", "rwkv_a1", "rwkv_a2",
             "rwkv_g1", "rwkv_g2", "rwkv_k_k", "rwkv_k_a", "rwkv_r_k")
    consts = [(P["norm_g"], l)] + [(P[n], j) for n in names]
    seg1, seg2p = P["seg1"], P["seg2p"]
    in_specs = ([_rows(tm, D), _mod_spec(mod, l)] + [_layer(a, i) for a, i in consts]
                + [_whole(seg1.shape), _whole(seg2p.shape)])
    args = [h, mod] + [a for a, _ in consts] + [seg1, seg2p]
    if sample:
        in_specs.append(_per_seq(rows, D))
        args.append(state_shift)
        act_shape, act_spec = (nb, D, rows), pl.BlockSpec((None, D, rows), lambda b, t: (b, 0, 0))
        wc_shape, wc_spec = act_shape, act_spec
        sh_shape, sh_spec = (nb, rows, D), _rows(tm, D)
        scratch = []
    else:
        ti = np.arange(HALF)
        tri = _np_bf16((ti[:, None] // CHUNK == ti[None, :] // CHUNK) & (ti[None, :] <= ti[:, None]))
        sel = _np_bf16(ti[None, :] == (np.arange(CARRY_ROWS)[:, None] * CHUNK + CHUNK - 1))
        in_specs += [_whole(tri.shape), _whole(sel.shape)]
        args += [tri, sel]
        act_shape, act_spec = h.shape, _rows(tm, D)
        wc_shape = (nb, rows // HALF, CARRY_ROWS, D)
        wc_spec = pl.BlockSpec((None, tm // HALF, CARRY_ROWS, D), lambda b, t: (b, t, 0, 0))
        sh_shape, sh_spec = (nb, 1, D), _per_seq(1, D)
        scratch = [pltpu.VMEM((CARRY_ROWS, D), F32)]
    f32 = jax.ShapeDtypeStruct(h.shape, F32)
    act = jax.ShapeDtypeStruct(act_shape, F32)
    rt_shape = act if sample else jax.ShapeDtypeStruct(h.shape, BF16)
    at, bt, kt, rt, vv, g, bonus, wc, new_shift = pl.pallas_call(
        functools.partial(_rwkv_pre_kernel, sample=sample, tm=tm, nt=nt),
        grid=(nb, nt), in_specs=in_specs,
        out_specs=[act_spec] * 5 + [_rows(tm, D)] * 2 + [wc_spec, sh_spec],
        out_shape=[act, act, act, rt_shape, act, f32, f32, jax.ShapeDtypeStruct(wc_shape, F32),
                   jax.ShapeDtypeStruct(sh_shape, F32)],
        scratch_shapes=scratch, compiler_params=_params(),
        name="rwkv_pre_sample" if sample else "rwkv_pre_prompt",
    )(*args)

    hh, n = RWKV_HEADS, RWKV_HEAD
    if sample:
        hv = lambda x: x.reshape(hh, n, rows)
        vec_spec = pl.BlockSpec((None, n, rows), lambda i: (i, 0, 0))
        st_spec = pl.BlockSpec((None, n, n, rows), lambda i: (i, 0, 0, 0))
        y_t, s_t = pl.pallas_call(
            _rwkv_core_sample_kernel,
            grid=(hh,),
            in_specs=[vec_spec] * 6 + [st_spec],
            out_specs=[vec_spec, st_spec],
            out_shape=[jax.ShapeDtypeStruct((hh, n, rows), F32), jax.ShapeDtypeStruct((hh, n, n, rows), F32)],
            compiler_params=_params(1),
            name="rwkv_core_sample",
        )(hv(at), hv(bt), hv(kt), hv(rt), hv(vv), hv(wc), jnp.transpose(state_wkv, (1, 2, 3, 0)))
        y = y_t.reshape(D, rows).T.reshape(nb, rows, D)
        new_s = jnp.transpose(s_t, (3, 0, 1, 2))
    else:
        ncht = rows // CHUNK
        wc_rows = wc[:, :, :HALF // CHUNK, :].reshape(nb, ncht, 1, D)
        npair = hh // 2
        nc = RWKV_CHUNKS_PER_STEP
        y, new_s = pl.pallas_call(
            functools.partial(_rwkv_core_kernel, nt=ncht // nc, nchunk=nc),
            grid=(nb, ncht // nc),
            in_specs=[_rows(nc * CHUNK, D)] * 5 + [pl.BlockSpec((None, nc, 1, D), lambda b, t: (b, t, 0, 0))],
            out_specs=[_rows(nc * CHUNK, D), pl.BlockSpec((None, hh, n, n), lambda b, t: (b, 0, 0, 0))],
            out_shape=[f32, jax.ShapeDtypeStruct((nb, hh, n, n), F32)],
            scratch_shapes=[pltpu.VMEM((npair, LANES, LANES), F32)],
            compiler_params=_params(),
            name="rwkv_core_prompt",
        )(at, bt, kt, rt, vv, wc_rows)

    consts = [(P["norm_g"], l), (P["rwkv_lnx_w"], j), (P["rwkv_lnx_b"], j), (P["rwkv_w_o"], j)]
    h_new = pl.pallas_call(
        functools.partial(_rwkv_post_kernel, sample=sample, tm=tm),
        grid=(nb, nt),
        in_specs=([_rows(tm, D)] * 4 + [_mod_spec(mod, l)] + [_layer(a, i) for a, i in consts]
                  + [_whole(seg1.shape), _whole(seg2p.shape)]),
        out_specs=_rows(tm, D), out_shape=f32, compiler_params=_params(),
        name="rwkv_post_sample" if sample else "rwkv_post_prompt",
    )(y, g, bonus, h, mod, *[a for a, _ in consts], seg1, seg2p)
    return h_new, new_shift, new_s


GLA_LEVELS = (1, 2, 4, 8, 16, 32)


def _gla_pre_kernel(h_ref, mod_ref, ng_ref, win_ref, wa1_ref, wa2_ref, ba_ref,
                    q_ref, k_ref, v_ref, og_ref, lg_ref, *, sample, tm):
    mod = lambda i: _mod(mod_ref, i, sample)
    dk, dv = GLA_DK_TOTAL, GLA_DV_TOTAL

    def half(rows):
        u = (_rms(h_ref[rows, :], ng_ref[0:1, :]) * (1.0 + mod(1)) + mod(0)).astype(BF16)
        p = _bdot(u, win_ref[...])
        gate = _bdot(_bdot(u, wa1_ref[...]).astype(BF16), wa2_ref[...]) + ba_ref[...]
        yield
        q_ref[rows, :] = p[:, :dk] * (GLA_DK ** -0.5)
        k_ref[rows, :] = p[:, dk:2 * dk]
        v_ref[rows, :] = p[:, 2 * dk:2 * dk + dv]
        og_ref[rows, :] = p[:, 2 * dk + dv:]
        lg_ref[rows, :] = -_softplus(-gate) * (1.0 / GLA_GATE_NORMALIZER)

    _interleave([half(rows) for rows in _half_rows(tm, sample)])


def _gla_core_kernel(q_ref, k_ref, v_ref, lg_ref, lvl_ref, o_ref, sout_ref, s_ref, *, nt, nchunk):
    t = pl.program_id(1)

    @pl.when(t == 0)
    def _():
        s_ref[...] = jnp.zeros_like(s_ref)

    row = lax.broadcasted_iota(jnp.int32, (CHUNK, 1), 0)
    ri = lax.broadcasted_iota(jnp.int32, (CHUNK, CHUNK), 0)
    ci = lax.broadcasted_iota(jnp.int32, (CHUNK, CHUNK), 1)
    heads = range(GLA_HEADS)
    units = range(nchunk * GLA_HEADS)
    tok = [slice((i // GLA_HEADS) * CHUNK, (i // GLA_HEADS + 1) * CHUNK) for i in units]
    sl = [slice((i % GLA_HEADS) * GLA_DK, (i % GLA_HEADS + 1) * GLA_DK) for i in units]
    sv = [slice((i % GLA_HEADS) * GLA_DV, (i % GLA_HEADS + 1) * GLA_DV) for i in units]
    lg = [lg_ref[c * CHUNK:(c + 1) * CHUNK, :] for c in range(nchunk)]
    pq = [_bdot(lvl_ref[...], _cat0(*_split3(lg[c]))) for c in range(nchunk)]
    q = [q_ref[tok[i], sl[i]] for i in units]
    k = [k_ref[tok[i], sl[i]] for i in units]
    vb = [v_ref[tok[i], sv[i]].astype(BF16) for i in units]
    a = [jnp.where(ri == ci, _dot_nt(q[i].astype(BF16), k[i].astype(BF16)), 0.0) for i in units]
    for li, m in enumerate(GLA_LEVELS):
        lm = m.bit_length() - 1
        second = ((row >> lm) & 1) == 1
        block = (ri >> (lm + 1)) == (ci >> (lm + 1))
        base = (li - 1) * 2 * CHUNK
        for i in units:
            c = i // GLA_HEADS
            if m == 1:
                qe = jnp.where(second, q[i] * jnp.exp(lg[c][:, sl[i]]), 0.0)
                ke = jnp.where(second, 0.0, k[i])
            else:
                qe = jnp.where(second, q[i] * jnp.exp(pq[c][base:base + CHUNK, sl[i]]), 0.0)
                ke = jnp.where(second, 0.0, k[i] * jnp.exp(pq[c][base + CHUNK:base + 2 * CHUNK, sl[i]]))
            a[i] = a[i] + jnp.where(block, _dot_nt(qe.astype(BF16), ke.astype(BF16)), 0.0)
    base = (len(GLA_LEVELS) - 1) * 2 * CHUNK
    g_inc = [pq[i // GLA_HEADS][base:base + CHUNK, sl[i]] for i in units]
    g_rest = [pq[i // GLA_HEADS][base + CHUNK:base + 2 * CHUNK, sl[i]] for i in units]
    av = [_bdot(a[i].astype(BF16), vb[i]) for i in units]
    qg = [(q[i] * jnp.exp(g_inc[i])).astype(BF16) for i in units]
    kg = [(k[i] * jnp.exp(g_rest[i])).astype(BF16) for i in units]
    kv = [_dot_tn(vb[i], kg[i]) for i in units]
    s = [s_ref[hd] for hd in heads]
    for c in range(nchunk):
        for hd in heads:
            i = c * GLA_HEADS + hd
            o_ref[tok[i], sv[i]] = av[i] + _dot_nt(qg[i], s[hd].astype(BF16))
        s = [s[hd] * jnp.exp(g_inc[c * GLA_HEADS + hd][CHUNK - 1:CHUNK, :]) + kv[c * GLA_HEADS + hd]
             for hd in heads]
    for hd in heads:
        s_ref[hd] = s[hd]

    @pl.when(t == nt - 1)
    def _():
        sout_ref[...] = s_ref[...]


def _gla_core_sample_kernel(q_ref, k_ref, v_ref, lg_ref, s_ref, o_ref, sout_ref):
    n = GLA_DK
    eye = (lax.broadcasted_iota(jnp.int32, (n, n), 0) == lax.broadcasted_iota(jnp.int32, (n, n), 1)).astype(F32)
    col = lambda x: jnp.sum(eye * x, axis=-1, keepdims=True)
    q, k, v = q_ref[...], k_ref[...], v_ref[...]
    s = s_ref[...]
    decay = jnp.exp(lg_ref[...])
    qk = jnp.sum(q * k, axis=-1, keepdims=True)
    o_ref[...] = qk * v + jnp.sum(col(q * decay) * s, axis=-2, keepdims=True)
    sout_ref[...] = col(decay) * s + col(k) * v


def _gla_post_kernel(o_ref, og_ref, h_ref, mod_ref, ng_ref, on_ref, wo_ref, out_ref, *, sample):
    mod = lambda i: _mod(mod_ref, i, sample)
    og = og_ref[...]
    parts = []
    for hd in range(GLA_HEADS):
        sv = slice(hd * GLA_DV, (hd + 1) * GLA_DV)
        parts.append((_rms(o_ref[:, sv], on_ref[...]) * _silu(og[:, sv])).astype(BF16))
    out = _bdot(_cat1(*parts), wo_ref[...])
    out_ref[...] = h_ref[...] + mod(2) * _rms(out, ng_ref[1:2, :])


def _gla_level_matrix():
    ti = np.arange(CHUNK)
    blocks = []
    for m in GLA_LEVELS[1:] + (CHUNK,):
        same = ti[:, None] // m == ti[None, :] // m
        blocks.append(same & (ti[None, :] <= ti[:, None]))
        blocks.append(same & (ti[None, :] > ti[:, None]))
    one = np.concatenate(blocks, axis=0)
    return _np_bf16(np.concatenate([one, one, one], axis=1))


def _gla_fused_kernel(h_ref, mod_ref, ng_ref, win_ref, wa1_ref, wa2_ref, ba_ref, on_ref, wo_ref, lvl_ref,
                      out_ref, sout_ref, q_s, k_s, v_s, og_s, lg_s, o_s, s_ref, *, tm, nt):
    _gla_pre_kernel(h_ref, mod_ref, ng_ref, win_ref, wa1_ref, wa2_ref, ba_ref,
                    q_s, k_s, v_s, og_s, lg_s, sample=False, tm=tm)
    _gla_core_kernel(q_s, k_s, v_s, lg_s, lvl_ref, o_s, sout_ref, s_ref, nt=nt, nchunk=tm // CHUNK)
    _gla_post_kernel(o_s, og_s, h_ref, mod_ref, ng_ref, on_ref, wo_ref, out_ref, sample=False)


def _gla_prompt_layer(h, mod, l, j, P):
    nb, rows, _ = h.shape
    tm = 2 * HALF
    nt = rows // tm
    dk, dv, hh = GLA_DK_TOTAL, GLA_DV_TOTAL, GLA_HEADS
    lvl = _gla_level_matrix()
    consts = [(P["norm_g"], l), (P["gla_w_in"], j), (P["gla_wa1"], j), (P["gla_wa2"], j), (P["gla_ba"], j),
              (P["gla_onorm_g"], j), (P["gla_w_o"], j)]
    h_new, s_t = pl.pallas_call(
        functools.partial(_gla_fused_kernel, tm=tm, nt=nt),
        grid=(nb, nt),
        in_specs=[_rows(tm, D), _mod_spec(mod, l)] + [_layer(a, i) for a, i in consts] + [_whole(lvl.shape)],
        out_specs=[_rows(tm, D), pl.BlockSpec((None, hh, GLA_DV, GLA_DK), lambda b, t: (b, 0, 0, 0))],
        out_shape=[jax.ShapeDtypeStruct(h.shape, F32), jax.ShapeDtypeStruct((nb, hh, GLA_DV, GLA_DK), F32)],
        scratch_shapes=[pltpu.VMEM((tm, dk), F32), pltpu.VMEM((tm, dk), F32), pltpu.VMEM((tm, dv), F32),
                        pltpu.VMEM((tm, dv), F32), pltpu.VMEM((tm, dk), F32), pltpu.VMEM((tm, dv), F32),
                        pltpu.VMEM((hh, GLA_DV, GLA_DK), F32)],
        compiler_params=_params(),
        name="gla_prompt",
    )(h, mod, *[a for a, _ in consts], lvl)
    return h_new, jnp.swapaxes(s_t, -1, -2)


def _gla_layer(h, mod, l, j, P, state, *, sample):
    if not sample:
        return _gla_prompt_layer(h, mod, l, j, P)
    nb, rows, _ = h.shape
    tm = rows
    nt = rows // tm
    dk, dv, hh = GLA_DK_TOTAL, GLA_DV_TOTAL, GLA_HEADS
    consts = [(P["norm_g"], l), (P["gla_w_in"], j), (P["gla_wa1"], j), (P["gla_wa2"], j), (P["gla_ba"], j)]
    shp = lambda width: jax.ShapeDtypeStruct((nb, rows, width), F32)
    q, k, v, og, lg = pl.pallas_call(
        functools.partial(_gla_pre_kernel, sample=sample, tm=tm),
        grid=(nb, nt),
        in_specs=[_rows(tm, D), _mod_spec(mod, l)] + [_layer(a, i) for a, i in consts],
        out_specs=[_rows(tm, dk), _rows(tm, dk), _rows(tm, dv), _rows(tm, dv), _rows(tm, dk)],
        out_shape=[shp(dk), shp(dk), shp(dv), shp(dv), shp(dk)],
        compiler_params=_params(),
        name="gla_pre_sample" if sample else "gla_pre_prompt",
    )(h, mod, *[a for a, _ in consts])

    if sample:
        sb = 8
        nseq = rows
        hk = lambda x: x.reshape(nseq, hh, 1, GLA_DK)
        kspec = pl.BlockSpec((sb, hh, 1, GLA_DK), lambda i: (i, 0, 0, 0))
        vspec = pl.BlockSpec((sb, hh, 1, GLA_DV), lambda i: (i, 0, 0, 0))
        sspec = pl.BlockSpec((sb, hh, GLA_DK, GLA_DV), lambda i: (i, 0, 0, 0))
        o, new_s = pl.pallas_call(
            _gla_core_sample_kernel,
            grid=(nseq // sb,),
            in_specs=[kspec, kspec, vspec, kspec, sspec],
            out_specs=[vspec, sspec],
            out_shape=[jax.ShapeDtypeStruct((nseq, hh, 1, GLA_DV), F32),
                       jax.ShapeDtypeStruct((nseq, hh, GLA_DK, GLA_DV), F32)],
            compiler_params=_params(1),
            name="gla_core_sample",
        )(hk(q), hk(k), v.reshape(nseq, hh, 1, GLA_DV), hk(lg), state)
        o = o.reshape(nb, rows, dv)

    tp = rows
    consts = [(P["norm_g"], l), (P["gla_onorm_g"], j), (P["gla_w_o"], j)]
    h_new = pl.pallas_call(
        functools.partial(_gla_post_kernel, sample=sample),
        grid=(nb, rows // tp),
        in_specs=[_rows(tp, dv), _rows(tp, dv), _rows(tp, D), _mod_spec(mod, l)] + [_layer(a, i) for a, i in consts],
        out_specs=_rows(tp, D), out_shape=shp(D), compiler_params=_params(),
        name="gla_post_sample" if sample else "gla_post_prompt",
    )(o, og, h, mod, *[a for a, _ in consts])
    return h_new, new_s


def _trunk(h, mod, P, states, ffn_wts, *, sample):
    st_a, st_sb, st_wb, st_gc, st_f = states
    nb, rows, _ = h.shape
    new_a, new_sb, new_wb, new_gc, new_f = [], [], [], [], []
    for l in range(DEPTH):
        kind, j = l % 3, l // 3
        if kind == 0:
            h, nbuf = _sconv_layer(h, mod, l, j, P, st_a, sample=sample)
            new_a.append(nbuf)
        elif kind == 1:
            sh = st_sb[j].reshape(1, rows, D) if sample else None
            wkv = st_wb[j] if sample else None
            h, nsh, ns = _rwkv_layer(h, mod, l, j, P, sh, wkv, sample=sample)
            new_sb.append(nsh.reshape(rows, D) if sample else nsh.reshape(nb, D))
            new_wb.append(ns)
        else:
            h, ns = _gla_layer(h, mod, l, j, P, st_gc[j] if sample else None, sample=sample)
            new_gc.append(ns)
        if sample:
            h, nbuf, wts = _ffn_sample_layer(h, mod, l, P, st_f)
            ffn_wts.append(wts)
        else:
            h, nbuf = _ffn_prompt_layer(h, mod, l, P, ffn_wts[l])
        new_f.append(nbuf)
    return (h, jnp.stack(new_a), jnp.stack(new_sb), jnp.stack(new_wb), jnp.stack(new_gc), jnp.stack(new_f))


def _prepare(p):
    bf = lambda x: x.astype(BF16)
    row = lambda x: x.reshape(x.shape[0], 1, -1)
    pad_c = lambda x: bf(jnp.pad(x, ((0, 0), (0, 0), (0, LANES - x.shape[2]))))
    pad_r = lambda x: bf(jnp.pad(x, ((0, 0), (0, LANES - x.shape[1]), (0, 0))))
    lane_head = np.arange(D) // RWKV_HEAD
    col = np.arange(LANES)
    P = dict(
        norm_g=p["norm_g"],
        sconv_w_in=bf(p["sconv_w_in"]), sconv_conv_w=p["sconv_conv_w"], sconv_w_out=bf(p["sconv_w_out"]),
        rwkv_mu=p["rwkv_mu"], rwkv_w_rkv=bf(p["rwkv_w_rkv"]), rwkv_w0=row(p["rwkv_w0"]),
        rwkv_w1=pad_c(p["rwkv_w1"]), rwkv_w2=pad_r(p["rwkv_w2"]), rwkv_a0=row(p["rwkv_a0"]),
        rwkv_a1=pad_c(p["rwkv_a1"]), rwkv_a2=pad_r(p["rwkv_a2"]), rwkv_g1=bf(p["rwkv_g1"]), rwkv_g2=bf(p["rwkv_g2"]),
        rwkv_k_k=row(p["rwkv_k_k"]), rwkv_k_a=row(p["rwkv_k_a"]), rwkv_r_k=row(p["rwkv_r_k"]),
        rwkv_lnx_w=row(p["rwkv_lnx_w"]), rwkv_lnx_b=row(p["rwkv_lnx_b"]), rwkv_w_o=bf(p["rwkv_w_o"]),
        gla_w_in=bf(p["gla_w_in"]), gla_wa1=pad_c(p["gla_wa1"]), gla_wa2=pad_r(p["gla_wa2"]),
        gla_ba=row(p["gla_ba"]), gla_onorm_g=row(p["gla_onorm_g"]), gla_w_o=bf(p["gla_w_o"]),
        ffn_w_up_f32=p["ffn_w_up"], ffn_conv_w=p["ffn_conv_w"], ffn_conv_b=row(p["ffn_conv_b"]),
        ffn_w_down_f32=p["ffn_w_down"],
        seg1=_np_bf16(lane_head[:, None] == col[None, :]),
        seg2p=_np_bf16((col[:, None] % SEG_STRIDE == lane_head[None, :]) & (col[:, None] < 3 * SEG_STRIDE)),
    )
    return P


def kernel(x_prompt, x_sample, state_conv_a, state_shift_b, state_wkv_b, state_gla_c, state_conv_ffn, c_prompt, c_sample, ada_w, ada_b, norm_g, sconv_w_in, sconv_conv_w, sconv_w_out, rwkv_mu, rwkv_w_rkv, rwkv_w0, rwkv_w1, rwkv_w2, rwkv_a0, rwkv_a1, rwkv_a2, rwkv_g1, rwkv_g2, rwkv_k_k, rwkv_k_a, rwkv_r_k, rwkv_lnx_w, rwkv_lnx_b, rwkv_w_o, gla_w_in, gla_wa1, gla_wa2, gla_ba, gla_onorm_g, gla_w_o, ffn_w_up, ffn_conv_w, ffn_conv_b, ffn_w_down):
    P = _prepare(dict(
        norm_g=norm_g, sconv_w_in=sconv_w_in, sconv_conv_w=sconv_conv_w, sconv_w_out=sconv_w_out,
        rwkv_mu=rwkv_mu, rwkv_w_rkv=rwkv_w_rkv, rwkv_w0=rwkv_w0, rwkv_w1=rwkv_w1, rwkv_w2=rwkv_w2,
        rwkv_a0=rwkv_a0, rwkv_a1=rwkv_a1, rwkv_a2=rwkv_a2, rwkv_g1=rwkv_g1, rwkv_g2=rwkv_g2,
        rwkv_k_k=rwkv_k_k, rwkv_k_a=rwkv_k_a, rwkv_r_k=rwkv_r_k, rwkv_lnx_w=rwkv_lnx_w,
        rwkv_lnx_b=rwkv_lnx_b, rwkv_w_o=rwkv_w_o,
        gla_w_in=gla_w_in, gla_wa1=gla_wa1, gla_wa2=gla_wa2, gla_ba=gla_ba,
        gla_onorm_g=gla_onorm_g, gla_w_o=gla_w_o,
        ffn_w_up=ffn_w_up, ffn_conv_w=ffn_conv_w, ffn_conv_b=ffn_conv_b, ffn_w_down=ffn_w_down))
    bp, bs = x_prompt.shape[0], x_sample.shape[0]
    mod_p, mod_s = _modulation(c_prompt, c_sample, ada_w, ada_b)
    mod_p = mod_p.reshape(DEPTH, bp, N_MOD, D)
    mod_s = mod_s.reshape(DEPTH, 1, bs, N_MOD * D)
    ffn_wts = []
    y_s, ca_s, sb_s, wb_s, gc_s, cf_s = _trunk(
        x_sample.reshape(1, bs, D), mod_s, P,
        (state_conv_a, state_shift_b, state_wkv_b, state_gla_c, state_conv_ffn), ffn_wts, sample=True)
    y_p, ca_p, sb_p, wb_p, gc_p, cf_p = _trunk(x_prompt, mod_p, P, (None,) * 5, ffn_wts, sample=False)
    return (y_p, y_s.reshape(bs, 1, D), ca_p, ca_s, sb_p, sb_s, wb_p, wb_s, gc_p, gc_s, cf_p, cf_s)
```

```python
import functools

import jax
import jax.numpy as jnp
import numpy as np
from jax import lax
from jax.experimental import pallas as pl
from jax.experimental.pallas import tpu as pltpu

F32, BF16 = jnp.float32, jnp.bfloat16

D = 1024
DEPTH = 4
N_MOD = 6
RMS_EPS = 1e-6
RWKV_HEADS, RWKV_HEAD = 16, 64
RWKV_GN_EPS = 64e-5
DECAY_SCALE = float(np.exp(-0.5))
GLA_HEADS, GLA_DK, GLA_DV = 4, 128, 256
GLA_DK_TOTAL, GLA_DV_TOTAL = 512, 1024
GLA_GATE_NORMALIZER = 16.0
D_FF = 2816

LANES = 128
CARRY_ROWS = 8
CHUNK = 64
HALF = 256
FFN_CHUNK = 256
RWKV_CHUNKS_PER_STEP = 8
SEG_STRIDE = 16
VMEM_LIMIT = 56 * 1024 * 1024


def _rms(x, g):
    return x * lax.rsqrt(jnp.mean(x * x, axis=-1, keepdims=True) + RMS_EPS) * g


def _silu(x):
    return x * jax.nn.sigmoid(x)


def _softplus(x):
    return jnp.maximum(x, 0.0) + jnp.log1p(jnp.exp(-jnp.abs(x)))


_bdot = functools.partial(jnp.dot, preferred_element_type=F32)


def _dot_nt(a, b):
    return lax.dot_general(a, b, (((1,), (1,)), ((), ())), preferred_element_type=F32)


def _dot_tn(a, b):
    return lax.dot_general(a, b, (((0,), (0,)), ((), ())), preferred_element_type=F32)


def _cat0(*xs):
    return jnp.concatenate(xs, axis=0)


def _cat1(*xs):
    return jnp.concatenate(xs, axis=1)


def _split2(x):
    hi = x.astype(BF16)
    return hi, (x - hi.astype(F32)).astype(BF16)


def _split3(x):
    hi = x.astype(BF16)
    r1 = x - hi.astype(F32)
    mid = r1.astype(BF16)
    return hi, mid, (r1 - mid.astype(F32)).astype(BF16)


def _dot01(m01, x, terms):
    parts = (_split2 if terms == 2 else _split3)(x)
    acc = _bdot(m01, parts[0])
    for part in parts[1:]:
        acc = acc + _bdot(m01, part)
    return acc


def _segsum(x, seg1, seg2p):
    hi, lo = _split2(x)
    s = _bdot(hi, seg1) + _bdot(lo, seg1)
    s_hi, s_mid, s_lo = (part.astype(F32) for part in _split3(s))
    packed = s_hi + pltpu.roll(s_mid, SEG_STRIDE, axis=1) + pltpu.roll(s_lo, 2 * SEG_STRIDE, axis=1)
    return _bdot(packed.astype(BF16), seg2p)


def _mod(mod_ref, i, sample):
    return mod_ref[:, i * D:(i + 1) * D] if sample else mod_ref[i:i + 1, :]


def _shift_rows(x, k, carry):
    row = lax.broadcasted_iota(jnp.int32, (x.shape[0], 1), 0)
    y = pltpu.roll(x, k, axis=0)
    for j in range(k):
        src = CARRY_ROWS - k + j
        y = jnp.where(row == j, carry[src:src + 1, :], y)
    return y


def _half_rows(tm, sample):
    return [slice(0, tm)] if sample else [slice(0, tm // 2), slice(tm // 2, tm)]


def _interleave(gens):
    live = list(gens)
    while live:
        for g in list(live):
            if next(g, StopIteration) is StopIteration:
                live.remove(g)


def _whole(shape):
    nd = len(shape)
    return pl.BlockSpec(tuple(shape), lambda b, t: (0,) * nd, pipeline_mode=pl.Buffered(1))


def _layer(arr, l):
    nd = arr.ndim - 1
    return pl.BlockSpec((None,) + tuple(arr.shape[1:]), lambda b, t: (l,) + (0,) * nd,
                        pipeline_mode=pl.Buffered(1))


def _mod_spec(mod, l):
    return pl.BlockSpec((None, None) + tuple(mod.shape[2:]), lambda b, t: (l, b, 0, 0))


def _state_spec(state, l):
    return pl.BlockSpec((None,) + tuple(state.shape[1:]), lambda b, t: (l, 0, 0, 0))


def _new_state_spec(rows, width):
    return pl.BlockSpec((rows, 2, width), lambda b, t: (0, 0, 0))


def _rows(tm, width):
    return pl.BlockSpec((None, tm, width), lambda b, t: (b, t, 0))


def _per_seq(r, width):
    return pl.BlockSpec((None, r, width), lambda b, t: (b, 0, 0))


def _params(n=2):
    return pltpu.CompilerParams(dimension_semantics=("arbitrary",) * n, vmem_limit_bytes=VMEM_LIMIT)


def _np_bf16(a):
    return jnp.asarray(np.asarray(a, np.float32), BF16)


def _mod_kernel(cp_ref, cs_ref, w_ref, b_ref, op_ref, os_ref):
    w = w_ref[...].astype(BF16)
    op_ref[...] = _bdot(_silu(cp_ref[...]).astype(BF16), w) + b_ref[...]
    os_ref[...] = _bdot(_silu(cs_ref[...]).astype(BF16), w) + b_ref[...]


def _modulation(c_prompt, c_sample, ada_w, ada_b):
    bp, bs = c_prompt.shape[0], c_sample.shape[0]
    tn = 3072
    out = lambda n: (pl.BlockSpec((None, n, tn), lambda l, j: (l, 0, j)),
                     jax.ShapeDtypeStruct((DEPTH, n, N_MOD * D), F32))
    (sp_p, sh_p), (sp_s, sh_s) = out(bp), out(bs)
    return pl.pallas_call(
        _mod_kernel,
        grid=(DEPTH, N_MOD * D // tn),
        in_specs=[pl.BlockSpec((bp, D), lambda l, j: (0, 0)),
                  pl.BlockSpec((bs, D), lambda l, j: (0, 0)),
                  pl.BlockSpec((None, D, tn), lambda l, j: (l, 0, j)),
                  pl.BlockSpec((None, 1, tn), lambda l, j: (l, 0, j))],
        out_specs=[sp_p, sp_s], out_shape=[sh_p, sh_s],
        compiler_params=_params(),
        name="adaln_mod",
    )(c_prompt, c_sample, ada_w, ada_b.reshape(DEPTH, 1, N_MOD * D))


def _sconv_kernel(*refs, sample, tm, nt):
    if sample:
        h_ref, mod_ref, ng_ref, win_ref, cw_ref, wout_ref, st_ref, o_ref, nb_ref = refs
    else:
        h_ref, mod_ref, ng_ref, win_ref, cw_ref, wout_ref, o_ref, nb_ref, carry_ref = refs
        t = pl.program_id(1)

        @pl.when(t == 0)
        def _():
            carry_ref[...] = jnp.zeros_like(carry_ref)

    mod = lambda i: _mod(mod_ref, i, sample)
    cw = cw_ref[...]
    tails = {}

    def half(i, rows):
        n = rows.stop - rows.start
        h = h_ref[rows, :]
        u = (_rms(h, ng_ref[0:1, :]) * (1.0 + mod(1)) + mod(0)).astype(BF16)
        p = _bdot(u, win_ref[...])
        yield
        bg = p[:, :D]
        z = p[:, D:2 * D] * p[:, 2 * D:]
        if sample:
            y = st_ref[:, 0, :] * cw[0:1] + st_ref[:, 1, :] * cw[1:2] + z * cw[2:3]
            nb_ref[:, 0, :] = st_ref[:, 1, :]
            nb_ref[:, 1, :] = z
        else:
            c = carry_ref[...] if i == 0 else tails[i - 1]
            tails[i] = z[n - CARRY_ROWS:, :]
            y = _shift_rows(z, 2, c) * cw[0:1] + _shift_rows(z, 1, c) * cw[1:2] + z * cw[2:3]
        out = _bdot((bg * y).astype(BF16), wout_ref[...])
        yield
        o_ref[rows, :] = h + mod(2) * _rms(out, ng_ref[1:2, :])

    halves = _half_rows(tm, sample)
    _interleave([half(i, rows) for i, rows in enumerate(halves)])
    if not sample:
        carry_ref[...] = tails[len(halves) - 1]

        @pl.when(t == nt - 1)
        def _():
            nb_ref[...] = carry_ref[CARRY_ROWS - 2:, :]


def _sconv_layer(h, mod, l, j, P, state, *, sample):
    nb, rows, _ = h.shape
    tm = rows if sample else 4 * HALF
    nt = rows // tm
    consts = [(P["norm_g"], l), (P["sconv_w_in"], j), (P["sconv_conv_w"], j), (P["sconv_w_out"], j)]
    in_specs = [_rows(tm, D), _mod_spec(mod, l)] + [_layer(a, i) for a, i in consts]
    args = [h, mod] + [a for a, _ in consts]
    if sample:
        in_specs.append(_state_spec(state, j))
        args.append(state)
        nb_shape, nb_spec, scratch = (rows, 2, D), _new_state_spec(rows, D), []
    else:
        nb_shape, nb_spec = (nb, 2, D), _per_seq(2, D)
        scratch = [pltpu.VMEM((CARRY_ROWS, D), F32)]
    return pl.pallas_call(
        functools.partial(_sconv_kernel, sample=sample, tm=tm, nt=nt),
        grid=(nb, nt), in_specs=in_specs,
        out_specs=[_rows(tm, D), nb_spec],
        out_shape=[jax.ShapeDtypeStruct(h.shape, F32), jax.ShapeDtypeStruct(nb_shape, F32)],
        scratch_shapes=scratch, compiler_params=_params(),
        name="sconv_sample" if sample else "sconv_prompt",
    )(*args)


def _ffn_prompt_kernel(h_ref, mod_ref, ng_ref, wg_ref, wv_ref, cw_ref, cb_ref, wdn_ref, o_ref, nb_ref, carry_ref,
                       *, tm, nt):
    mod = lambda i: _mod(mod_ref, i, False)
    cw = cw_ref[...]
    t = pl.program_id(1)

    @pl.when(t == 0)
    def _():
        carry_ref[...] = jnp.zeros_like(carry_ref)

    def pre(rows):
        h = h_ref[rows, :]
        return h, (_rms(h, ng_ref[2:3, :]) * (1.0 + mod(4)) + mod(3)).astype(BF16)

    def act(hc, val):
        return (_silu(hc + cb_ref[...]) * val).astype(BF16)

    def post(rows, h, out):
        o_ref[rows, :] = h + mod(5) * _rms(out, ng_ref[3:4, :])

    def conv(g, c):
        return _shift_rows(g, 2, c) * cw[0:1] + _shift_rows(g, 1, c) * cw[1:2] + g * cw[2:3]

    half = tm // 2
    rows_a, rows_b = slice(0, half), slice(half, tm)
    h_a, u_a = pre(rows_a)
    g_a = _bdot(u_a, wg_ref[...])
    h_b, u_b = pre(rows_b)
    v_a = _bdot(u_a, wv_ref[...])
    g_b = _bdot(u_b, wg_ref[...])
    act_a = act(conv(g_a, carry_ref[...]), v_a)
    v_b = _bdot(u_b, wv_ref[...])
    d_a = _bdot(act_a, wdn_ref[...])
    act_b = act(conv(g_b, g_a[half - CARRY_ROWS:, :]), v_b)
    carry_ref[...] = g_b[half - CARRY_ROWS:, :]
    d_b = _bdot(act_b, wdn_ref[...])
    post(rows_a, h_a, d_a)
    post(rows_b, h_b, d_b)

    @pl.when(t == nt - 1)
    def _():
        nb_ref[...] = carry_ref[CARRY_ROWS - 2:, :]


def _ffn_sample_kernel(h_ref, mod_ref, ng_ref, wg_ref, wv_ref, cw_ref, cb_ref, wd_ref, st_ref,
                       o_ref, nb_ref, wgb_ref, wvb_ref, wdb_ref, u_ref, acc_ref, *, nj):
    j = pl.program_id(0)
    mod = lambda i: _mod(mod_ref, i, True)

    @pl.when(j == 0)
    def _():
        u_ref[...] = (_rms(h_ref[...], ng_ref[2:3, :]) * (1.0 + mod(4)) + mod(3)).astype(BF16)
        acc_ref[...] = jnp.zeros_like(acc_ref)

    wg, wv, wd = wg_ref[...].astype(BF16), wv_ref[...].astype(BF16), wd_ref[...].astype(BF16)
    wgb_ref[...], wvb_ref[...], wdb_ref[...] = wg, wv, wd
    u = u_ref[...]
    g = _bdot(u, wg)
    val = _bdot(u, wv)
    cw = cw_ref[...]
    hc = st_ref[:, 0, :] * cw[0:1] + st_ref[:, 1, :] * cw[1:2] + g * cw[2:3]
    nb_ref[:, 0, :] = st_ref[:, 1, :]
    nb_ref[:, 1, :] = g
    acc_ref[...] += _bdot((_silu(hc + cb_ref[...]) * val).astype(BF16), wd)

    @pl.when(j == nj - 1)
    def _():
        o_ref[...] = h_ref[...] + mod(5) * _rms(acc_ref[...], ng_ref[3:4, :])


def _ffn_prompt_layer(h, mod, l, P, wts):
    nb, rows, _ = h.shape
    tm = 2 * HALF
    nt = rows // tm
    wg, wv, wd = wts
    return pl.pallas_call(
        functools.partial(_ffn_prompt_kernel, tm=tm, nt=nt),
        grid=(nb, nt),
        in_specs=[_rows(tm, D), _mod_spec(mod, l), _layer(P["norm_g"], l), _whole(wg.shape), _whole(wv.shape),
                  _layer(P["ffn_conv_w"], l), _layer(P["ffn_conv_b"], l), _whole(wd.shape)],
        out_specs=[_rows(tm, D), _per_seq(2, D_FF)],
        out_shape=[jax.ShapeDtypeStruct(h.shape, F32), jax.ShapeDtypeStruct((nb, 2, D_FF), F32)],
        scratch_shapes=[pltpu.VMEM((CARRY_ROWS, D_FF), F32)], compiler_params=_params(),
        name="ffn_prompt",
    )(h, mod, P["norm_g"], wg, wv, P["ffn_conv_w"], P["ffn_conv_b"], wd)


def _ffn_sample_layer(h, mod, l, P, state):
    _, rows, _ = h.shape
    c = FFN_CHUNK
    nj = D_FF // c
    w_up, w_down = P["ffn_w_up_f32"], P["ffn_w_down_f32"]
    bf = lambda shape: jax.ShapeDtypeStruct(shape, BF16)
    h_new, nbuf, wg, wv, wd = pl.pallas_call(
        functools.partial(_ffn_sample_kernel, nj=nj),
        grid=(nj,),
        in_specs=[pl.BlockSpec((None, rows, D), lambda j: (0, 0, 0)),
                  pl.BlockSpec((None, None) + tuple(mod.shape[2:]), lambda j: (l, 0, 0, 0)),
                  pl.BlockSpec((None, 4, D), lambda j: (l, 0, 0)),
                  pl.BlockSpec((None, D, c), lambda j: (l, 0, j)),
                  pl.BlockSpec((None, D, c), lambda j: (l, 0, nj + j)),
                  pl.BlockSpec((None, 3, c), lambda j: (l, 0, j)),
                  pl.BlockSpec((None, 1, c), lambda j: (l, 0, j)),
                  pl.BlockSpec((None, c, D), lambda j: (l, j, 0)),
                  pl.BlockSpec((None, rows, 2, c), lambda j: (l, 0, 0, j))],
        out_specs=[pl.BlockSpec((None, rows, D), lambda j: (0, 0, 0)),
                   pl.BlockSpec((rows, 2, c), lambda j: (0, 0, j)),
                   pl.BlockSpec((D, c), lambda j: (0, j)),
                   pl.BlockSpec((D, c), lambda j: (0, j)),
                   pl.BlockSpec((c, D), lambda j: (j, 0))],
        out_shape=[jax.ShapeDtypeStruct(h.shape, F32), jax.ShapeDtypeStruct((rows, 2, D_FF), F32),
                   bf((D, D_FF)), bf((D, D_FF)), bf((D_FF, D))],
        scratch_shapes=[pltpu.VMEM((rows, D), BF16), pltpu.VMEM((rows, D), F32)],
        compiler_params=_params(1),
        name="ffn_sample",
    )(h, mod, P["norm_g"], w_up, w_up, P["ffn_conv_w"], P["ffn_conv_b"], w_down, state)
    return h_new, nbuf, (wg, wv, wd)


def _rwkv_pre_kernel(*refs, sample, tm, nt):
    (h_ref, mod_ref, ng_ref, mu_ref, wrkv_ref, w0_ref, w1_ref, w2_ref, a0_ref, a1_ref, a2_ref,
     g1_ref, g2_ref, kk_ref, ka_ref, rk_ref, seg1_ref, seg2_ref) = refs[:18]
    if sample:
        shift_ref = refs[18]
        outs = refs[19:]
    else:
        tri_ref, sel_ref = refs[18:20]
        outs = refs[20:-1]
        carry_ref = refs[-1]
        t = pl.program_id(1)

        @pl.when(t == 0)
        def _():
            carry_ref[...] = jnp.zeros_like(carry_ref)

    at_ref, bt_ref, kt_ref, rt_ref, v_ref, g_ref, bonus_ref, wc_ref, sh_ref = outs
    mod = lambda i: _mod(mod_ref, i, sample)
    seg = lambda x: _segsum(x, seg1_ref[...], seg2_ref[...])
    tails = {}

    def half(i, rows):
        n = rows.stop - rows.start

        def put(ref, val):
            if sample:
                ref[...] = val.T
            else:
                ref[rows, :] = val

        u = _rms(h_ref[rows, :], ng_ref[0:1, :]) * (1.0 + mod(1)) + mod(0)
        if sample:
            prev = shift_ref[...]
            sh_ref[...] = u
        else:
            tails[i] = u[n - CARRY_ROWS:, :]
            prev = _shift_rows(u, 1, carry_ref[...] if i == 0 else tails[i - 1])
        ub, xb = u.astype(BF16), (prev - u).astype(BF16)
        mix = lambda m: ub + xb * mu_ref[m:m + 1, :].astype(BF16)
        r = _bdot(mix(0), wrkv_ref[0])
        k = _bdot(mix(2), wrkv_ref[1])
        v = _bdot(mix(3), wrkv_ref[2])
        zw = _bdot(mix(1), w1_ref[...])
        za = _bdot(mix(4), a1_ref[...])
        zg = _bdot(mix(5), g1_ref[...])
        yield
        put(v_ref, v)
        z = w0_ref[...] + _bdot(jnp.tanh(zw).astype(BF16), w2_ref[...])
        a = jax.nn.sigmoid(a0_ref[...] + _bdot(za.astype(BF16), a2_ref[...]))
        g_ref[rows, :] = _bdot(jax.nn.sigmoid(zg).astype(BF16), g2_ref[...])
        lw = -DECAY_SCALE * jax.nn.sigmoid(z)
        kk = k * kk_ref[...]
        k2 = k * (1.0 + (a - 1.0) * ka_ref[...])
        ss = seg(kk * kk)
        rk = seg(r * k2 * rk_ref[...])
        if sample:
            lc = lw
        else:
            lc = _dot01(tri_ref[...], lw, 2)
        yield
        kk = kk * jnp.minimum(lax.rsqrt(ss), 1e12)
        bonus_ref[rows, :] = rk * v
        if sample:
            wc_ref[...] = jnp.exp(lw).T
        else:
            wc_ref[i] = jnp.exp(_dot01(sel_ref[...], lc, 3))
        e_neg = jnp.exp(-lc)
        put(at_ref, -kk * jnp.exp(lc - lw))
        put(bt_ref, kk * a * e_neg)
        put(kt_ref, k2 * e_neg)
        put(rt_ref, (r * jnp.exp(lc)).astype(rt_ref.dtype))

    halves = _half_rows(tm, sample)
    _interleave([half(i, rows) for i, rows in enumerate(halves)])
    if not sample:
        carry_ref[...] = tails[len(halves) - 1]

        @pl.when(t == nt - 1)
        def _():
            sh_ref[...] = carry_ref[CARRY_ROWS - 1:, :]


def _rwkv_core_kernel(at_ref, bt_ref, kt_ref, rt_ref, v_ref, wc_ref, y_ref, sout_ref, s_ref, *, nt, nchunk):
    t = pl.program_id(1)

    @pl.when(t == 0)
    def _():
        s_ref[...] = jnp.zeros_like(s_ref)

    n = RWKV_HEAD
    lane_lo = lax.broadcasted_iota(jnp.int32, (1, LANES), 1) < n
    ri = lax.broadcasted_iota(jnp.int32, (CHUNK, LANES), 0)
    ci = lax.broadcasted_iota(jnp.int32, (CHUNK, LANES), 1) & (n - 1)
    m_strict = ci < ri
    m_incl = ci <= ri
    eye = (ci == ri).astype(F32)
    shift = n.bit_length() - 1
    m_bd = ((lax.broadcasted_iota(jnp.int32, (LANES, LANES), 0) >> shift)
            == (lax.broadcasted_iota(jnp.int32, (LANES, LANES), 1) >> shift))

    def bd(x):
        z = jnp.zeros_like(x)
        return _cat0(jnp.where(lane_lo, x, z), jnp.where(lane_lo, z, x))

    bd2 = lambda xs: (bd(xs[0]), bd(xs[1]))

    zero = jnp.zeros((LANES, LANES), BF16)
    halves = lambda w: w[:, :LANES] + w[:, LANES:]

    def rhs3(y):
        return _cat0(_cat1(y[0], y[1]), _cat1(y[0], zero))

    def rhs3_t(y):
        return _cat0(_cat1(y[0], y[0]), _cat1(y[1], zero))

    pairs = range(RWKV_HEADS // 2)
    lanes = [slice(p * LANES, (p + 1) * LANES) for p in pairs]
    npair = len(lanes)
    units = range(nchunk * npair)
    tok = [slice((i // npair) * CHUNK, (i // npair + 1) * CHUNK) for i in units]
    ln = [lanes[i % npair] for i in units]
    a_k = [_cat1(*_split2(at_ref[tok[i], ln[i]])) for i in units]
    b_s = [_split2(bt_ref[tok[i], ln[i]]) for i in units]
    k_s = [_split2(kt_ref[tok[i], ln[i]]) for i in units]
    v_s = [_split2(v_ref[tok[i], ln[i]]) for i in units]
    b_bd = [bd2(b_s[i]) for i in units]
    k_bd = [bd2(k_s[i]) for i in units]
    v_bd = [bd2(v_s[i]) for i in units]
    r_s = [rt_ref[tok[i], ln[i]] for i in units]
    bk_hi = [_cat0(b_s[i][0], k_s[i][0]) for i in units]
    a_ak = [_split2(jnp.where(m_strict, halves(_dot_nt(a_k[i], rhs3_t(k_bd[i]))), 0.0)) for i in units]
    sc_r = [_dot_nt(r_s[i], _cat0(b_bd[i][0], k_bd[i][0])) for i in units]
    a_r = [_cat1(jnp.where(m_incl, sc_r[i][:, :LANES], 0.0).astype(BF16),
                 jnp.where(m_incl, sc_r[i][:, LANES:], 0.0).astype(BF16)) for i in units]
    pw = [jnp.where(m_strict, halves(_dot_nt(a_k[i], rhs3_t(b_bd[i]))), 0.0) for i in units]
    tinv = [eye + pw[i] for i in units]
    pws = [_split2(pw[i]) for i in units]
    pw = [halves(_bdot(_cat1(*pws[i]), rhs3(bd2(pws[i])))) for i in units]
    for step in range(5):
        pws = [_split2(pw[i]) for i in units]
        tis = [_split2(tinv[i]) for i in units]
        rhs = [rhs3(bd2(pws[i])) for i in units]
        if step < 4:
            w = [_bdot(_cat0(_cat1(*pws[i]), _cat1(*tis[i])), rhs[i]) for i in units]
            pw = [halves(w[i][:CHUNK]) for i in units]
            tinv = [tinv[i] + halves(w[i][CHUNK:]) for i in units]
        else:
            tinv = [tinv[i] + halves(_bdot(_cat1(*tis[i]), rhs[i])) for i in units]
    tis = [_cat1(*_split2(tinv[i])) for i in units]
    s = [s_ref[p] for p in pairs]
    for c in range(nchunk):
        un = [c * npair + p for p in pairs]
        ss = [_split2(s[p]) for p in pairs]
        x = [halves(_dot_nt(a_k[i], rhs3_t(ss[p]))) + halves(_bdot(_cat1(*a_ak[i]), rhs3(v_bd[i])))
             for p, i in enumerate(un)]
        ub = [halves(_bdot(tis[i], rhs3(bd2(_split2(x[p]))))).astype(BF16) for p, i in enumerate(un)]
        for p, i in enumerate(un):
            y_ref[tok[i], lanes[p]] = (_dot_nt(r_s[i], ss[p][0])
                                       + _bdot(a_r[i], _cat0(bd(ub[p]), v_bd[i][0])))
        s = [jnp.where(m_bd, s[p] + _dot_tn(_cat0(ub[p], v_s[i][0]), bk_hi[i]), 0.0) * wc_ref[c][:, lanes[p]]
             for p, i in enumerate(un)]
    for p in pairs:
        s_ref[p] = s[p]

    @pl.when(t == nt - 1)
    def _():
        for p in pairs:
            sout_ref[2 * p] = s[p][:n, :n]
            sout_ref[2 * p + 1] = pltpu.roll(s[p], n, axis=1)[n:, :n]


def _rwkv_core_sample_kernel(at_ref, bt_ref, kt_ref, rt_ref, v_ref, wc_ref, s_ref, y_ref, sout_ref):
    s = s_ref[...]
    u = jnp.sum(s * at_ref[...][None, :, :], axis=1)
    s1 = s + u[:, None, :] * bt_ref[...][None, :, :] + v_ref[...][:, None, :] * kt_ref[...][None, :, :]
    y_ref[...] = jnp.sum(s1 * rt_ref[...][None, :, :], axis=1)
    sout_ref[...] = s1 * wc_ref[...][None, :, :]


def _rwkv_post_kernel(y_ref, g_ref, bonus_ref, h_ref, mod_ref, ng_ref, lnw_ref, lnb_ref, wo_ref,
                      seg1_ref, seg2_ref, o_ref, *, sample, tm):
    mod = lambda i: _mod(mod_ref, i, sample)
    seg = lambda x: _segsum(x, seg1_ref[...], seg2_ref[...])

    def half(rows):
        y = y_ref[rows, :]
        mean = seg(y) * (1.0 / RWKV_HEAD)
        yield
        yc = y - mean
        var = seg(yc * yc) * (1.0 / RWKV_HEAD)
        yield
        yn = yc * lax.rsqrt(var + RWKV_GN_EPS) * lnw_ref[...] + lnb_ref[...] + bonus_ref[rows, :]
        out = _bdot((yn * g_ref[rows, :]).astype(BF16), wo_ref[...])
        yield
        o_ref[rows, :] = h_ref[rows, :] + mod(2) * _rms(out, ng_ref[1:2, :])

    _interleave([half(rows) for rows in _half_rows(tm, sample)])


def _rwkv_layer(h, mod, l, j, P, state_shift, state_wkv, *, sample):
    nb, rows, _ = h.shape
    tm = rows if sample else 2 * HALF
    nt = rows // tm
    names = ("rwkv_mu", "rwkv_w_rkv", "rwkv_w0", "rwkv_w1", "rwkv_w2", "rwkv_a0", "rwkv_a1", "rwkv_a2",
             "rwkv_g1", "rwkv_g2", "rwkv_k_k", "rwkv_k_a", "rwkv_r_k")
    consts = [(P["norm_g"], l)] + [(P[n], j) for n in names]
    seg1, seg2p = P["seg1"], P["seg2p"]
    in_specs = ([_rows(tm, D), _mod_spec(mod, l)] + [_layer(a, i) for a, i in consts]
                + [_whole(seg1.shape), _whole(seg2p.shape)])
    args = [h, mod] + [a for a, _ in consts] + [seg1, seg2p]
    if sample:
        in_specs.append(_per_seq(rows, D))
        args.append(state_shift)
        act_shape, act_spec = (nb, D, rows), pl.BlockSpec((None, D, rows), lambda b, t: (b, 0, 0))
        wc_shape, wc_spec = act_shape, act_spec
        sh_shape, sh_spec = (nb, rows, D), _rows(tm, D)
        scratch = []
    else:
        ti = np.arange(HALF)
        tri = _np_bf16((ti[:, None] // CHUNK == ti[None, :] // CHUNK) & (ti[None, :] <= ti[:, None]))
        sel = _np_bf16(ti[None, :] == (np.arange(CARRY_ROWS)[:, None] * CHUNK + CHUNK - 1))
        in_specs += [_whole(tri.shape), _whole(sel.shape)]
        args += [tri, sel]
        act_shape, act_spec = h.shape, _rows(tm, D)
        wc_shape = (nb, rows // HALF, CARRY_ROWS, D)
        wc_spec = pl.BlockSpec((None, tm // HALF, CARRY_ROWS, D), lambda b, t: (b, t, 0, 0))
        sh_shape, sh_spec = (nb, 1, D), _per_seq(1, D)
        scratch = [pltpu.VMEM((CARRY_ROWS, D), F32)]
    f32 = jax.ShapeDtypeStruct(h.shape, F32)
    act = jax.ShapeDtypeStruct(act_shape, F32)
    rt_shape = act if sample else jax.ShapeDtypeStruct(h.shape, BF16)
    at, bt, kt, rt, vv, g, bonus, wc, new_shift = pl.pallas_call(
        functools.partial(_rwkv_pre_kernel, sample=sample, tm=tm, nt=nt),
        grid=(nb, nt), in_specs=in_specs,
        out_specs=[act_spec] * 5 + [_rows(tm, D)] * 2 + [wc_spec, sh_spec],
        out_shape=[act, act, act, rt_shape, act, f32, f32, jax.ShapeDtypeStruct(wc_shape, F32),
                   jax.ShapeDtypeStruct(sh_shape, F32)],
        scratch_shapes=scratch, compiler_params=_params(),
        name="rwkv_pre_sample" if sample else "rwkv_pre_prompt",
    )(*args)

    hh, n = RWKV_HEADS, RWKV_HEAD
    if sample:
        hv = lambda x: x.reshape(hh, n, rows)
        vec_spec = pl.BlockSpec((None, n, rows), lambda i: (i, 0, 0))
        st_spec = pl.BlockSpec((None, n, n, rows), lambda i: (i, 0, 0, 0))
        y_t, s_t = pl.pallas_call(
            _rwkv_core_sample_kernel,
            grid=(hh,),
            in_specs=[vec_spec] * 6 + [st_spec],
            out_specs=[vec_spec, st_spec],
            out_shape=[jax.ShapeDtypeStruct((hh, n, rows), F32), jax.ShapeDtypeStruct((hh, n, n, rows), F32)],
            compiler_params=_params(1),
            name="rwkv_core_sample",
        )(hv(at), hv(bt), hv(kt), hv(rt), hv(vv), hv(wc), jnp.transpose(state_wkv, (1, 2, 3, 0)))
        y = y_t.reshape(D, rows).T.reshape(nb, rows, D)
        new_s = jnp.transpose(s_t, (3, 0, 1, 2))
    else:
        ncht = rows // CHUNK
        wc_rows = wc[:, :, :HALF // CHUNK, :].reshape(nb, ncht, 1, D)
        npair = hh // 2
        nc = RWKV_CHUNKS_PER_STEP
        y, new_s = pl.pallas_call(
            functools.partial(_rwkv_core_kernel, nt=ncht // nc, nchunk=nc),
            grid=(nb, ncht // nc),
            in_specs=[_rows(nc * CHUNK, D)] * 5 + [pl.BlockSpec((None, nc, 1, D), lambda b, t: (b, t, 0, 0))],
            out_specs=[_rows(nc * CHUNK, D), pl.BlockSpec((None, hh, n, n), lambda b, t: (b, 0, 0, 0))],
            out_shape=[f32, jax.ShapeDtypeStruct((nb, hh, n, n), F32)],
            scratch_shapes=[pltpu.VMEM((npair, LANES, LANES), F32)],
            compiler_params=_params(),
            name="rwkv_core_prompt",
        )(at, bt, kt, rt, vv, wc_rows)

    consts = [(P["norm_g"], l), (P["rwkv_lnx_w"], j), (P["rwkv_lnx_b"], j), (P["rwkv_w_o"], j)]
    h_new = pl.pallas_call(
        functools.partial(_rwkv_post_kernel, sample=sample, tm=tm),
        grid=(nb, nt),
        in_specs=([_rows(tm, D)] * 4 + [_mod_spec(mod, l)] + [_layer(a, i) for a, i in consts]
                  + [_whole(seg1.shape), _whole(seg2p.shape)]),
        out_specs=_rows(tm, D), out_shape=f32, compiler_params=_params(),
        name="rwkv_post_sample" if sample else "rwkv_post_prompt",
    )(y, g, bonus, h, mod, *[a for a, _ in consts], seg1, seg2p)
    return h_new, new_shift, new_s


GLA_LEVELS = (1, 2, 4, 8, 16, 32)


def _gla_pre_stages(h_ref, mod_ref, ng_ref, win_ref, wa1_ref, wa2_ref, ba_ref,
                    q_ref, k_ref, v_ref, og_ref, lg_ref, rows, sample):
    mod = lambda i: _mod(mod_ref, i, sample)
    dk, dv = GLA_DK_TOTAL, GLA_DV_TOTAL
    u = (_rms(h_ref[rows, :], ng_ref[0:1, :]) * (1.0 + mod(1)) + mod(0)).astype(BF16)
    p = _bdot(u, win_ref[...])
    gate = _bdot(_bdot(u, wa1_ref[...]).astype(BF16), wa2_ref[...]) + ba_ref[...]
    yield
    q_ref[rows, :] = p[:, :dk] * (GLA_DK ** -0.5)
    k_ref[rows, :] = p[:, dk:2 * dk]
    v_ref[rows, :] = p[:, 2 * dk:2 * dk + dv]
    og_ref[rows, :] = p[:, 2 * dk + dv:]
    lg_ref[rows, :] = -_softplus(-gate) * (1.0 / GLA_GATE_NORMALIZER)


def _gla_pre_kernel(*refs, sample, tm):
    _interleave([_gla_pre_stages(*refs, rows, sample) for rows in _half_rows(tm, sample)])


def _gla_core_stages(q_ref, k_ref, v_ref, lg_ref, lvl_ref, o_ref, state, row0, nchunk):
    row = lax.broadcasted_iota(jnp.int32, (CHUNK, 1), 0)
    ri = lax.broadcasted_iota(jnp.int32, (CHUNK, CHUNK), 0)
    ci = lax.broadcasted_iota(jnp.int32, (CHUNK, CHUNK), 1)
    heads = range(GLA_HEADS)
    units = range(nchunk * GLA_HEADS)
    chunk_rows = [slice(row0 + c * CHUNK, row0 + (c + 1) * CHUNK) for c in range(nchunk)]
    tok = [chunk_rows[i // GLA_HEADS] for i in units]
    sl = [slice((i % GLA_HEADS) * GLA_DK, (i % GLA_HEADS + 1) * GLA_DK) for i in units]
    sv = [slice((i % GLA_HEADS) * GLA_DV, (i % GLA_HEADS + 1) * GLA_DV) for i in units]
    lg = [lg_ref[chunk_rows[c], :] for c in range(nchunk)]
    pq = [_bdot(lvl_ref[...], _cat0(*_split3(lg[c]))) for c in range(nchunk)]
    q = [q_ref[tok[i], sl[i]] for i in units]
    k = [k_ref[tok[i], sl[i]] for i in units]
    vb = [v_ref[tok[i], sv[i]].astype(BF16) for i in units]
    yield
    a = [jnp.where(ri == ci, _dot_nt(q[i].astype(BF16), k[i].astype(BF16)), 0.0) for i in units]
    for li, m in enumerate(GLA_LEVELS):
        lm = m.bit_length() - 1
        second = ((row >> lm) & 1) == 1
        block = (ri >> (lm + 1)) == (ci >> (lm + 1))
        base = (li - 1) * 2 * CHUNK
        for i in units:
            c = i // GLA_HEADS
            if m == 1:
                qe = jnp.where(second, q[i] * jnp.exp(lg[c][:, sl[i]]), 0.0)
                ke = jnp.where(second, 0.0, k[i])
            else:
                qe = jnp.where(second, q[i] * jnp.exp(pq[c][base:base + CHUNK, sl[i]]), 0.0)
                ke = jnp.where(second, 0.0, k[i] * jnp.exp(pq[c][base + CHUNK:base + 2 * CHUNK, sl[i]]))
            a[i] = a[i] + jnp.where(block, _dot_nt(qe.astype(BF16), ke.astype(BF16)), 0.0)
        yield
    base = (len(GLA_LEVELS) - 1) * 2 * CHUNK
    g_inc = [pq[i // GLA_HEADS][base:base + CHUNK, sl[i]] for i in units]
    g_rest = [pq[i // GLA_HEADS][base + CHUNK:base + 2 * CHUNK, sl[i]] for i in units]
    av = [_bdot(a[i].astype(BF16), vb[i]) for i in units]
    qg = [(q[i] * jnp.exp(g_inc[i])).astype(BF16) for i in units]
    kg = [(k[i] * jnp.exp(g_rest[i])).astype(BF16) for i in units]
    kv = [_dot_tn(vb[i], kg[i]) for i in units]
    yield
    s = state["s"]
    for c in range(nchunk):
        for hd in heads:
            i = c * GLA_HEADS + hd
            o_ref[tok[i], sv[i]] = av[i] + _dot_nt(qg[i], s[hd].astype(BF16))
        s = [s[hd] * jnp.exp(g_inc[c * GLA_HEADS + hd][CHUNK - 1:CHUNK, :]) + kv[c * GLA_HEADS + hd]
             for hd in heads]
        yield
    state["s"] = s


def _gla_core_sample_kernel(q_ref, k_ref, v_ref, lg_ref, s_ref, o_ref, sout_ref):
    n = GLA_DK
    eye = (lax.broadcasted_iota(jnp.int32, (n, n), 0) == lax.broadcasted_iota(jnp.int32, (n, n), 1)).astype(F32)
    col = lambda x: jnp.sum(eye * x, axis=-1, keepdims=True)
    q, k, v = q_ref[...], k_ref[...], v_ref[...]
    s = s_ref[...]
    decay = jnp.exp(lg_ref[...])
    qk = jnp.sum(q * k, axis=-1, keepdims=True)
    o_ref[...] = qk * v + jnp.sum(col(q * decay) * s, axis=-2, keepdims=True)
    sout_ref[...] = col(decay) * s + col(k) * v


def _gla_post_stages(o_ref, og_ref, h_ref, mod_ref, ng_ref, on_ref, wo_ref, out_ref, rows, sample):
    mod = lambda i: _mod(mod_ref, i, sample)
    parts = []
    for hd in range(GLA_HEADS):
        sv = slice(hd * GLA_DV, (hd + 1) * GLA_DV)
        parts.append((_rms(o_ref[rows, sv], on_ref[...]) * _silu(og_ref[rows, sv])).astype(BF16))
    out = _bdot(_cat1(*parts), wo_ref[...])
    yield
    out_ref[rows, :] = h_ref[rows, :] + mod(2) * _rms(out, ng_ref[1:2, :])


def _gla_post_kernel(o_ref, *refs, sample):
    _interleave([_gla_post_stages(o_ref, *refs, slice(0, o_ref.shape[0]), sample)])


def _gla_level_matrix():
    ti = np.arange(CHUNK)
    blocks = []
    for m in GLA_LEVELS[1:] + (CHUNK,):
        same = ti[:, None] // m == ti[None, :] // m
        blocks.append(same & (ti[None, :] <= ti[:, None]))
        blocks.append(same & (ti[None, :] > ti[:, None]))
    one = np.concatenate(blocks, axis=0)
    return _np_bf16(np.concatenate([one, one, one], axis=1))


def _gla_fused_kernel(h_ref, mod_ref, ng_ref, win_ref, wa1_ref, wa2_ref, ba_ref, on_ref, wo_ref, lvl_ref,
                      out_ref, sout_ref, q_s, k_s, v_s, og_s, lg_s, o_s, s_ref, *, tm, nt):
    t = pl.program_id(1)

    @pl.when(t == 0)
    def _():
        s_ref[...] = jnp.zeros_like(s_ref)

    half = tm // 2
    rows_a, rows_b = slice(0, half), slice(half, tm)
    state = {"s": [s_ref[hd] for hd in range(GLA_HEADS)]}
    pre = lambda rows: _gla_pre_stages(h_ref, mod_ref, ng_ref, win_ref, wa1_ref, wa2_ref, ba_ref,
                                       q_s, k_s, v_s, og_s, lg_s, rows, False)
    core = lambda row0: _gla_core_stages(q_s, k_s, v_s, lg_s, lvl_ref, o_s, state, row0, half // CHUNK)
    post = lambda rows: _gla_post_stages(o_s, og_s, h_ref, mod_ref, ng_ref, on_ref, wo_ref, out_ref, rows, False)
    _interleave([pre(rows_a)])
    _interleave([core(0), pre(rows_b)])
    _interleave([core(half), post(rows_a)])
    _interleave([post(rows_b)])
    for hd in range(GLA_HEADS):
        s_ref[hd] = state["s"][hd]

    @pl.when(t == nt - 1)
    def _():
        sout_ref[...] = s_ref[...]


def _gla_prompt_layer(h, mod, l, j, P):
    nb, rows, _ = h.shape
    tm = 2 * HALF
    nt = rows // tm
    dk, dv, hh = GLA_DK_TOTAL, GLA_DV_TOTAL, GLA_HEADS
    lvl = _gla_level_matrix()
    consts = [(P["norm_g"], l), (P["gla_w_in"], j), (P["gla_wa1"], j), (P["gla_wa2"], j), (P["gla_ba"], j),
              (P["gla_onorm_g"], j), (P["gla_w_o"], j)]
    h_new, s_t = pl.pallas_call(
        functools.partial(_gla_fused_kernel, tm=tm, nt=nt),
        grid=(nb, nt),
        in_specs=[_rows(tm, D), _mod_spec(mod, l)] + [_layer(a, i) for a, i in consts] + [_whole(lvl.shape)],
        out_specs=[_rows(tm, D), pl.BlockSpec((None, hh, GLA_DV, GLA_DK), lambda b, t: (b, 0, 0, 0))],
        out_shape=[jax.ShapeDtypeStruct(h.shape, F32), jax.ShapeDtypeStruct((nb, hh, GLA_DV, GLA_DK), F32)],
        scratch_shapes=[pltpu.VMEM((tm, dk), F32), pltpu.VMEM((tm, dk), F32), pltpu.VMEM((tm, dv), F32),
                        pltpu.VMEM((tm, dv), F32), pltpu.VMEM((tm, dk), F32), pltpu.VMEM((tm, dv), F32),
                        pltpu.VMEM((hh, GLA_DV, GLA_DK), F32)],
        compiler_params=_params(),
        name="gla_prompt",
    )(h, mod, *[a for a, _ in consts], lvl)
    return h_new, jnp.swapaxes(s_t, -1, -2)


def _gla_layer(h, mod, l, j, P, state, *, sample):
    if not sample:
        return _gla_prompt_layer(h, mod, l, j, P)
    nb, rows, _ = h.shape
    tm = rows
    nt = rows // tm
    dk, dv, hh = GLA_DK_TOTAL, GLA_DV_TOTAL, GLA_HEADS
    consts = [(P["norm_g"], l), (P["gla_w_in"], j), (P["gla_wa1"], j), (P["gla_wa2"], j), (P["gla_ba"], j)]
    shp = lambda width: jax.ShapeDtypeStruct((nb, rows, width), F32)
    q, k, v, og, lg = pl.pallas_call(
        functools.partial(_gla_pre_kernel, sample=sample, tm=tm),
        grid=(nb, nt),
        in_specs=[_rows(tm, D), _mod_spec(mod, l)] + [_layer(a, i) for a, i in consts],
        out_specs=[_rows(tm, dk), _rows(tm, dk), _rows(tm, dv), _rows(tm, dv), _rows(tm, dk)],
        out_shape=[shp(dk), shp(dk), shp(dv), shp(dv), shp(dk)],
        compiler_params=_params(),
        name="gla_pre_sample" if sample else "gla_pre_prompt",
    )(h, mod, *[a for a, _ in consts])

    if sample:
        sb = 8
        nseq = rows
        hk = lambda x: x.reshape(nseq, hh, 1, GLA_DK)
        kspec = pl.BlockSpec((sb, hh, 1, GLA_DK), lambda i: (i, 0, 0, 0))
        vspec = pl.BlockSpec((sb, hh, 1, GLA_DV), lambda i: (i, 0, 0, 0))
        sspec = pl.BlockSpec((sb, hh, GLA_DK, GLA_DV), lambda i: (i, 0, 0, 0))
        o, new_s = pl.pallas_call(
            _gla_core_sample_kernel,
            grid=(nseq // sb,),
            in_specs=[kspec, kspec, vspec, kspec, sspec],
            out_specs=[vspec, sspec],
            out_shape=[jax.ShapeDtypeStruct((nseq, hh, 1, GLA_DV), F32),
                       jax.ShapeDtypeStruct((nseq, hh, GLA_DK, GLA_DV), F32)],
            compiler_params=_params(1),
            name="gla_core_sample",
        )(hk(q), hk(k), v.reshape(nseq, hh, 1, GLA_DV), hk(lg), state)
        o = o.reshape(nb, rows, dv)

    tp = rows
    consts = [(P["norm_g"], l), (P["gla_onorm_g"], j), (P["gla_w_o"], j)]
    h_new = pl.pallas_call(
        functools.partial(_gla_post_kernel, sample=sample),
        grid=(nb, rows // tp),
        in_specs=[_rows(tp, dv), _rows(tp, dv), _rows(tp, D), _mod_spec(mod, l)] + [_layer(a, i) for a, i in consts],
        out_specs=_rows(tp, D), out_shape=shp(D), compiler_params=_params(),
        name="gla_post_sample" if sample else "gla_post_prompt",
    )(o, og, h, mod, *[a for a, _ in consts])
    return h_new, new_s


def _trunk(h, mod, P, states, ffn_wts, *, sample):
    st_a, st_sb, st_wb, st_gc, st_f = states
    nb, rows, _ = h.shape
    new_a, new_sb, new_wb, new_gc, new_f = [], [], [], [], []
    for l in range(DEPTH):
        kind, j = l % 3, l // 3
        if kind == 0:
            h, nbuf = _sconv_layer(h, mod, l, j, P, st_a, sample=sample)
            new_a.append(nbuf)
        elif kind == 1:
            sh = st_sb[j].reshape(1, rows, D) if sample else None
            wkv = st_wb[j] if sample else None
            h, nsh, ns = _rwkv_layer(h, mod, l, j, P, sh, wkv, sample=sample)
            new_sb.append(nsh.reshape(rows, D) if sample else nsh.reshape(nb, D))
            new_wb.append(ns)
        else:
            h, ns = _gla_layer(h, mod, l, j, P, st_gc[j] if sample else None, sample=sample)
            new_gc.append(ns)
        if sample:
            h, nbuf, wts = _ffn_sample_layer(h, mod, l, P, st_f)
            ffn_wts.append(wts)
        else:
            h, nbuf = _ffn_prompt_layer(h, mod, l, P, ffn_wts[l])
        new_f.append(nbuf)
    return (h, jnp.stack(new_a), jnp.stack(new_sb), jnp.stack(new_wb), jnp.stack(new_gc), jnp.stack(new_f))


def _prepare(p):
    bf = lambda x: x.astype(BF16)
    row = lambda x: x.reshape(x.shape[0], 1, -1)
    pad_c = lambda x: bf(jnp.pad(x, ((0, 0), (0, 0), (0, LANES - x.shape[2]))))
    pad_r = lambda x: bf(jnp.pad(x, ((0, 0), (0, LANES - x.shape[1]), (0, 0))))
    lane_head = np.arange(D) // RWKV_HEAD
    col = np.arange(LANES)
    P = dict(
        norm_g=p["norm_g"],
        sconv_w_in=bf(p["sconv_w_in"]), sconv_conv_w=p["sconv_conv_w"], sconv_w_out=bf(p["sconv_w_out"]),
        rwkv_mu=p["rwkv_mu"], rwkv_w_rkv=bf(p["rwkv_w_rkv"]), rwkv_w0=row(p["rwkv_w0"]),
        rwkv_w1=pad_c(p["rwkv_w1"]), rwkv_w2=pad_r(p["rwkv_w2"]), rwkv_a0=row(p["rwkv_a0"]),
        rwkv_a1=pad_c(p["rwkv_a1"]), rwkv_a2=pad_r(p["rwkv_a2"]), rwkv_g1=bf(p["rwkv_g1"]), rwkv_g2=bf(p["rwkv_g2"]),
        rwkv_k_k=row(p["rwkv_k_k"]), rwkv_k_a=row(p["rwkv_k_a"]), rwkv_r_k=row(p["rwkv_r_k"]),
        rwkv_lnx_w=row(p["rwkv_lnx_w"]), rwkv_lnx_b=row(p["rwkv_lnx_b"]), rwkv_w_o=bf(p["rwkv_w_o"]),
        gla_w_in=bf(p["gla_w_in"]), gla_wa1=pad_c(p["gla_wa1"]), gla_wa2=pad_r(p["gla_wa2"]),
        gla_ba=row(p["gla_ba"]), gla_onorm_g=row(p["gla_onorm_g"]), gla_w_o=bf(p["gla_w_o"]),
        ffn_w_up_f32=p["ffn_w_up"], ffn_conv_w=p["ffn_conv_w"], ffn_conv_b=row(p["ffn_conv_b"]),
        ffn_w_down_f32=p["ffn_w_down"],
        seg1=_np_bf16(lane_head[:, None] == col[None, :]),
        seg2p=_np_bf16((col[:, None] % SEG_STRIDE == lane_head[None, :]) & (col[:, None] < 3 * SEG_STRIDE)),
    )
    return P


def kernel(x_prompt, x_sample, state_conv_a, state_shift_b, state_wkv_b, state_gla_c, state_conv_ffn, c_prompt, c_sample, ada_w, ada_b, norm_g, sconv_w_in, sconv_conv_w, sconv_w_out, rwkv_mu, rwkv_w_rkv, rwkv_w0, rwkv_w1, rwkv_w2, rwkv_a0, rwkv_a1, rwkv_a2, rwkv_g1, rwkv_g2, rwkv_k_k, rwkv_k_a, rwkv_r_k, rwkv_lnx_w, rwkv_lnx_b, rwkv_w_o, gla_w_in, gla_wa1, gla_wa2, gla_ba, gla_onorm_g, gla_w_o, ffn_w_up, ffn_conv_w, ffn_conv_b, ffn_w_down):
    P = _prepare(dict(
        norm_g=norm_g, sconv_w_in=sconv_w_in, sconv_conv_w=sconv_conv_w, sconv_w_out=sconv_w_out,
        rwkv_mu=rwkv_mu, rwkv_w_rkv=rwkv_w_rkv, rwkv_w0=rwkv_w0, rwkv_w1=rwkv_w1, rwkv_w2=rwkv_w2,
        rwkv_a0=rwkv_a0, rwkv_a1=rwkv_a1, rwkv_a2=rwkv_a2, rwkv_g1=rwkv_g1, rwkv_g2=rwkv_g2,
        rwkv_k_k=rwkv_k_k, rwkv_k_a=rwkv_k_a, rwkv_r_k=rwkv_r_k, rwkv_lnx_w=rwkv_lnx_w,
        rwkv_lnx_b=rwkv_lnx_b, rwkv_w_o=rwkv_w_o,
        gla_w_in=gla_w_in, gla_wa1=gla_wa1, gla_wa2=gla_wa2, gla_ba=gla_ba,
        gla_onorm_g=gla_onorm_g, gla_w_o=gla_w_o,
        ffn_w_up=ffn_w_up, ffn_conv_w=ffn_conv_w, ffn_conv_b=ffn_conv_b, ffn_w_down=ffn_w_down))
    bp, bs = x_prompt.shape[0], x_sample.shape[0]
    mod_p, mod_s = _modulation(c_prompt, c_sample, ada_w, ada_b)
    mod_p = mod_p.reshape(DEPTH, bp, N_MOD, D)
    mod_s = mod_s.reshape(DEPTH, 1, bs, N_MOD * D)
    ffn_wts = []
    y_s, ca_s, sb_s, wb_s, gc_s, cf_s = _trunk(
        x_sample.reshape(1, bs, D), mod_s, P,
        (state_conv_a, state_shift_b, state_wkv_b, state_gla_c, state_conv_ffn), ffn_wts, sample=True)
    y_p, ca_p, sb_p, wb_p, gc_p, cf_p = _trunk(x_prompt, mod_p, P, (None,) * 5, ffn_wts, sample=False)
    return (y_p, y_s.reshape(bs, 1, D), ca_p, ca_s, sb_p, sb_s, wb_p, wb_s, gc_p, gc_s, cf_p, cf_s)
```

```python
import functools

import jax
import jax.numpy as jnp
import numpy as np
from jax import lax
from jax.experimental import pallas as pl
from jax.experimental.pallas import tpu as pltpu

F32, BF16 = jnp.float32, jnp.bfloat16

D = 1024
DEPTH = 4
N_MOD = 6
RMS_EPS = 1e-6
RWKV_HEADS, RWKV_HEAD = 16, 64
RWKV_GN_EPS = 64e-5
DECAY_SCALE = float(np.exp(-0.5))
GLA_HEADS, GLA_DK, GLA_DV = 4, 128, 256
GLA_DK_TOTAL, GLA_DV_TOTAL = 512, 1024
GLA_GATE_NORMALIZER = 16.0
D_FF = 2816

LANES = 128
CARRY_ROWS = 8
CHUNK = 64
HALF = 256
FFN_CHUNK = 256
RWKV_CHUNKS_PER_STEP = 8
SEG_STRIDE = 16
VMEM_LIMIT = 56 * 1024 * 1024


def _rms(x, g):
    return x * lax.rsqrt(jnp.mean(x * x, axis=-1, keepdims=True) + RMS_EPS) * g


def _silu(x):
    return x * jax.nn.sigmoid(x)


def _softplus(x):
    return jnp.maximum(x, 0.0) + jnp.log1p(jnp.exp(-jnp.abs(x)))


_bdot = functools.partial(jnp.dot, preferred_element_type=F32)


def _dot_nt(a, b):
    return lax.dot_general(a, b, (((1,), (1,)), ((), ())), preferred_element_type=F32)


def _dot_tn(a, b):
    return lax.dot_general(a, b, (((0,), (0,)), ((), ())), preferred_element_type=F32)


def _cat0(*xs):
    return jnp.concatenate(xs, axis=0)


def _cat1(*xs):
    return jnp.concatenate(xs, axis=1)


def _split2(x):
    hi = x.astype(BF16)
    return hi, (x - hi.astype(F32)).astype(BF16)


def _split3(x):
    hi = x.astype(BF16)
    r1 = x - hi.astype(F32)
    mid = r1.astype(BF16)
    return hi, mid, (r1 - mid.astype(F32)).astype(BF16)


def _dot01(m01, x, terms):
    parts = (_split2 if terms == 2 else _split3)(x)
    acc = _bdot(m01, parts[0])
    for part in parts[1:]:
        acc = acc + _bdot(m01, part)
    return acc


def _segsum(x, seg1, seg2p):
    hi, lo = _split2(x)
    s = _bdot(hi, seg1) + _bdot(lo, seg1)
    s_hi, s_mid, s_lo = (part.astype(F32) for part in _split3(s))
    packed = s_hi + pltpu.roll(s_mid, SEG_STRIDE, axis=1) + pltpu.roll(s_lo, 2 * SEG_STRIDE, axis=1)
    return _bdot(packed.astype(BF16), seg2p)


def _mod(mod_ref, i, sample):
    return mod_ref[:, i * D:(i + 1) * D] if sample else mod_ref[i:i + 1, :]


def _shift_rows(x, k, carry):
    row = lax.broadcasted_iota(jnp.int32, (x.shape[0], 1), 0)
    y = pltpu.roll(x, k, axis=0)
    for j in range(k):
        src = CARRY_ROWS - k + j
        y = jnp.where(row == j, carry[src:src + 1, :], y)
    return y


def _half_rows(tm, sample):
    return [slice(0, tm)] if sample else [slice(0, tm // 2), slice(tm // 2, tm)]


def _interleave(gens):
    live = list(gens)
    while live:
        for g in list(live):
            if next(g, StopIteration) is StopIteration:
                live.remove(g)


def _whole(shape):
    nd = len(shape)
    return pl.BlockSpec(tuple(shape), lambda b, t: (0,) * nd, pipeline_mode=pl.Buffered(1))


def _layer(arr, l):
    nd = arr.ndim - 1
    return pl.BlockSpec((None,) + tuple(arr.shape[1:]), lambda b, t: (l,) + (0,) * nd,
                        pipeline_mode=pl.Buffered(1))


def _mod_spec(mod, l):
    return pl.BlockSpec((None, None) + tuple(mod.shape[2:]), lambda b, t: (l, b, 0, 0))


def _state_spec(state, l):
    return pl.BlockSpec((None,) + tuple(state.shape[1:]), lambda b, t: (l, 0, 0, 0))


def _new_state_spec(rows, width):
    return pl.BlockSpec((rows, 2, width), lambda b, t: (0, 0, 0))


def _rows(tm, width):
    return pl.BlockSpec((None, tm, width), lambda b, t: (b, t, 0))


def _per_seq(r, width):
    return pl.BlockSpec((None, r, width), lambda b, t: (b, 0, 0))


def _params(n=2):
    return pltpu.CompilerParams(dimension_semantics=("arbitrary",) * n, vmem_limit_bytes=VMEM_LIMIT)


def _np_bf16(a):
    return jnp.asarray(np.asarray(a, np.float32), BF16)


def _mod_kernel(cp_ref, cs_ref, w_ref, b_ref, op_ref, os_ref):
    w = w_ref[...].astype(BF16)
    op_ref[...] = _bdot(_silu(cp_ref[...]).astype(BF16), w) + b_ref[...]
    os_ref[...] = _bdot(_silu(cs_ref[...]).astype(BF16), w) + b_ref[...]


def _modulation(c_prompt, c_sample, ada_w, ada_b):
    bp, bs = c_prompt.shape[0], c_sample.shape[0]
    tn = 3072
    out = lambda n: (pl.BlockSpec((None, n, tn), lambda l, j: (l, 0, j)),
                     jax.ShapeDtypeStruct((DEPTH, n, N_MOD * D), F32))
    (sp_p, sh_p), (sp_s, sh_s) = out(bp), out(bs)
    return pl.pallas_call(
        _mod_kernel,
        grid=(DEPTH, N_MOD * D // tn),
        in_specs=[pl.BlockSpec((bp, D), lambda l, j: (0, 0)),
                  pl.BlockSpec((bs, D), lambda l, j: (0, 0)),
                  pl.BlockSpec((None, D, tn), lambda l, j: (l, 0, j)),
                  pl.BlockSpec((None, 1, tn), lambda l, j: (l, 0, j))],
        out_specs=[sp_p, sp_s], out_shape=[sh_p, sh_s],
        compiler_params=_params(),
        name="adaln_mod",
    )(c_prompt, c_sample, ada_w, ada_b.reshape(DEPTH, 1, N_MOD * D))


def _sconv_kernel(*refs, sample, tm, nt):
    if sample:
        h_ref, mod_ref, ng_ref, win_ref, cw_ref, wout_ref, st_ref, o_ref, nb_ref = refs
    else:
        h_ref, mod_ref, ng_ref, win_ref, cw_ref, wout_ref, o_ref, nb_ref, carry_ref = refs
        t = pl.program_id(1)

        @pl.when(t == 0)
        def _():
            carry_ref[...] = jnp.zeros_like(carry_ref)

    mod = lambda i: _mod(mod_ref, i, sample)
    cw = cw_ref[...]
    tails = {}

    def half(i, rows):
        n = rows.stop - rows.start
        h = h_ref[rows, :]
        u = (_rms(h, ng_ref[0:1, :]) * (1.0 + mod(1)) + mod(0)).astype(BF16)
        p = _bdot(u, win_ref[...])
        yield
        bg = p[:, :D]
        z = p[:, D:2 * D] * p[:, 2 * D:]
        if sample:
            y = st_ref[:, 0, :] * cw[0:1] + st_ref[:, 1, :] * cw[1:2] + z * cw[2:3]
            nb_ref[:, 0, :] = st_ref[:, 1, :]
            nb_ref[:, 1, :] = z
        else:
            c = carry_ref[...] if i == 0 else tails[i - 1]
            tails[i] = z[n - CARRY_ROWS:, :]
            y = _shift_rows(z, 2, c) * cw[0:1] + _shift_rows(z, 1, c) * cw[1:2] + z * cw[2:3]
        out = _bdot((bg * y).astype(BF16), wout_ref[...])
        yield
        o_ref[rows, :] = h + mod(2) * _rms(out, ng_ref[1:2, :])

    halves = _half_rows(tm, sample)
    _interleave([half(i, rows) for i, rows in enumerate(halves)])
    if not sample:
        carry_ref[...] = tails[len(halves) - 1]

        @pl.when(t == nt - 1)
        def _():
            nb_ref[...] = carry_ref[CARRY_ROWS - 2:, :]


def _sconv_layer(h, mod, l, j, P, state, *, sample):
    nb, rows, _ = h.shape
    tm = rows if sample else 4 * HALF
    nt = rows // tm
    consts = [(P["norm_g"], l), (P["sconv_w_in"], j), (P["sconv_conv_w"], j), (P["sconv_w_out"], j)]
    in_specs = [_rows(tm, D), _mod_spec(mod, l)] + [_layer(a, i) for a, i in consts]
    args = [h, mod] + [a for a, _ in consts]
    if sample:
        in_specs.append(_state_spec(state, j))
        args.append(state)
        nb_shape, nb_spec, scratch = (rows, 2, D), _new_state_spec(rows, D), []
    else:
        nb_shape, nb_spec = (nb, 2, D), _per_seq(2, D)
        scratch = [pltpu.VMEM((CARRY_ROWS, D), F32)]
    return pl.pallas_call(
        functools.partial(_sconv_kernel, sample=sample, tm=tm, nt=nt),
        grid=(nb, nt), in_specs=in_specs,
        out_specs=[_rows(tm, D), nb_spec],
        out_shape=[jax.ShapeDtypeStruct(h.shape, F32), jax.ShapeDtypeStruct(nb_shape, F32)],
        scratch_shapes=scratch, compiler_params=_params(),
        name="sconv_sample" if sample else "sconv_prompt",
    )(*args)


def _ffn_prompt_kernel(h_ref, mod_ref, ng_ref, wg_ref, wv_ref, cw_ref, cb_ref, wdn_ref, o_ref, nb_ref, carry_ref,
                       *, tm, nt):
    mod = lambda i: _mod(mod_ref, i, False)
    cw = cw_ref[...]
    t = pl.program_id(1)

    @pl.when(t == 0)
    def _():
        carry_ref[...] = jnp.zeros_like(carry_ref)

    def pre(rows):
        h = h_ref[rows, :]
        return h, (_rms(h, ng_ref[2:3, :]) * (1.0 + mod(4)) + mod(3)).astype(BF16)

    def act(hc, val):
        return (_silu(hc + cb_ref[...]) * val).astype(BF16)

    def post(rows, h, out):
        o_ref[rows, :] = h + mod(5) * _rms(out, ng_ref[3:4, :])

    def conv(g, c):
        return _shift_rows(g, 2, c) * cw[0:1] + _shift_rows(g, 1, c) * cw[1:2] + g * cw[2:3]

    n = tm // HALF
    rows = [slice(i * HALF, (i + 1) * HALF) for i in range(n)]
    h, u, g, acts, d = {}, {}, {}, {}, {}
    h[0], u[0] = pre(rows[0])
    g[0] = _bdot(u[0], wg_ref[...])
    for i in range(n):
        if i + 1 < n:
            h[i + 1], u[i + 1] = pre(rows[i + 1])
        v_i = _bdot(u[i], wv_ref[...])
        if i + 1 < n:
            g[i + 1] = _bdot(u[i + 1], wg_ref[...])
        acts[i] = act(conv(g[i], carry_ref[...] if i == 0 else g[i - 1][HALF - CARRY_ROWS:, :]), v_i)
        if i > 0:
            d[i - 1] = _bdot(acts[i - 1], wdn_ref[...])
        if i > 1:
            post(rows[i - 2], h[i - 2], d[i - 2])
    carry_ref[...] = g[n - 1][HALF - CARRY_ROWS:, :]
    d[n - 1] = _bdot(acts[n - 1], wdn_ref[...])
    if n > 1:
        post(rows[n - 2], h[n - 2], d[n - 2])
    post(rows[n - 1], h[n - 1], d[n - 1])

    @pl.when(t == nt - 1)
    def _():
        nb_ref[...] = carry_ref[CARRY_ROWS - 2:, :]


def _ffn_sample_kernel(h_ref, mod_ref, ng_ref, wg_ref, wv_ref, cw_ref, cb_ref, wd_ref, st_ref,
                       o_ref, nb_ref, wgb_ref, wvb_ref, wdb_ref, u_ref, acc_ref, *, nj):
    j = pl.program_id(0)
    mod = lambda i: _mod(mod_ref, i, True)

    @pl.when(j == 0)
    def _():
        u_ref[...] = (_rms(h_ref[...], ng_ref[2:3, :]) * (1.0 + mod(4)) + mod(3)).astype(BF16)
        acc_ref[...] = jnp.zeros_like(acc_ref)

    wg, wv, wd = wg_ref[...].astype(BF16), wv_ref[...].astype(BF16), wd_ref[...].astype(BF16)
    wgb_ref[...], wvb_ref[...], wdb_ref[...] = wg, wv, wd
    u = u_ref[...]
    g = _bdot(u, wg)
    val = _bdot(u, wv)
    cw = cw_ref[...]
    hc = st_ref[:, 0, :] * cw[0:1] + st_ref[:, 1, :] * cw[1:2] + g * cw[2:3]
    nb_ref[:, 0, :] = st_ref[:, 1, :]
    nb_ref[:, 1, :] = g
    acc_ref[...] += _bdot((_silu(hc + cb_ref[...]) * val).astype(BF16), wd)

    @pl.when(j == nj - 1)
    def _():
        o_ref[...] = h_ref[...] + mod(5) * _rms(acc_ref[...], ng_ref[3:4, :])


def _ffn_prompt_layer(h, mod, l, P, wts):
    nb, rows, _ = h.shape
    tm = 4 * HALF
    nt = rows // tm
    wg, wv, wd = wts
    return pl.pallas_call(
        functools.partial(_ffn_prompt_kernel, tm=tm, nt=nt),
        grid=(nb, nt),
        in_specs=[_rows(tm, D), _mod_spec(mod, l), _layer(P["norm_g"], l), _whole(wg.shape), _whole(wv.shape),
                  _layer(P["ffn_conv_w"], l), _layer(P["ffn_conv_b"], l), _whole(wd.shape)],
        out_specs=[_rows(tm, D), _per_seq(2, D_FF)],
        out_shape=[jax.ShapeDtypeStruct(h.shape, F32), jax.ShapeDtypeStruct((nb, 2, D_FF), F32)],
        scratch_shapes=[pltpu.VMEM((CARRY_ROWS, D_FF), F32)], compiler_params=_params(),
        name="ffn_prompt",
    )(h, mod, P["norm_g"], wg, wv, P["ffn_conv_w"], P["ffn_conv_b"], wd)


def _ffn_sample_layer(h, mod, l, P, state):
    _, rows, _ = h.shape
    c = FFN_CHUNK
    nj = D_FF // c
    w_up, w_down = P["ffn_w_up_f32"], P["ffn_w_down_f32"]
    bf = lambda shape: jax.ShapeDtypeStruct(shape, BF16)
    h_new, nbuf, wg, wv, wd = pl.pallas_call(
        functools.partial(_ffn_sample_kernel, nj=nj),
        grid=(nj,),
        in_specs=[pl.BlockSpec((None, rows, D), lambda j: (0, 0, 0)),
                  pl.BlockSpec((None, None) + tuple(mod.shape[2:]), lambda j: (l, 0, 0, 0)),
                  pl.BlockSpec((None, 4, D), lambda j: (l, 0, 0)),
                  pl.BlockSpec((None, D, c), lambda j: (l, 0, j)),
                  pl.BlockSpec((None, D, c), lambda j: (l, 0, nj + j)),
                  pl.BlockSpec((None, 3, c), lambda j: (l, 0, j)),
                  pl.BlockSpec((None, 1, c), lambda j: (l, 0, j)),
                  pl.BlockSpec((None, c, D), lambda j: (l, j, 0)),
                  pl.BlockSpec((None, rows, 2, c), lambda j: (l, 0, 0, j))],
        out_specs=[pl.BlockSpec((None, rows, D), lambda j: (0, 0, 0)),
                   pl.BlockSpec((rows, 2, c), lambda j: (0, 0, j)),
                   pl.BlockSpec((D, c), lambda j: (0, j)),
                   pl.BlockSpec((D, c), lambda j: (0, j)),
                   pl.BlockSpec((c, D), lambda j: (j, 0))],
        out_shape=[jax.ShapeDtypeStruct(h.shape, F32), jax.ShapeDtypeStruct((rows, 2, D_FF), F32),
                   bf((D, D_FF)), bf((D, D_FF)), bf((D_FF, D))],
        scratch_shapes=[pltpu.VMEM((rows, D), BF16), pltpu.VMEM((rows, D), F32)],
        compiler_params=_params(1),
        name="ffn_sample",
    )(h, mod, P["norm_g"], w_up, w_up, P["ffn_conv_w"], P["ffn_conv_b"], w_down, state)
    return h_new, nbuf, (wg, wv, wd)


def _rwkv_pre_kernel(*refs, sample, tm, nt):
    (h_ref, mod_ref, ng_ref, mu_ref, wrkv_ref, w0_ref, w1_ref, w2_ref, a0_ref, a1_ref, a2_ref,
     g1_ref, g2_ref, kk_ref, ka_ref, rk_ref, seg1_ref, seg2_ref) = refs[:18]
    if sample:
        shift_ref = refs[18]
        outs = refs[19:]
    else:
        tri_ref, sel_ref = refs[18:20]
        outs = refs[20:-1]
        carry_ref = refs[-1]
        t = pl.program_id(1)

        @pl.when(t == 0)
        def _():
            carry_ref[...] = jnp.zeros_like(carry_ref)

    at_ref, bt_ref, kt_ref, rt_ref, v_ref, g_ref, bonus_ref, wc_ref, sh_ref = outs
    mod = lambda i: _mod(mod_ref, i, sample)
    seg = lambda x: _segsum(x, seg1_ref[...], seg2_ref[...])
    tails = {}

    def half(i, rows):
        n = rows.stop - rows.start

        def put(ref, val):
            if sample:
                ref[...] = val.T
            else:
                ref[rows, :] = val

        u = _rms(h_ref[rows, :], ng_ref[0:1, :]) * (1.0 + mod(1)) + mod(0)
        if sample:
            prev = shift_ref[...]
            sh_ref[...] = u
        else:
            tails[i] = u[n - CARRY_ROWS:, :]
            prev = _shift_rows(u, 1, carry_ref[...] if i == 0 else tails[i - 1])
        ub, xb = u.astype(BF16), (prev - u).astype(BF16)
        mix = lambda m: ub + xb * mu_ref[m:m + 1, :].astype(BF16)
        r = _bdot(mix(0), wrkv_ref[0])
        k = _bdot(mix(2), wrkv_ref[1])
        v = _bdot(mix(3), wrkv_ref[2])
        zw = _bdot(mix(1), w1_ref[...])
        za = _bdot(mix(4), a1_ref[...])
        zg = _bdot(mix(5), g1_ref[...])
        yield
        put(v_ref, v)
        z = w0_ref[...] + _bdot(jnp.tanh(zw).astype(BF16), w2_ref[...])
        a = jax.nn.sigmoid(a0_ref[...] + _bdot(za.astype(BF16), a2_ref[...]))
        g_ref[rows, :] = _bdot(jax.nn.sigmoid(zg).astype(BF16), g2_ref[...])
        lw = -DECAY_SCALE * jax.nn.sigmoid(z)
        kk = k * kk_ref[...]
        k2 = k * (1.0 + (a - 1.0) * ka_ref[...])
        ss = seg(kk * kk)
        rk = seg(r * k2 * rk_ref[...])
        if sample:
            lc = lw
        else:
            lc = _dot01(tri_ref[...], lw, 2)
        yield
        kk = kk * jnp.minimum(lax.rsqrt(ss), 1e12)
        bonus_ref[rows, :] = rk * v
        if sample:
            wc_ref[...] = jnp.exp(lw).T
        else:
            wc_ref[i] = jnp.exp(_dot01(sel_ref[...], lc, 3))
        e_neg = jnp.exp(-lc)
        put(at_ref, -kk * jnp.exp(lc - lw))
        put(bt_ref, kk * a * e_neg)
        put(kt_ref, k2 * e_neg)
        put(rt_ref, (r * jnp.exp(lc)).astype(rt_ref.dtype))

    halves = _half_rows(tm, sample)
    _interleave([half(i, rows) for i, rows in enumerate(halves)])
    if not sample:
        carry_ref[...] = tails[len(halves) - 1]

        @pl.when(t == nt - 1)
        def _():
            sh_ref[...] = carry_ref[CARRY_ROWS - 1:, :]


def _rwkv_core_kernel(at_ref, bt_ref, kt_ref, rt_ref, v_ref, wc_ref, y_ref, sout_ref, s_ref, *, nt, nchunk):
    t = pl.program_id(1)

    @pl.when(t == 0)
    def _():
        s_ref[...] = jnp.zeros_like(s_ref)

    n = RWKV_HEAD
    lane_lo = lax.broadcasted_iota(jnp.int32, (1, LANES), 1) < n
    ri = lax.broadcasted_iota(jnp.int32, (CHUNK, LANES), 0)
    ci = lax.broadcasted_iota(jnp.int32, (CHUNK, LANES), 1) & (n - 1)
    m_strict = ci < ri
    m_incl = ci <= ri
    eye = (ci == ri).astype(F32)
    shift = n.bit_length() - 1
    m_bd = ((lax.broadcasted_iota(jnp.int32, (LANES, LANES), 0) >> shift)
            == (lax.broadcasted_iota(jnp.int32, (LANES, LANES), 1) >> shift))

    def bd(x):
        z = jnp.zeros_like(x)
        return _cat0(jnp.where(lane_lo, x, z), jnp.where(lane_lo, z, x))

    bd2 = lambda xs: (bd(xs[0]), bd(xs[1]))

    zero = jnp.zeros((LANES, LANES), BF16)
    halves = lambda w: w[:, :LANES] + w[:, LANES:]

    def rhs3(y):
        return _cat0(_cat1(y[0], y[1]), _cat1(y[0], zero))

    def rhs3_t(y):
        return _cat0(_cat1(y[0], y[0]), _cat1(y[1], zero))

    pairs = range(RWKV_HEADS // 2)
    lanes = [slice(p * LANES, (p + 1) * LANES) for p in pairs]
    npair = len(lanes)
    units = range(nchunk * npair)
    tok = [slice((i // npair) * CHUNK, (i // npair + 1) * CHUNK) for i in units]
    ln = [lanes[i % npair] for i in units]
    a_k = [_cat1(*_split2(at_ref[tok[i], ln[i]])) for i in units]
    b_s = [_split2(bt_ref[tok[i], ln[i]]) for i in units]
    k_s = [_split2(kt_ref[tok[i], ln[i]]) for i in units]
    v_s = [_split2(v_ref[tok[i], ln[i]]) for i in units]
    b_bd = [bd2(b_s[i]) for i in units]
    k_bd = [bd2(k_s[i]) for i in units]
    v_bd = [bd2(v_s[i]) for i in units]
    r_s = [rt_ref[tok[i], ln[i]] for i in units]
    bk_hi = [_cat0(b_s[i][0], k_s[i][0]) for i in units]
    a_ak = [_split2(jnp.where(m_strict, halves(_dot_nt(a_k[i], rhs3_t(k_bd[i]))), 0.0)) for i in units]
    sc_r = [_dot_nt(r_s[i], _cat0(b_bd[i][0], k_bd[i][0])) for i in units]
    a_r = [_cat1(jnp.where(m_incl, sc_r[i][:, :LANES], 0.0).astype(BF16),
                 jnp.where(m_incl, sc_r[i][:, LANES:], 0.0).astype(BF16)) for i in units]
    pw = [jnp.where(m_strict, halves(_dot_nt(a_k[i], rhs3_t(b_bd[i]))), 0.0) for i in units]
    tinv = [eye + pw[i] for i in units]
    pws = [_split2(pw[i]) for i in units]
    pw = [halves(_bdot(_cat1(*pws[i]), rhs3(bd2(pws[i])))) for i in units]
    for step in range(5):
        pws = [_split2(pw[i]) for i in units]
        tis = [_split2(tinv[i]) for i in units]
        rhs = [rhs3(bd2(pws[i])) for i in units]
        if step < 4:
            w = [_bdot(_cat0(_cat1(*pws[i]), _cat1(*tis[i])), rhs[i]) for i in units]
            pw = [halves(w[i][:CHUNK]) for i in units]
            tinv = [tinv[i] + halves(w[i][CHUNK:]) for i in units]
        else:
            tinv = [tinv[i] + halves(_bdot(_cat1(*tis[i]), rhs[i])) for i in units]
    tis = [_cat1(*_split2(tinv[i])) for i in units]
    s = [s_ref[p] for p in pairs]
    for c in range(nchunk):
        un = [c * npair + p for p in pairs]
        ss = [_split2(s[p]) for p in pairs]
        x = [halves(_dot_nt(a_k[i], rhs3_t(ss[p]))) + halves(_bdot(_cat1(*a_ak[i]), rhs3(v_bd[i])))
             for p, i in enumerate(un)]
        ub = [halves(_bdot(tis[i], rhs3(bd2(_split2(x[p]))))).astype(BF16) for p, i in enumerate(un)]
        for p, i in enumerate(un):
            y_ref[tok[i], lanes[p]] = (_dot_nt(r_s[i], ss[p][0])
                                       + _bdot(a_r[i], _cat0(bd(ub[p]), v_bd[i][0])))
        s = [jnp.where(m_bd, s[p] + _dot_tn(_cat0(ub[p], v_s[i][0]), bk_hi[i]), 0.0) * wc_ref[c][:, lanes[p]]
             for p, i in enumerate(un)]
    for p in pairs:
        s_ref[p] = s[p]

    @pl.when(t == nt - 1)
    def _():
        for p in pairs:
            sout_ref[2 * p] = s[p][:n, :n]
            sout_ref[2 * p + 1] = pltpu.roll(s[p], n, axis=1)[n:, :n]


def _rwkv_core_sample_kernel(at_ref, bt_ref, kt_ref, rt_ref, v_ref, wc_ref, s_ref, y_ref, sout_ref):
    s = s_ref[...]
    u = jnp.sum(s * at_ref[...][None, :, :], axis=1)
    s1 = s + u[:, None, :] * bt_ref[...][None, :, :] + v_ref[...][:, None, :] * kt_ref[...][None, :, :]
    y_ref[...] = jnp.sum(s1 * rt_ref[...][None, :, :], axis=1)
    sout_ref[...] = s1 * wc_ref[...][None, :, :]


def _rwkv_post_kernel(y_ref, g_ref, bonus_ref, h_ref, mod_ref, ng_ref, lnw_ref, lnb_ref, wo_ref,
                      seg1_ref, seg2_ref, o_ref, *, sample, tm):
    mod = lambda i: _mod(mod_ref, i, sample)
    seg = lambda x: _segsum(x, seg1_ref[...], seg2_ref[...])

    def half(rows):
        y = y_ref[rows, :]
        mean = seg(y) * (1.0 / RWKV_HEAD)
        yield
        yc = y - mean
        var = seg(yc * yc) * (1.0 / RWKV_HEAD)
        yield
        yn = yc * lax.rsqrt(var + RWKV_GN_EPS) * lnw_ref[...] + lnb_ref[...] + bonus_ref[rows, :]
        out = _bdot((yn * g_ref[rows, :]).astype(BF16), wo_ref[...])
        yield
        o_ref[rows, :] = h_ref[rows, :] + mod(2) * _rms(out, ng_ref[1:2, :])

    _interleave([half(rows) for rows in _half_rows(tm, sample)])


def _rwkv_layer(h, mod, l, j, P, state_shift, state_wkv, *, sample):
    nb, rows, _ = h.shape
    tm = rows if sample else 2 * HALF
    nt = rows // tm
    names = ("rwkv_mu", "rwkv_w_rkv", "rwkv_w0", "rwkv_w1", "rwkv_w2", "rwkv_a0", "rwkv_a1", "rwkv_a2",
             "rwkv_g1", "rwkv_g2", "rwkv_k_k", "rwkv_k_a", "rwkv_r_k")
    consts = [(P["norm_g"], l)] + [(P[n], j) for n in names]
    seg1, seg2p = P["seg1"], P["seg2p"]
    in_specs = ([_rows(tm, D), _mod_spec(mod, l)] + [_layer(a, i) for a, i in consts]
                + [_whole(seg1.shape), _whole(seg2p.shape)])
    args = [h, mod] + [a for a, _ in consts] + [seg1, seg2p]
    if sample:
        in_specs.append(_per_seq(rows, D))
        args.append(state_shift)
        act_shape, act_spec = (nb, D, rows), pl.BlockSpec((None, D, rows), lambda b, t: (b, 0, 0))
        wc_shape, wc_spec = act_shape, act_spec
        sh_shape, sh_spec = (nb, rows, D), _rows(tm, D)
        scratch = []
    else:
        ti = np.arange(HALF)
        tri = _np_bf16((ti[:, None] // CHUNK == ti[None, :] // CHUNK) & (ti[None, :] <= ti[:, None]))
        sel = _np_bf16(ti[None, :] == (np.arange(CARRY_ROWS)[:, None] * CHUNK + CHUNK - 1))
        in_specs += [_whole(tri.shape), _whole(sel.shape)]
        args += [tri, sel]
        act_shape, act_spec = h.shape, _rows(tm, D)
        wc_shape = (nb, rows // HALF, CARRY_ROWS, D)
        wc_spec = pl.BlockSpec((None, tm // HALF, CARRY_ROWS, D), lambda b, t: (b, t, 0, 0))
        sh_shape, sh_spec = (nb, 1, D), _per_seq(1, D)
        scratch = [pltpu.VMEM((CARRY_ROWS, D), F32)]
    f32 = jax.ShapeDtypeStruct(h.shape, F32)
    act = jax.ShapeDtypeStruct(act_shape, F32)
    rt_shape = act if sample else jax.ShapeDtypeStruct(h.shape, BF16)
    at, bt, kt, rt, vv, g, bonus, wc, new_shift = pl.pallas_call(
        functools.partial(_rwkv_pre_kernel, sample=sample, tm=tm, nt=nt),
        grid=(nb, nt), in_specs=in_specs,
        out_specs=[act_spec] * 5 + [_rows(tm, D)] * 2 + [wc_spec, sh_spec],
        out_shape=[act, act, act, rt_shape, act, f32, f32, jax.ShapeDtypeStruct(wc_shape, F32),
                   jax.ShapeDtypeStruct(sh_shape, F32)],
        scratch_shapes=scratch, compiler_params=_params(),
        name="rwkv_pre_sample" if sample else "rwkv_pre_prompt",
    )(*args)

    hh, n = RWKV_HEADS, RWKV_HEAD
    if sample:
        hv = lambda x: x.reshape(hh, n, rows)
        vec_spec = pl.BlockSpec((None, n, rows), lambda i: (i, 0, 0))
        st_spec = pl.BlockSpec((None, n, n, rows), lambda i: (i, 0, 0, 0))
        y_t, s_t = pl.pallas_call(
            _rwkv_core_sample_kernel,
            grid=(hh,),
            in_specs=[vec_spec] * 6 + [st_spec],
            out_specs=[vec_spec, st_spec],
            out_shape=[jax.ShapeDtypeStruct((hh, n, rows), F32), jax.ShapeDtypeStruct((hh, n, n, rows), F32)],
            compiler_params=_params(1),
            name="rwkv_core_sample",
        )(hv(at), hv(bt), hv(kt), hv(rt), hv(vv), hv(wc), jnp.transpose(state_wkv, (1, 2, 3, 0)))
        y = y_t.reshape(D, rows).T.reshape(nb, rows, D)
        new_s = jnp.transpose(s_t, (3, 0, 1, 2))
    else:
        ncht = rows // CHUNK
        wc_rows = wc[:, :, :HALF // CHUNK, :].reshape(nb, ncht, 1, D)
        npair = hh // 2
        nc = RWKV_CHUNKS_PER_STEP
        y, new_s = pl.pallas_call(
            functools.partial(_rwkv_core_kernel, nt=ncht // nc, nchunk=nc),
            grid=(nb, ncht // nc),
            in_specs=[_rows(nc * CHUNK, D)] * 5 + [pl.BlockSpec((None, nc, 1, D), lambda b, t: (b, t, 0, 0))],
            out_specs=[_rows(nc * CHUNK, D), pl.BlockSpec((None, hh, n, n), lambda b, t: (b, 0, 0, 0))],
            out_shape=[f32, jax.ShapeDtypeStruct((nb, hh, n, n), F32)],
            scratch_shapes=[pltpu.VMEM((npair, LANES, LANES), F32)],
            compiler_params=_params(),
            name="rwkv_core_prompt",
        )(at, bt, kt, rt, vv, wc_rows)

    consts = [(P["norm_g"], l), (P["rwkv_lnx_w"], j), (P["rwkv_lnx_b"], j), (P["rwkv_w_o"], j)]
    h_new = pl.pallas_call(
        functools.partial(_rwkv_post_kernel, sample=sample, tm=tm),
        grid=(nb, nt),
        in_specs=([_rows(tm, D)] * 4 + [_mod_spec(mod, l)] + [_layer(a, i) for a, i in consts]
                  + [_whole(seg1.shape), _whole(seg2p.shape)]),
        out_specs=_rows(tm, D), out_shape=f32, compiler_params=_params(),
        name="rwkv_post_sample" if sample else "rwkv_post_prompt",
    )(y, g, bonus, h, mod, *[a for a, _ in consts], seg1, seg2p)
    return h_new, new_shift, new_s


GLA_LEVELS = (1, 2, 4, 8, 16, 32)


def _gla_pre_stages(h_ref, mod_ref, ng_ref, win_ref, wa1_ref, wa2_ref, ba_ref,
                    q_ref, k_ref, v_ref, og_ref, lg_ref, rows, sample):
    mod = lambda i: _mod(mod_ref, i, sample)
    dk, dv = GLA_DK_TOTAL, GLA_DV_TOTAL
    u = (_rms(h_ref[rows, :], ng_ref[0:1, :]) * (1.0 + mod(1)) + mod(0)).astype(BF16)
    p = _bdot(u, win_ref[...])
    gate = _bdot(_bdot(u, wa1_ref[...]).astype(BF16), wa2_ref[...]) + ba_ref[...]
    yield
    q_ref[rows, :] = p[:, :dk] * (GLA_DK ** -0.5)
    k_ref[rows, :] = p[:, dk:2 * dk]
    v_ref[rows, :] = p[:, 2 * dk:2 * dk + dv]
    og_ref[rows, :] = p[:, 2 * dk + dv:]
    lg_ref[rows, :] = -_softplus(-gate) * (1.0 / GLA_GATE_NORMALIZER)


def _gla_pre_kernel(*refs, sample, tm):
    _interleave([_gla_pre_stages(*refs, rows, sample) for rows in _half_rows(tm, sample)])


def _gla_core_stages(q_ref, k_ref, v_ref, lg_ref, lvl_ref, o_ref, state, row0, nchunk):
    row = lax.broadcasted_iota(jnp.int32, (CHUNK, 1), 0)
    ri = lax.broadcasted_iota(jnp.int32, (CHUNK, CHUNK), 0)
    ci = lax.broadcasted_iota(jnp.int32, (CHUNK, CHUNK), 1)
    heads = range(GLA_HEADS)
    units = range(nchunk * GLA_HEADS)
    chunk_rows = [slice(row0 + c * CHUNK, row0 + (c + 1) * CHUNK) for c in range(nchunk)]
    tok = [chunk_rows[i // GLA_HEADS] for i in units]
    sl = [slice((i % GLA_HEADS) * GLA_DK, (i % GLA_HEADS + 1) * GLA_DK) for i in units]
    sv = [slice((i % GLA_HEADS) * GLA_DV, (i % GLA_HEADS + 1) * GLA_DV) for i in units]
    lg = [lg_ref[chunk_rows[c], :] for c in range(nchunk)]
    pq = [_bdot(lvl_ref[...], _cat0(*_split3(lg[c]))) for c in range(nchunk)]
    q = [q_ref[tok[i], sl[i]] for i in units]
    k = [k_ref[tok[i], sl[i]] for i in units]
    vb = [v_ref[tok[i], sv[i]].astype(BF16) for i in units]
    yield
    a = [jnp.where(ri == ci, _dot_nt(q[i].astype(BF16), k[i].astype(BF16)), 0.0) for i in units]
    for li, m in enumerate(GLA_LEVELS):
        lm = m.bit_length() - 1
        second = ((row >> lm) & 1) == 1
        block = (ri >> (lm + 1)) == (ci >> (lm + 1))
        base = (li - 1) * 2 * CHUNK
        for i in units:
            c = i // GLA_HEADS
            if m == 1:
                qe = jnp.where(second, q[i] * jnp.exp(lg[c][:, sl[i]]), 0.0)
                ke = jnp.where(second, 0.0, k[i])
            else:
                qe = jnp.where(second, q[i] * jnp.exp(pq[c][base:base + CHUNK, sl[i]]), 0.0)
                ke = jnp.where(second, 0.0, k[i] * jnp.exp(pq[c][base + CHUNK:base + 2 * CHUNK, sl[i]]))
            a[i] = a[i] + jnp.where(block, _dot_nt(qe.astype(BF16), ke.astype(BF16)), 0.0)
        yield
    base = (len(GLA_LEVELS) - 1) * 2 * CHUNK
    g_inc = [pq[i // GLA_HEADS][base:base + CHUNK, sl[i]] for i in units]
    g_rest = [pq[i // GLA_HEADS][base + CHUNK:base + 2 * CHUNK, sl[i]] for i in units]
    av = [_bdot(a[i].astype(BF16), vb[i]) for i in units]
    qg = [(q[i] * jnp.exp(g_inc[i])).astype(BF16) for i in units]
    kg = [(k[i] * jnp.exp(g_rest[i])).astype(BF16) for i in units]
    kv = [_dot_tn(vb[i], kg[i]) for i in units]
    yield
    s = state["s"]
    for c in range(nchunk):
        for hd in heads:
            i = c * GLA_HEADS + hd
            o_ref[tok[i], sv[i]] = av[i] + _dot_nt(qg[i], s[hd].astype(BF16))
        s = [s[hd] * jnp.exp(g_inc[c * GLA_HEADS + hd][CHUNK - 1:CHUNK, :]) + kv[c * GLA_HEADS + hd]
             for hd in heads]
        yield
    state["s"] = s


def _gla_core_sample_kernel(q_ref, k_ref, v_ref, lg_ref, s_ref, o_ref, sout_ref):
    n = GLA_DK
    eye = (lax.broadcasted_iota(jnp.int32, (n, n), 0) == lax.broadcasted_iota(jnp.int32, (n, n), 1)).astype(F32)
    col = lambda x: jnp.sum(eye * x, axis=-1, keepdims=True)
    q, k, v = q_ref[...], k_ref[...], v_ref[...]
    s = s_ref[...]
    decay = jnp.exp(lg_ref[...])
    qk = jnp.sum(q * k, axis=-1, keepdims=True)
    o_ref[...] = qk * v + jnp.sum(col(q * decay) * s, axis=-2, keepdims=True)
    sout_ref[...] = col(decay) * s + col(k) * v


def _gla_post_stages(o_ref, og_ref, h_ref, mod_ref, ng_ref, on_ref, wo_ref, out_ref, rows, sample):
    mod = lambda i: _mod(mod_ref, i, sample)
    parts = []
    for hd in range(GLA_HEADS):
        sv = slice(hd * GLA_DV, (hd + 1) * GLA_DV)
        parts.append((_rms(o_ref[rows, sv], on_ref[...]) * _silu(og_ref[rows, sv])).astype(BF16))
    out = _bdot(_cat1(*parts), wo_ref[...])
    yield
    out_ref[rows, :] = h_ref[rows, :] + mod(2) * _rms(out, ng_ref[1:2, :])


def _gla_post_kernel(o_ref, *refs, sample):
    _interleave([_gla_post_stages(o_ref, *refs, slice(0, o_ref.shape[0]), sample)])


def _gla_level_matrix():
    ti = np.arange(CHUNK)
    blocks = []
    for m in GLA_LEVELS[1:] + (CHUNK,):
        same = ti[:, None] // m == ti[None, :] // m
        blocks.append(same & (ti[None, :] <= ti[:, None]))
        blocks.append(same & (ti[None, :] > ti[:, None]))
    one = np.concatenate(blocks, axis=0)
    return _np_bf16(np.concatenate([one, one, one], axis=1))


def _gla_fused_kernel(h_ref, mod_ref, ng_ref, win_ref, wa1_ref, wa2_ref, ba_ref, on_ref, wo_ref, lvl_ref,
                      out_ref, sout_ref, q_s, k_s, v_s, og_s, lg_s, o_s, s_ref, *, tm, nt):
    t = pl.program_id(1)

    @pl.when(t == 0)
    def _():
        s_ref[...] = jnp.zeros_like(s_ref)

    half = tm // 2
    rows_a, rows_b = slice(0, half), slice(half, tm)
    state = {"s": [s_ref[hd] for hd in range(GLA_HEADS)]}
    pre = lambda rows: _gla_pre_stages(h_ref, mod_ref, ng_ref, win_ref, wa1_ref, wa2_ref, ba_ref,
                                       q_s, k_s, v_s, og_s, lg_s, rows, False)
    core = lambda row0: _gla_core_stages(q_s, k_s, v_s, lg_s, lvl_ref, o_s, state, row0, half // CHUNK)
    post = lambda rows: _gla_post_stages(o_s, og_s, h_ref, mod_ref, ng_ref, on_ref, wo_ref, out_ref, rows, False)
    _interleave([pre(rows_a)])
    _interleave([core(0), pre(rows_b)])
    _interleave([core(half), post(rows_a)])
    _interleave([post(rows_b)])
    for hd in range(GLA_HEADS):
        s_ref[hd] = state["s"][hd]

    @pl.when(t == nt - 1)
    def _():
        sout_ref[...] = s_ref[...]


def _gla_prompt_layer(h, mod, l, j, P):
    nb, rows, _ = h.shape
    tm = 2 * HALF
    nt = rows // tm
    dk, dv, hh = GLA_DK_TOTAL, GLA_DV_TOTAL, GLA_HEADS
    lvl = _gla_level_matrix()
    consts = [(P["norm_g"], l), (P["gla_w_in"], j), (P["gla_wa1"], j), (P["gla_wa2"], j), (P["gla_ba"], j),
              (P["gla_onorm_g"], j), (P["gla_w_o"], j)]
    h_new, s_t = pl.pallas_call(
        functools.partial(_gla_fused_kernel, tm=tm, nt=nt),
        grid=(nb, nt),
        in_specs=[_rows(tm, D), _mod_spec(mod, l)] + [_layer(a, i) for a, i in consts] + [_whole(lvl.shape)],
        out_specs=[_rows(tm, D), pl.BlockSpec((None, hh, GLA_DV, GLA_DK), lambda b, t: (b, 0, 0, 0))],
        out_shape=[jax.ShapeDtypeStruct(h.shape, F32), jax.ShapeDtypeStruct((nb, hh, GLA_DV, GLA_DK), F32)],
        scratch_shapes=[pltpu.VMEM((tm, dk), F32), pltpu.VMEM((tm, dk), F32), pltpu.VMEM((tm, dv), F32),
                        pltpu.VMEM((tm, dv), F32), pltpu.VMEM((tm, dk), F32), pltpu.VMEM((tm, dv), F32),
                        pltpu.VMEM((hh, GLA_DV, GLA_DK), F32)],
        compiler_params=_params(),
        name="gla_prompt",
    )(h, mod, *[a for a, _ in consts], lvl)
    return h_new, jnp.swapaxes(s_t, -1, -2)


def _gla_layer(h, mod, l, j, P, state, *, sample):
    if not sample:
        return _gla_prompt_layer(h, mod, l, j, P)
    nb, rows, _ = h.shape
    tm = rows
    nt = rows // tm
    dk, dv, hh = GLA_DK_TOTAL, GLA_DV_TOTAL, GLA_HEADS
    consts = [(P["norm_g"], l), (P["gla_w_in"], j), (P["gla_wa1"], j), (P["gla_wa2"], j), (P["gla_ba"], j)]
    shp = lambda width: jax.ShapeDtypeStruct((nb, rows, width), F32)
    q, k, v, og, lg = pl.pallas_call(
        functools.partial(_gla_pre_kernel, sample=sample, tm=tm),
        grid=(nb, nt),
        in_specs=[_rows(tm, D), _mod_spec(mod, l)] + [_layer(a, i) for a, i in consts],
        out_specs=[_rows(tm, dk), _rows(tm, dk), _rows(tm, dv), _rows(tm, dv), _rows(tm, dk)],
        out_shape=[shp(dk), shp(dk), shp(dv), shp(dv), shp(dk)],
        compiler_params=_params(),
        name="gla_pre_sample" if sample else "gla_pre_prompt",
    )(h, mod, *[a for a, _ in consts])

    if sample:
        sb = 8
        nseq = rows
        hk = lambda x: x.reshape(nseq, hh, 1, GLA_DK)
        kspec = pl.BlockSpec((sb, hh, 1, GLA_DK), lambda i: (i, 0, 0, 0))
        vspec = pl.BlockSpec((sb, hh, 1, GLA_DV), lambda i: (i, 0, 0, 0))
        sspec = pl.BlockSpec((sb, hh, GLA_DK, GLA_DV), lambda i: (i, 0, 0, 0))
        o, new_s = pl.pallas_call(
            _gla_core_sample_kernel,
            grid=(nseq // sb,),
            in_specs=[kspec, kspec, vspec, kspec, sspec],
            out_specs=[vspec, sspec],
            out_shape=[jax.ShapeDtypeStruct((nseq, hh, 1, GLA_DV), F32),
                       jax.ShapeDtypeStruct((nseq, hh, GLA_DK, GLA_DV), F32)],
            compiler_params=_params(1),
            name="gla_core_sample",
        )(hk(q), hk(k), v.reshape(nseq, hh, 1, GLA_DV), hk(lg), state)
        o = o.reshape(nb, rows, dv)

    tp = rows
    consts = [(P["norm_g"], l), (P["gla_onorm_g"], j), (P["gla_w_o"], j)]
    h_new = pl.pallas_call(
        functools.partial(_gla_post_kernel, sample=sample),
        grid=(nb, rows // tp),
        in_specs=[_rows(tp, dv), _rows(tp, dv), _rows(tp, D), _mod_spec(mod, l)] + [_layer(a, i) for a, i in consts],
        out_specs=_rows(tp, D), out_shape=shp(D), compiler_params=_params(),
        name="gla_post_sample" if sample else "gla_post_prompt",
    )(o, og, h, mod, *[a for a, _ in consts])
    return h_new, new_s


def _trunk(h, mod, P, states, ffn_wts, *, sample):
    st_a, st_sb, st_wb, st_gc, st_f = states
    nb, rows, _ = h.shape
    new_a, new_sb, new_wb, new_gc, new_f = [], [], [], [], []
    for l in range(DEPTH):
        kind, j = l % 3, l // 3
        if kind == 0:
            h, nbuf = _sconv_layer(h, mod, l, j, P, st_a, sample=sample)
            new_a.append(nbuf)
        elif kind == 1:
            sh = st_sb[j].reshape(1, rows, D) if sample else None
            wkv = st_wb[j] if sample else None
            h, nsh, ns = _rwkv_layer(h, mod, l, j, P, sh, wkv, sample=sample)
            new_sb.append(nsh.reshape(rows, D) if sample else nsh.reshape(nb, D))
            new_wb.append(ns)
        else:
            h, ns = _gla_layer(h, mod, l, j, P, st_gc[j] if sample else None, sample=sample)
            new_gc.append(ns)
        if sample:
            h, nbuf, wts = _ffn_sample_layer(h, mod, l, P, st_f)
            ffn_wts.append(wts)
        else:
            h, nbuf = _ffn_prompt_layer(h, mod, l, P, ffn_wts[l])
        new_f.append(nbuf)
    return (h, jnp.stack(new_a), jnp.stack(new_sb), jnp.stack(new_wb), jnp.stack(new_gc), jnp.stack(new_f))


def _prepare(p):
    bf = lambda x: x.astype(BF16)
    row = lambda x: x.reshape(x.shape[0], 1, -1)
    pad_c = lambda x: bf(jnp.pad(x, ((0, 0), (0, 0), (0, LANES - x.shape[2]))))
    pad_r = lambda x: bf(jnp.pad(x, ((0, 0), (0, LANES - x.shape[1]), (0, 0))))
    lane_head = np.arange(D) // RWKV_HEAD
    col = np.arange(LANES)
    P = dict(
        norm_g=p["norm_g"],
        sconv_w_in=bf(p["sconv_w_in"]), sconv_conv_w=p["sconv_conv_w"], sconv_w_out=bf(p["sconv_w_out"]),
        rwkv_mu=p["rwkv_mu"], rwkv_w_rkv=bf(p["rwkv_w_rkv"]), rwkv_w0=row(p["rwkv_w0"]),
        rwkv_w1=pad_c(p["rwkv_w1"]), rwkv_w2=pad_r(p["rwkv_w2"]), rwkv_a0=row(p["rwkv_a0"]),
        rwkv_a1=pad_c(p["rwkv_a1"]), rwkv_a2=pad_r(p["rwkv_a2"]), rwkv_g1=bf(p["rwkv_g1"]), rwkv_g2=bf(p["rwkv_g2"]),
        rwkv_k_k=row(p["rwkv_k_k"]), rwkv_k_a=row(p["rwkv_k_a"]), rwkv_r_k=row(p["rwkv_r_k"]),
        rwkv_lnx_w=row(p["rwkv_lnx_w"]), rwkv_lnx_b=row(p["rwkv_lnx_b"]), rwkv_w_o=bf(p["rwkv_w_o"]),
        gla_w_in=bf(p["gla_w_in"]), gla_wa1=pad_c(p["gla_wa1"]), gla_wa2=pad_r(p["gla_wa2"]),
        gla_ba=row(p["gla_ba"]), gla_onorm_g=row(p["gla_onorm_g"]), gla_w_o=bf(p["gla_w_o"]),
        ffn_w_up_f32=p["ffn_w_up"], ffn_conv_w=p["ffn_conv_w"], ffn_conv_b=row(p["ffn_conv_b"]),
        ffn_w_down_f32=p["ffn_w_down"],
        seg1=_np_bf16(lane_head[:, None] == col[None, :]),
        seg2p=_np_bf16((col[:, None] % SEG_STRIDE == lane_head[None, :]) & (col[:, None] < 3 * SEG_STRIDE)),
    )
    return P


def kernel(x_prompt, x_sample, state_conv_a, state_shift_b, state_wkv_b, state_gla_c, state_conv_ffn, c_prompt, c_sample, ada_w, ada_b, norm_g, sconv_w_in, sconv_conv_w, sconv_w_out, rwkv_mu, rwkv_w_rkv, rwkv_w0, rwkv_w1, rwkv_w2, rwkv_a0, rwkv_a1, rwkv_a2, rwkv_g1, rwkv_g2, rwkv_k_k, rwkv_k_a, rwkv_r_k, rwkv_lnx_w, rwkv_lnx_b, rwkv_w_o, gla_w_in, gla_wa1, gla_wa2, gla_ba, gla_onorm_g, gla_w_o, ffn_w_up, ffn_conv_w, ffn_conv_b, ffn_w_down):
    P = _prepare(dict(
        norm_g=norm_g, sconv_w_in=sconv_w_in, sconv_conv_w=sconv_conv_w, sconv_w_out=sconv_w_out,
        rwkv_mu=rwkv_mu, rwkv_w_rkv=rwkv_w_rkv, rwkv_w0=rwkv_w0, rwkv_w1=rwkv_w1, rwkv_w2=rwkv_w2,
        rwkv_a0=rwkv_a0, rwkv_a1=rwkv_a1, rwkv_a2=rwkv_a2, rwkv_g1=rwkv_g1, rwkv_g2=rwkv_g2,
        rwkv_k_k=rwkv_k_k, rwkv_k_a=rwkv_k_a, rwkv_r_k=rwkv_r_k, rwkv_lnx_w=rwkv_lnx_w,
        rwkv_lnx_b=rwkv_lnx_b, rwkv_w_o=rwkv_w_o,
        gla_w_in=gla_w_in, gla_wa1=gla_wa1, gla_wa2=gla_wa2, gla_ba=gla_ba,
        gla_onorm_g=gla_onorm_g, gla_w_o=gla_w_o,
        ffn_w_up=ffn_w_up, ffn_conv_w=ffn_conv_w, ffn_conv_b=ffn_conv_b, ffn_w_down=ffn_w_down))
    bp, bs = x_prompt.shape[0], x_sample.shape[0]
    mod_p, mod_s = _modulation(c_prompt, c_sample, ada_w, ada_b)
    mod_p = mod_p.reshape(DEPTH, bp, N_MOD, D)
    mod_s = mod_s.reshape(DEPTH, 1, bs, N_MOD * D)
    ffn_wts = []
    y_s, ca_s, sb_s, wb_s, gc_s, cf_s = _trunk(
        x_sample.reshape(1, bs, D), mod_s, P,
        (state_conv_a, state_shift_b, state_wkv_b, state_gla_c, state_conv_ffn), ffn_wts, sample=True)
    y_p, ca_p, sb_p, wb_p, gc_p, cf_p = _trunk(x_prompt, mod_p, P, (None,) * 5, ffn_wts, sample=False)
    return (y_p, y_s.reshape(bs, 1, D), ca_p, ca_s, sb_p, sb_s, wb_p, wb_s, gc_p, gc_s, cf_p, cf_s)
```
